```python
import jax
import jax.numpy as jnp
from jax import lax
import numpy as np

D_MODEL = 2048
BATCH = 1
SEQ = 16384
DEPTH = 4
DEC_BATCH = 8
DEC_SEQ = 64
PAST_LEN = 4096

CHUNK = 64
N_MIXERS = 4
D_FF = 5632
FFN_RES = 0.5
EPS = 1e-6
NEG = -1e30

POOL_WINDOWS = (2, 4, 8, 16)
POOL_GROUPS = len(POOL_WINDOWS)
POOL_GW = D_MODEL // POOL_GROUPS
POOL_HIST = max(POOL_WINDOWS) - 1

D_RNN = D_MODEL
LRU_BLOCKS = 8
LRU_BW = D_RNN // LRU_BLOCKS
CONV_W = 4
LRU_C = 8.0

FOX_HEADS = 16
FOX_HD = D_MODEL // FOX_HEADS
Q_BLOCK = 128

ML_HEADS = 8
ML_DV = D_MODEL // ML_HEADS
ML_DK = ML_DV // 2
ML_SPLITS = (ML_HEADS * ML_DK, 2 * ML_HEADS * ML_DK, 2 * ML_HEADS * ML_DK + ML_HEADS * ML_DV,
             2 * ML_HEADS * ML_DK + ML_HEADS * ML_DV + D_MODEL,
             2 * ML_HEADS * ML_DK + ML_HEADS * ML_DV + D_MODEL + ML_HEADS)
ML_IN = ML_SPLITS[-1] + ML_HEADS

kernel_name = "hybrid_streaming_encoder_step"

F32 = jnp.float32


def _rmsnorm(x, g):
    xf = x.astype(F32)
    y = xf * lax.rsqrt(jnp.mean(xf * xf, axis=-1, keepdims=True) + EPS)
    return (y * g.astype(F32)).astype(x.dtype)


def _ffn_half(x, g, wg, wu, wd):
    h = _rmsnorm(x, g)
    return x + (FFN_RES * ((jax.nn.silu(h @ wg) * (h @ wu)) @ wd)).astype(x.dtype)


def _pool_mixer(h, hist, pos0, w_pool, scale):
    B, T, _ = h.shape
    xe = jnp.concatenate([hist.astype(h.dtype), h], axis=1)
    cs = jnp.pad(jnp.cumsum(xe.astype(F32), axis=1), ((0, 0), (1, 0), (0, 0)))
    top = cs[:, POOL_HIST + 1:POOL_HIST + 1 + T]
    pos = (pos0 + jnp.arange(T)).astype(F32)
    groups = []
    for g, w in enumerate(POOL_WINDOWS):
        lo = POOL_HIST + 1 - w
        sl = slice(g * POOL_GW, (g + 1) * POOL_GW)
        cnt = jnp.minimum(pos + 1.0, float(w))[None, :, None]
        groups.append((top[..., sl] - cs[:, lo:lo + T, sl]) / cnt)
    pooled = jnp.stack(groups, axis=2) - h.astype(F32).reshape(B, T, POOL_GROUPS, POOL_GW)
    y = jnp.einsum('btgc,gcd->btgd', pooled, w_pool.astype(F32)).reshape(B, T, D_MODEL)
    y = y * scale.astype(F32)
    return y.astype(h.dtype), xe[:, -POOL_HIST:]


def _lru_combine(lhs, rhs):
    a1, b1 = lhs
    a2, b2 = rhs
    return a1 * a2, a2 * b1 + b2


def _rglru_mixer(h, conv_state, h0, w_in, conv_w, conv_b, w_a, b_a, w_i, b_i, lam, w_out):
    B, T, _ = h.shape
    gate, xr = jnp.split(h @ w_in, 2, axis=-1)
    xe = jnp.concatenate([conv_state.astype(xr.dtype), xr], axis=1)
    xc = conv_b + xe[:, 0:T] * conv_w[0]
    for j in range(1, CONV_W):
        xc = xc + xe[:, j:j + T] * conv_w[j]
    xb = xc.reshape(B, T, LRU_BLOCKS, LRU_BW)
    r = jax.nn.sigmoid((jnp.einsum('btnc,ncd->btnd', xb, w_a).reshape(B, T, D_RNN) + b_a).astype(F32))
    i = jax.nn.sigmoid((jnp.einsum('btnc,ncd->btnd', xb, w_i).reshape(B, T, D_RNN) + b_i).astype(F32))
    log_a = -LRU_C * r * jax.nn.softplus(-lam.astype(F32))
    a = jnp.exp(log_a)
    u = jnp.sqrt(-jnp.expm1(2.0 * log_a)) * i * xc.astype(F32)
    A, hs = lax.associative_scan(_lru_combine, (a, u), axis=1)
    hs = hs + A * h0.astype(F32)[:, None]
    y = (hs * jax.nn.gelu(gate.astype(F32))).astype(h.dtype) @ w_out
    return y.astype(h.dtype), xe[:, -(CONV_W - 1):], hs[:, -1]


def _fox_project(h, w_qkv, w_f, b_f):
    B, T, _ = h.shape
    q, k, v = jnp.split(h @ w_qkv, 3, axis=-1)
    shp = (B, T, FOX_HEADS, FOX_HD)
    lf = jax.nn.log_sigmoid((h @ w_f + b_f).astype(F32))
    return q.reshape(shp), k.reshape(shp), v.reshape(shp), lf


def _fox_block(q, cq, qpos, k, v, ck, kpos):
    s = jnp.einsum('bqhd,bkhd->bhqk', q, k) * (FOX_HD ** -0.5)
    s = s + jnp.swapaxes(cq, 1, 2)[..., None] - jnp.swapaxes(ck, 1, 2)[:, :, None, :]
    s = jnp.where((kpos[None, :] <= qpos[:, None])[None, None], s, NEG)
    p = jax.nn.softmax(s, axis=-1)
    return jnp.einsum('bhqk,bkhd->bqhd', p, v)


def _fox_attend(q, k_all, v_all, lf_all):
    B, T = q.shape[:2]
    S = k_all.shape[1]
    P = S - T
    c = jnp.cumsum(lf_all.astype(F32), axis=1)
    qf, kf, vf = q.astype(F32), k_all.astype(F32), v_all.astype(F32)
    kpos = jnp.arange(S)
    qpos = P + jnp.arange(T)
    cq = c[:, P:]
    if T > Q_BLOCK and T % Q_BLOCK == 0:
        nb = T // Q_BLOCK
        qb = jnp.swapaxes(qf.reshape(B, nb, Q_BLOCK, FOX_HEADS, FOX_HD), 0, 1)
        cqb = jnp.swapaxes(cq.reshape(B, nb, Q_BLOCK, FOX_HEADS), 0, 1)
        pb = qpos.reshape(nb, Q_BLOCK)
        ob = lax.map(lambda blk: _fox_block(blk[0], blk[1], blk[2], kf, vf, c, kpos), (qb, cqb, pb))
        o = jnp.swapaxes(ob, 0, 1).reshape(B, T, FOX_HEADS, FOX_HD)
    else:
        o = _fox_block(qf, cq, qpos, kf, vf, c, kpos)
    return o.reshape(B, T, D_MODEL)


def _mlstm_chunk(carry, xs):
    C, n, m = carry
    q, k, v, ig, lf = xs
    L = q.shape[1]
    b = jnp.cumsum(lf, axis=1)
    g = ig - b
    mt = b + jnp.maximum(m[:, None], lax.cummax(g, axis=1))
    causal = jnp.tril(jnp.ones((L, L), dtype=bool))
    logd = b[:, :, None] + g[:, None, :] - mt[:, :, None]
    d = jnp.exp(jnp.where(causal[None, :, :, None], logd, NEG))
    sc = jnp.einsum('bthd,bshd->btsh', q, k) * d
    inter = jnp.exp(b + m[:, None] - mt)
    num = jnp.einsum('btsh,bshv->bthv', sc, v) + inter[..., None] * jnp.einsum('bthd,bhdv->bthv', q, C)
    den = jnp.sum(sc, axis=2) + inter * jnp.einsum('bthd,bhd->bth', q, n)
    hout = num / jnp.maximum(jnp.abs(den), jnp.exp(-mt))[..., None]
    m_new = mt[:, -1]
    decay = jnp.exp(b[:, -1] + m - m_new)
    w = jnp.exp(b[:, -1:] + g - m_new[:, None])
    C_new = decay[..., None, None] * C + jnp.einsum('bsh,bshd,bshv->bhdv', w, k, v)
    n_new = decay[..., None] * n + jnp.einsum('bsh,bshd->bhd', w, k)
    return (C_new, n_new, m_new), hout


def _mlstm_mixer(h, C0, n0, m0, w_in, b_i, b_f, g_norm, w_out):
    B, T, _ = h.shape
    q, k, v, o, ig, fg = jnp.split(h @ w_in, list(ML_SPLITS), axis=-1)
    q = q.reshape(B, T, ML_HEADS, ML_DK).astype(F32) * (ML_DK ** -0.5)
    k = k.reshape(B, T, ML_HEADS, ML_DK).astype(F32)
    v = v.reshape(B, T, ML_HEADS, ML_DV).astype(F32)
    ig = ig.astype(F32) + b_i.astype(F32)
    lf = jax.nn.log_sigmoid(fg.astype(F32) + b_f.astype(F32))
    L = CHUNK if T % CHUNK == 0 else T
    nc = T // L

    def blocks(a):
        return jnp.swapaxes(a.reshape((B, nc, L) + a.shape[2:]), 0, 1)

    carry0 = (C0.astype(F32), n0.astype(F32), m0.astype(F32))
    (C, n, m), hb = lax.scan(_mlstm_chunk, carry0, (blocks(q), blocks(k), blocks(v), blocks(ig), blocks(lf)))
    hs = jnp.swapaxes(hb, 0, 1).reshape(B, T, ML_HEADS, ML_DV)
    hs = hs * lax.rsqrt(jnp.mean(hs * hs, axis=-1, keepdims=True) + EPS)
    hs = hs.reshape(B, T, D_MODEL) * g_norm.astype(F32) * jax.nn.sigmoid(o.astype(F32))
    y = hs.astype(h.dtype) @ w_out
    return y.astype(h.dtype), C, n, m


def setup_inputs(seed: int = 0) -> dict:
    key = jax.random.key(seed)
    keys = iter(jax.random.split(key, 64))

    def nrm(shape, scale):
        return jax.random.normal(next(keys), shape, F32) * scale

    def gain(shape):
        return 1.0 + 0.05 * jax.random.normal(next(keys), shape, F32)

    D = D_MODEL
    u = jax.random.uniform(next(keys), (D_RNN,), F32, minval=0.9, maxval=0.999)
    s = u ** (1.0 / LRU_C)
    lru_lambda = jnp.log(s) - jnp.log1p(-s)
    return {
        "x_prompt": nrm((BATCH, SEQ, D), 1.0),
        "x_sample": nrm((DEC_BATCH, DEC_SEQ, D), 1.0),
        "state_pool": nrm((DEC_BATCH, POOL_HIST, D), 1.0),
        "state_lru_conv": nrm((DEC_BATCH, CONV_W - 1, D_RNN), 1.0),
        "state_lru_h": nrm((DEC_BATCH, D_RNN), 0.5),
        "cache_fox_k": nrm((DEC_BATCH, PAST_LEN, FOX_HEADS, FOX_HD), 1.0),
        "cache_fox_v": nrm((DEC_BATCH, PAST_LEN, FOX_HEADS, FOX_HD), 1.0),
        "cache_fox_logf": jax.nn.log_sigmoid(3.0 + nrm((DEC_BATCH, PAST_LEN, FOX_HEADS), 1.0)),
        "state_mlstm_c": nrm((DEC_BATCH, ML_HEADS, ML_DK, ML_DV), 1.0),
        "state_mlstm_n": nrm((DEC_BATCH, ML_HEADS, ML_DK), 1.0),
        "state_mlstm_m": nrm((DEC_BATCH, ML_HEADS), 1.0),
        "ffn1_norm": gain((DEPTH, D)),
        "ffn1_w_gate": nrm((DEPTH, D, D_FF), D ** -0.5),
        "ffn1_w_up": nrm((DEPTH, D, D_FF), D ** -0.5),
        "ffn1_w_down": nrm((DEPTH, D_FF, D), D_FF ** -0.5),
        "mix_norm": gain((DEPTH, D)),
        "ffn2_norm": gain((DEPTH, D)),
        "ffn2_w_gate": nrm((DEPTH, D, D_FF), D ** -0.5),
        "ffn2_w_up": nrm((DEPTH, D, D_FF), D ** -0.5),
        "ffn2_w_down": nrm((DEPTH, D_FF, D), D_FF ** -0.5),
        "pool_w": nrm((POOL_GROUPS, POOL_GW, POOL_GW), POOL_GW ** -0.5),
        "pool_scale": gain((D,)),
        "lru_w_in": nrm((D, 2 * D_RNN), D ** -0.5),
        "lru_conv_w": nrm((CONV_W, D_RNN), CONV_W ** -0.5),
        "lru_conv_b": nrm((D_RNN,), 0.02),
        "lru_w_a": nrm((LRU_BLOCKS, LRU_BW, LRU_BW), LRU_BW ** -0.5),
        "lru_b_a": nrm((D_RNN,), 0.02),
        "lru_w_i": nrm((LRU_BLOCKS, LRU_BW, LRU_BW), LRU_BW ** -0.5),
        "lru_b_i": nrm((D_RNN,), 0.02),
        "lru_lambda": lru_lambda,
        "lru_w_out": nrm((D_RNN, D), D_RNN ** -0.5),
        "fox_w_qkv": nrm((D, 3 * D), D ** -0.5),
        "fox_w_f": nrm((D, FOX_HEADS), D ** -0.5),
        "fox_b_f": jnp.linspace(2.0, 6.0, FOX_HEADS, dtype=F32) + nrm((FOX_HEADS,), 0.1),
        "fox_w_o": nrm((D, D), D ** -0.5),
        "mlstm_w_in": nrm((D, ML_IN), D ** -0.5),
        "mlstm_b_i": nrm((ML_HEADS,), 0.1),
        "mlstm_b_f": jnp.linspace(3.0, 6.0, ML_HEADS, dtype=F32) + nrm((ML_HEADS,), 0.1),
        "mlstm_norm": gain((D,)),
        "mlstm_w_out": nrm((D, D), D ** -0.5),
        "final_norm": gain((D,)),
    }


def reference(x_prompt, x_sample, state_pool, state_lru_conv, state_lru_h,
              cache_fox_k, cache_fox_v, cache_fox_logf,
              state_mlstm_c, state_mlstm_n, state_mlstm_m,
              ffn1_norm, ffn1_w_gate, ffn1_w_up, ffn1_w_down,
              mix_norm,
              ffn2_norm, ffn2_w_gate, ffn2_w_up, ffn2_w_down,
              pool_w, pool_scale,
              lru_w_in, lru_conv_w, lru_conv_b, lru_w_a, lru_b_a, lru_w_i, lru_b_i, lru_lambda, lru_w_out,
              fox_w_qkv, fox_w_f, fox_b_f, fox_w_o,
              mlstm_w_in, mlstm_b_i, mlstm_b_f, mlstm_norm, mlstm_w_out,
              final_norm):
    bp = x_prompt.shape[0]
    xp, xs = x_prompt, x_sample
    for layer in range(DEPTH):
        xp = _ffn_half(xp, ffn1_norm[layer], ffn1_w_gate[layer], ffn1_w_up[layer], ffn1_w_down[layer])
        xs = _ffn_half(xs, ffn1_norm[layer], ffn1_w_gate[layer], ffn1_w_up[layer], ffn1_w_down[layer])
        hp = _rmsnorm(xp, mix_norm[layer])
        hs = _rmsnorm(xs, mix_norm[layer])
        kind = layer % N_MIXERS
        if kind == 0:
            mp, pool_p = _pool_mixer(hp, jnp.zeros((bp, POOL_HIST, D_MODEL), hp.dtype), 0, pool_w, pool_scale)
            ms, pool_s = _pool_mixer(hs, state_pool, PAST_LEN, pool_w, pool_scale)
        elif kind == 1:
            mp, lru_conv_p, lru_h_p = _rglru_mixer(
                hp, jnp.zeros((bp, CONV_W - 1, D_RNN), hp.dtype), jnp.zeros((bp, D_RNN), F32),
                lru_w_in, lru_conv_w, lru_conv_b, lru_w_a, lru_b_a, lru_w_i, lru_b_i, lru_lambda, lru_w_out)
            ms, lru_conv_s, lru_h_s = _rglru_mixer(
                hs, state_lru_conv, state_lru_h,
                lru_w_in, lru_conv_w, lru_conv_b, lru_w_a, lru_b_a, lru_w_i, lru_b_i, lru_lambda, lru_w_out)
        elif kind == 2:
            qp, fox_k_p, fox_v_p, fox_lf_p = _fox_project(hp, fox_w_qkv, fox_w_f, fox_b_f)
            mp = _fox_attend(qp, fox_k_p, fox_v_p, fox_lf_p).astype(hp.dtype) @ fox_w_o
            qs, fox_k_s, fox_v_s, fox_lf_s = _fox_project(hs, fox_w_qkv, fox_w_f, fox_b_f)
            o_s = _fox_attend(qs,
                              jnp.concatenate([cache_fox_k.astype(fox_k_s.dtype), fox_k_s], axis=1),
                              jnp.concatenate([cache_fox_v.astype(fox_v_s.dtype), fox_v_s], axis=1),
                              jnp.concatenate([cache_fox_logf.astype(F32), fox_lf_s], axis=1))
            ms = o_s.astype(hs.dtype) @ fox_w_o
        else:
            mp, ml_c_p, ml_n_p, ml_m_p = _mlstm_mixer(
                hp, jnp.zeros((bp, ML_HEADS, ML_DK, ML_DV), F32), jnp.zeros((bp, ML_HEADS, ML_DK), F32),
                jnp.zeros((bp, ML_HEADS), F32), mlstm_w_in, mlstm_b_i, mlstm_b_f, mlstm_norm, mlstm_w_out)
            ms, ml_c_s, ml_n_s, ml_m_s = _mlstm_mixer(
                hs, state_mlstm_c, state_mlstm_n, state_mlstm_m,
                mlstm_w_in, mlstm_b_i, mlstm_b_f, mlstm_norm, mlstm_w_out)
        xp = xp + mp.astype(xp.dtype)
        xs = xs + ms.astype(xs.dtype)
        xp = _ffn_half(xp, ffn2_norm[layer], ffn2_w_gate[layer], ffn2_w_up[layer], ffn2_w_down[layer])
        xs = _ffn_half(xs, ffn2_norm[layer], ffn2_w_gate[layer], ffn2_w_up[layer], ffn2_w_down[layer])
    y_prompt = _rmsnorm(xp, final_norm)
    y_sample = _rmsnorm(xs, final_norm)
    return (y_prompt, y_sample, pool_p, pool_s, lru_conv_p, lru_conv_s, lru_h_p, lru_h_s,
            fox_k_p, fox_k_s, fox_v_p, fox_v_s, fox_lf_p, fox_lf_s,
            ml_c_p, ml_c_s, ml_n_p, ml_n_s, ml_m_p, ml_m_s)
```

```python
import functools
import math

import jax
import jax.numpy as jnp
from jax import lax
from jax.experimental import pallas as pl
from jax.experimental.pallas import tpu as pltpu

F32 = jnp.float32
BF = jnp.bfloat16
HIGHEST = lax.Precision.HIGHEST

D = 2048
D_FF = 5632
EPS = 1e-6
NEG = -1e30
LOG2E = math.log2(math.e)

POOL_WINDOWS = (2, 4, 8, 16)
POOL_GW = D // len(POOL_WINDOWS)
POOL_PAD = 16
CONV_W = 4
CONV_PAD = 8
LRU_BLOCKS = 8
LRU_BW = D // LRU_BLOCKS
LRU_C = 8.0
FOX_HEADS = 16
FOX_HD = D // FOX_HEADS
ML_HEADS = 8
ML_DV = D // ML_HEADS
ML_DK = ML_DV // 2
ML_QK = ML_HEADS * ML_DK

V7X_VMEM_LIMIT = 56 * 1024 * 1024


def _params(semantics, vmem_bytes):
    return pltpu.CompilerParams(dimension_semantics=semantics,
                                vmem_limit_bytes=min(int(vmem_bytes), V7X_VMEM_LIMIT))


def _rms(xf, g):
    ms = jnp.mean(xf * xf, axis=-1, keepdims=True)
    return xf * lax.rsqrt(ms + EPS) * g


def _nt_dot(a, b):
    return lax.dot_general(a, b, (((1,), (1,)), ((), ())), preferred_element_type=F32)


def _log_sigmoid(z):
    return jnp.minimum(z, 0.0) - jnp.log1p(jnp.exp(-jnp.abs(z)))


def _tri(n, kind, seg=None):
    a = lax.broadcasted_iota(jnp.int32, (n, n), 0)
    b = lax.broadcasted_iota(jnp.int32, (n, n), 1)
    if kind == "le":
        m = b <= a
        if seg is not None:
            m = jnp.logical_and(m, b >= jnp.bitwise_and(a, -seg))
    elif kind == "ge":
        m = b >= a
        if seg is not None:
            m = jnp.logical_and(m, a >= jnp.bitwise_and(b, -seg))
    else:
        m = a > b
    return jnp.where(m, 1.0, 0.0).astype(F32)


def _ffn_body(x_ref, g_ref, wg_ref, wu_ref, wd_ref, o_ref, h_ref, *, nj):
    j = pl.program_id(1)

    @pl.when(j == 0)
    def _():
        h_ref[...] = _rms(x_ref[...], g_ref[...]).astype(BF)

    h = h_ref[...]
    gt = jnp.dot(h, wg_ref[...], preferred_element_type=F32)
    up = jnp.dot(h, wu_ref[...], preferred_element_type=F32)
    a = (gt * jax.nn.sigmoid(gt) * up).astype(BF)
    c = jnp.dot(a, wd_ref[...], preferred_element_type=F32)

    @pl.when(j == 0)
    def _():
        o_ref[...] = c

    @pl.when(j > 0)
    def _():
        o_ref[...] += c

    @pl.when(j == nj - 1)
    def _():
        o_ref[...] = x_ref[...] + 0.5 * o_ref[...]


def _ffn(x, g, wg, wu, wd, *, tm=512, tf=512):
    R = x.shape[0]
    nj = D_FF // tf
    vmem = 4 * tm * D * 4 + tm * D * 2 + 6 * D * tf * 2 + 4 * tm * tf * 4 + 2 * tm * D * 4
    return pl.pallas_call(
        functools.partial(_ffn_body, nj=nj),
        grid=(R // tm, nj),
        in_specs=[
            pl.BlockSpec((tm, D), lambda i, j: (i, 0)),
            pl.BlockSpec((1, D), lambda i, j: (0, 0)),
            pl.BlockSpec((D, tf), lambda i, j: (0, j)),
            pl.BlockSpec((D, tf), lambda i, j: (0, j)),
            pl.BlockSpec((tf, D), lambda i, j: (j, 0)),
        ],
        out_specs=pl.BlockSpec((tm, D), lambda i, j: (i, 0)),
        out_shape=jax.ShapeDtypeStruct((R, D), F32),
        scratch_shapes=[pltpu.VMEM((tm, D), BF)],
        compiler_params=_params(("arbitrary", "arbitrary"), vmem),
        name="ffn",
    )(x, g, wg, wu, wd)


def _mm_res_body(a_ref, w_ref, x_ref, o_ref):
    o_ref[...] = x_ref[...] + jnp.dot(a_ref[...], w_ref[...], preferred_element_type=F32)


def _mm_res(a, w, x, *, tm=512):
    R = x.shape[0]
    vmem = 2 * tm * D * 2 + 2 * D * D * 2 + 5 * tm * D * 4
    return pl.pallas_call(
        _mm_res_body,
        grid=(R // tm,),
        in_specs=[
            pl.BlockSpec((tm, D), lambda i: (i, 0)),
            pl.BlockSpec((D, D), lambda i: (0, 0)),
            pl.BlockSpec((tm, D), lambda i: (i, 0)),
        ],
        out_specs=pl.BlockSpec((tm, D), lambda i: (i, 0)),
        out_shape=jax.ShapeDtypeStruct((R, D), F32),
        compiler_params=_params(("arbitrary",), vmem),
        name="mm_res",
    )(a, w, x)


def _nmm_body(x_ref, g_ref, w_ref, o_ref, h_ref):
    @pl.when(pl.program_id(1) == 0)
    def _():
        h_ref[...] = _rms(x_ref[...], g_ref[...]).astype(BF)

    o_ref[...] = jnp.dot(h_ref[...], w_ref[...], preferred_element_type=F32)


def _nmm(x, g, w, *, tm=512, tn=2048):
    R = x.shape[0]
    N = w.shape[1]
    vmem = 2 * tm * D * 4 + tm * D * 2 + 2 * D * tn * 2 + 3 * tm * tn * 4 + tm * D * 4
    return pl.pallas_call(
        _nmm_body,
        grid=(R // tm, N // tn),
        in_specs=[
            pl.BlockSpec((tm, D), lambda i, j: (i, 0)),
            pl.BlockSpec((1, D), lambda i, j: (0, 0)),
            pl.BlockSpec((D, tn), lambda i, j: (0, j)),
        ],
        out_specs=pl.BlockSpec((tm, tn), lambda i, j: (i, j)),
        out_shape=jax.ShapeDtypeStruct((R, N), F32),
        scratch_shapes=[pltpu.VMEM((tm, D), BF)],
        compiler_params=_params(("arbitrary", "arbitrary"), vmem),
        name="norm_matmul",
    )(x, g, w)


def _norm_body(x_ref, g_ref, o_ref):
    o_ref[...] = _rms(x_ref[...], g_ref[...])


def _norm(x, g, *, tm=512):
    R = x.shape[0]
    return pl.pallas_call(
        _norm_body,
        grid=(R // tm,),
        in_specs=[pl.BlockSpec((tm, D), lambda i: (i, 0)), pl.BlockSpec((1, D), lambda i: (0, 0))],
        out_specs=pl.BlockSpec((tm, D), lambda i: (i, 0)),
        out_shape=jax.ShapeDtypeStruct((R, D), F32),
        compiler_params=_params(("arbitrary",), 8 * tm * D * 4),
        name="final_norm",
    )(x, g)


def _pool_body(x_ref, g_ref, hist_ref, w_ref, sc_ref, o_ref, st_ref, xe_ref, *, tm, pos0):
    t = pl.program_id(1)

    @pl.when(t == 0)
    def _():
        xe_ref[0:POOL_PAD, :] = hist_ref[0]

    xf = x_ref[...]
    h = _rms(xf, g_ref[...])
    xe_ref[POOL_PAD:POOL_PAD + tm, :] = h
    row = lax.broadcasted_iota(jnp.int32, (tm, POOL_GW), 0)
    pos1 = (row + (pos0 + 1) + t * tm).astype(F32)
    for gi, w in enumerate(POOL_WINDOWS):
        c0 = gi * POOL_GW
        hg = h[:, c0:c0 + POOL_GW]
        s = hg
        for back in range(1, w):
            s = s + xe_ref[POOL_PAD - back:POOL_PAD - back + tm, c0:c0 + POOL_GW]
        cnt = jnp.minimum(pos1, float(w))
        pooled = s / cnt - hg
        y = jnp.dot(pooled.astype(BF), w_ref[gi], preferred_element_type=F32)
        o_ref[:, c0:c0 + POOL_GW] = xf[:, c0:c0 + POOL_GW] + y * sc_ref[:, c0:c0 + POOL_GW]
    tail = xe_ref[tm:tm + POOL_PAD, :]
    st_ref[0] = tail
    xe_ref[0:POOL_PAD, :] = tail


def _pool_mixer(x, g, hist, w, scale, *, B, T, tm, pos0):
    nt = T // tm
    vmem = 4 * tm * D * 4 + (tm + POOL_PAD) * D * 4 + 2 * 4 * POOL_GW * POOL_GW * 2 + 6 * tm * D * 4
    return pl.pallas_call(
        functools.partial(_pool_body, tm=tm, pos0=pos0),
        grid=(B, nt),
        in_specs=[
            pl.BlockSpec((tm, D), lambda b, t: (b * nt + t, 0)),
            pl.BlockSpec((1, D), lambda b, t: (0, 0)),
            pl.BlockSpec((1, POOL_PAD, D), lambda b, t: (b, 0, 0)),
            pl.BlockSpec((len(POOL_WINDOWS), POOL_GW, POOL_GW), lambda b, t: (0, 0, 0)),
            pl.BlockSpec((1, D), lambda b, t: (0, 0)),
        ],
        out_specs=[
            pl.BlockSpec((tm, D), lambda b, t: (b * nt + t, 0)),
            pl.BlockSpec((1, POOL_PAD, D), lambda b, t: (b, 0, 0)),
        ],
        out_shape=[jax.ShapeDtypeStruct((B * T, D), F32), jax.ShapeDtypeStruct((B, POOL_PAD, D), F32)],
        scratch_shapes=[pltpu.VMEM((tm + POOL_PAD, D), F32)],
        compiler_params=_params(("arbitrary", "arbitrary"), vmem),
        name="pool_mixer",
    )(x, g, hist, w, scale)


def _lru_body(gate_ref, xr_ref, cst_ref, h0_ref, cw_ref, cb_ref, wa_ref, ba_ref, wi_ref, bi_ref, lam_ref,
              y_ref, cso_ref, ho_ref, xe_ref, a_ref, u_ref, hc_ref, *, tm):
    t = pl.program_id(1)

    @pl.when(t == 0)
    def _():
        xe_ref[0:CONV_PAD, :] = cst_ref[0]
        hc_ref[...] = h0_ref[0]

    xe_ref[CONV_PAD:CONV_PAD + tm, :] = xr_ref[...]
    base = CONV_PAD - (CONV_W - 1)
    xc = cb_ref[...] + xe_ref[base:base + tm, :] * cw_ref[0:1, :]
    for j in range(1, CONV_W):
        xc = xc + xe_ref[base + j:base + j + tm, :] * cw_ref[j:j + 1, :]
    tail = xe_ref[tm:tm + CONV_PAD, :]
    cso_ref[0] = tail
    xe_ref[0:CONV_PAD, :] = tail

    xcb = xc.astype(BF)
    nlam = -lam_ref[...]
    sp = jnp.maximum(nlam, 0.0) + jnp.log1p(jnp.exp(-jnp.abs(nlam)))
    for n in range(LRU_BLOCKS):
        blk = slice(n * LRU_BW, (n + 1) * LRU_BW)
        ra = jnp.dot(xcb[:, blk], wa_ref[n], preferred_element_type=F32) + ba_ref[:, blk]
        ia = jnp.dot(xcb[:, blk], wi_ref[n], preferred_element_type=F32) + bi_ref[:, blk]
        r = jax.nn.sigmoid(ra)
        ig = jax.nn.sigmoid(ia)
        log_a = -LRU_C * r * sp[:, blk]
        a_ref[:, blk] = jnp.exp(log_a)
        th = jnp.tanh(log_a)
        one_minus_a2 = -2.0 * th / (1.0 - th)
        u_ref[:, blk] = jnp.sqrt(one_minus_a2) * ig * xc[:, blk]

    rowi = lax.broadcasted_iota(jnp.int32, (8, D), 0)

    def group(gi, carry):
        r0 = pl.multiple_of(gi * 8, 8)
        a8 = a_ref[pl.ds(r0, 8), :]
        u8 = u_ref[pl.ds(r0, 8), :]
        for d in (1, 2, 4):
            keep = rowi >= d
            u8 = jnp.where(keep, a8 * pltpu.roll(u8, d, axis=0) + u8, u8)
            a8 = jnp.where(keep, a8 * pltpu.roll(a8, d, axis=0), a8)
        hs8 = a8 * carry + u8
        u_ref[pl.ds(r0, 8), :] = hs8
        return hs8[7:8, :]

    carry = lax.fori_loop(0, tm // 8, group, hc_ref[...])
    hc_ref[...] = carry
    ho_ref[0] = carry
    y_ref[...] = (u_ref[...] * jax.nn.gelu(gate_ref[...])).astype(BF)


def _lru_core(gx, cst, h0, cw, cb, wa, ba, wi, bi, lam, *, B, T, tm):
    nt = T // tm
    vmem = 4 * tm * D * 4 + 2 * tm * D * 2 + (3 * tm + CONV_PAD) * D * 4 + 4 * 8 * LRU_BW * LRU_BW * 2 + 6 * tm * D * 4
    vec = pl.BlockSpec((1, D), lambda b, t: (0, 0))
    wblk = pl.BlockSpec((LRU_BLOCKS, LRU_BW, LRU_BW), lambda b, t: (0, 0, 0))
    return pl.pallas_call(
        functools.partial(_lru_body, tm=tm),
        grid=(B, nt),
        in_specs=[
            pl.BlockSpec((tm, D), lambda b, t: (b * nt + t, 0)),
            pl.BlockSpec((tm, D), lambda b, t: (b * nt + t, 1)),
            pl.BlockSpec((1, CONV_PAD, D), lambda b, t: (b, 0, 0)),
            pl.BlockSpec((1, 1, D), lambda b, t: (b, 0, 0)),
            pl.BlockSpec((CONV_W, D), lambda b, t: (0, 0)),
            vec, wblk, vec, wblk, vec, vec,
        ],
        out_specs=[
            pl.BlockSpec((tm, D), lambda b, t: (b * nt + t, 0)),
            pl.BlockSpec((1, CONV_PAD, D), lambda b, t: (b, 0, 0)),
            pl.BlockSpec((1, 1, D), lambda b, t: (b, 0, 0)),
        ],
        out_shape=[
            jax.ShapeDtypeStruct((B * T, D), BF),
            jax.ShapeDtypeStruct((B, CONV_PAD, D), F32),
            jax.ShapeDtypeStruct((B, 1, D), F32),
        ],
        scratch_shapes=[
            pltpu.VMEM((tm + CONV_PAD, D), F32),
            pltpu.VMEM((tm, D), F32),
            pltpu.VMEM((tm, D), F32),
            pltpu.VMEM((1, D), F32),
        ],
        compiler_params=_params(("arbitrary", "arbitrary"), vmem),
        name="rglru_core",
    )(gx, gx, cst, h0, cw, cb, wa, ba, wi, bi, lam)


def _fox_proj_body(x_ref, g_ref, w_ref, wf_ref, wft_ref, bf_ref, bft_ref,
                   q_ref, k_ref, v_ref, kb_ref, vb_ref, lf_ref, cc_ref, cr_ref,
                   h_ref, carc_ref, carr_ref, *, tm, seg, carry, qscale):
    i = pl.program_id(0)
    j = pl.program_id(1)

    @pl.when(j == 0)
    def _():
        h = _rms(x_ref[...], g_ref[...]).astype(BF)
        h_ref[...] = h
        lf = _log_sigmoid(jnp.dot(h, wf_ref[...], preferred_element_type=F32) + bf_ref[...])
        lft = _log_sigmoid(_nt_dot(wft_ref[...], h) + bft_ref[...])
        lf_ref[...] = lf
        cc = jnp.dot(_tri(tm, "le", seg), lf, precision=HIGHEST, preferred_element_type=F32)
        cr = jnp.dot(lft, _tri(tm, "ge", seg), precision=HIGHEST, preferred_element_type=F32)
        if carry:
            @pl.when(i == 0)
            def _():
                carc_ref[...] = jnp.zeros_like(carc_ref)
                carr_ref[...] = jnp.zeros_like(carr_ref)

            cc = cc + carc_ref[...]
            cr = cr + carr_ref[...]
            carc_ref[...] = cc[tm - 1:tm, :]
            carr_ref[...] = cr[:, tm - 1:tm]
        cc_ref[...] = cc
        cr_ref[...] = cr

    y = jnp.dot(h_ref[...], w_ref[...], preferred_element_type=F32)

    @pl.when(j == 0)
    def _():
        yq = (y * qscale).astype(BF)
        for hh in range(FOX_HEADS):
            q_ref[hh] = yq[:, hh * FOX_HD:(hh + 1) * FOX_HD]

    @pl.when(j == 1)
    def _():
        k_ref[...] = y
        yb = y.astype(BF)
        for hh in range(FOX_HEADS):
            kb_ref[hh] = yb[:, hh * FOX_HD:(hh + 1) * FOX_HD]

    @pl.when(j == 2)
    def _():
        v_ref[...] = y
        yb = y.astype(BF)
        for hh in range(FOX_HEADS):
            vb_ref[hh] = yb[:, hh * FOX_HD:(hh + 1) * FOX_HD]


def _fox_proj(x, g, w, wf, wft, bf, bft, *, tm, seg, carry):
    R = x.shape[0]
    H = FOX_HEADS
    qscale = (FOX_HD ** -0.5) * LOG2E
    vmem = (2 * tm * D * 4 + tm * D * 2 + 2 * D * D * 2 + 4 * tm * D * 4 + 6 * tm * D * 2
            + 3 * tm * tm * 4 + 3 * tm * D * 4)
    row = pl.BlockSpec((tm, D), lambda i, j: (i, 0))
    heads = pl.BlockSpec((H, tm, FOX_HD), lambda i, j: (0, i, 0))
    return pl.pallas_call(
        functools.partial(_fox_proj_body, tm=tm, seg=seg, carry=carry, qscale=qscale),
        grid=(R // tm, 3),
        in_specs=[
            row,
            pl.BlockSpec((1, D), lambda i, j: (0, 0)),
            pl.BlockSpec((D, D), lambda i, j: (0, j)),
            pl.BlockSpec((D, H), lambda i, j: (0, 0)),
            pl.BlockSpec((H, D), lambda i, j: (0, 0)),
            pl.BlockSpec((1, H), lambda i, j: (0, 0)),
            pl.BlockSpec((H, 1), lambda i, j: (0, 0)),
        ],
        out_specs=[
            heads, row, row, heads, heads,
            pl.BlockSpec((tm, H), lambda i, j: (i, 0)),
            pl.BlockSpec((tm, H), lambda i, j: (i, 0)),
            pl.BlockSpec((H, tm), lambda i, j: (0, i)),
        ],
        out_shape=[
            jax.ShapeDtypeStruct((H, R, FOX_HD), BF),
            jax.ShapeDtypeStruct((R, D), F32),
            jax.ShapeDtypeStruct((R, D), F32),
            jax.ShapeDtypeStruct((H, R, FOX_HD), BF),
            jax.ShapeDtypeStruct((H, R, FOX_HD), BF),
            jax.ShapeDtypeStruct((R, H), F32),
            jax.ShapeDtypeStruct((R, H), F32),
            jax.ShapeDtypeStruct((H, R), F32),
        ],
        scratch_shapes=[pltpu.VMEM((tm, D), BF), pltpu.VMEM((1, H), F32), pltpu.VMEM((H, 1), F32)],
        compiler_params=_params(("arbitrary", "arbitrary"), vmem),
        name="fox_proj",
    )(x, g, w, wf, wft, bf, bft)


def _softmax_step(s_biased, cq, v_bf, m_ref, l_ref, acc_ref):
    m_old = m_ref[...]
    m_new = jnp.maximum(m_old, jnp.max(s_biased, axis=-1, keepdims=True) + cq)
    p = jnp.exp2(s_biased - (m_new - cq))
    alpha = jnp.exp2(m_old - m_new)
    l_ref[...] = alpha * l_ref[...] + jnp.sum(p, axis=-1, keepdims=True)
    acc_ref[...] = alpha * acc_ref[...] + jnp.dot(p.astype(BF), v_bf, preferred_element_type=F32)
    m_ref[...] = m_new


def _fox_attn_body(q_ref, k_ref, v_ref, cc_ref, cr_ref, o_ref, m_ref, l_ref, acc_ref, *, tq):
    h = pl.program_id(0)
    qi = pl.program_id(1)
    q = q_ref[0]
    lane = lax.broadcasted_iota(jnp.int32, (tq, FOX_HEADS), 1)
    cq = jnp.sum(jnp.where(lane == h, cc_ref[...], 0.0), axis=1, keepdims=True) * LOG2E
    m_ref[...] = jnp.full_like(m_ref, NEG)
    l_ref[...] = jnp.zeros_like(l_ref)
    acc_ref[...] = jnp.zeros_like(acc_ref)

    def tile(kk, masked):
        k0 = pl.multiple_of(kk * tq, tq)
        ck = cr_ref[pl.ds(h, 1), pl.ds(k0, tq)] * LOG2E
        s = _nt_dot(q, k_ref[0, pl.ds(k0, tq), :]) - ck
        if masked:
            r = lax.broadcasted_iota(jnp.int32, (tq, tq), 0)
            c = lax.broadcasted_iota(jnp.int32, (tq, tq), 1)
            s = jnp.where(c <= r, s, NEG)
        _softmax_step(s, cq, v_ref[0, pl.ds(k0, tq), :], m_ref, l_ref, acc_ref)

    def body(kk, _):
        tile(kk, False)
        return 0

    lax.fori_loop(0, qi, body, 0)
    tile(qi, True)
    o_ref[...] = (acc_ref[...] / l_ref[...]).astype(BF)


def _fox_attn(qh, kh, vh, cc, cr, *, T, tq=512):
    H = FOX_HEADS
    vmem = 4 * T * FOX_HD * 2 + 2 * H * T * 4 + 8 * tq * tq * 4 + 16 * tq * FOX_HD * 4
    return pl.pallas_call(
        functools.partial(_fox_attn_body, tq=tq),
        grid=(H, T // tq),
        in_specs=[
            pl.BlockSpec((1, tq, FOX_HD), lambda h, i: (h, i, 0)),
            pl.BlockSpec((1, T, FOX_HD), lambda h, i: (h, 0, 0)),
            pl.BlockSpec((1, T, FOX_HD), lambda h, i: (h, 0, 0)),
            pl.BlockSpec((tq, H), lambda h, i: (i, 0)),
            pl.BlockSpec((H, T), lambda h, i: (0, 0)),
        ],
        out_specs=pl.BlockSpec((tq, FOX_HD), lambda h, i: (i, h)),
        out_shape=jax.ShapeDtypeStruct((T, D), BF),
        scratch_shapes=[pltpu.VMEM((tq, 1), F32), pltpu.VMEM((tq, 1), F32), pltpu.VMEM((tq, FOX_HD), F32)],
        compiler_params=_params(("arbitrary", "arbitrary"), vmem),
        name="fox_attention",
    )(qh, kh, vh, cc, cr)


def _suffix_body(x_ref, o_ref, car_ref, *, tc):
    @pl.when(pl.program_id(1) == 0)
    def _():
        car_ref[...] = jnp.zeros_like(car_ref)

    x = x_ref[0]
    o_ref[0] = jnp.dot(x, _tri(tc, "gt"), precision=HIGHEST, preferred_element_type=F32) + car_ref[...]
    car_ref[...] = car_ref[...] + jnp.sum(x, axis=1, keepdims=True)


def _suffix_sum(x, *, tc=512):
    B, H, S = x.shape
    nc = S // tc
    return pl.pallas_call(
        functools.partial(_suffix_body, tc=tc),
        grid=(B, nc),
        in_specs=[pl.BlockSpec((1, H, tc), lambda b, t: (b, 0, nc - 1 - t))],
        out_specs=pl.BlockSpec((1, H, tc), lambda b, t: (b, 0, nc - 1 - t)),
        out_shape=jax.ShapeDtypeStruct((B, H, S), F32),
        scratch_shapes=[pltpu.VMEM((H, 1), F32)],
        compiler_params=_params(("arbitrary", "arbitrary"), 8 * tc * tc * 4),
        name="suffix_sum",
    )(x)


def _fox_decode_body(q_ref, kn_ref, vn_ref, kc_ref, vc_ref, dsuf_ref, ec_ref, er_ref, o_ref,
                     m_ref, l_ref, acc_ref, *, nt, tq):
    t = pl.program_id(1)

    @pl.when(t == 0)
    def _():
        m_ref[...] = jnp.full_like(m_ref, NEG)
        l_ref[...] = jnp.zeros_like(l_ref)
        acc_ref[...] = jnp.zeros_like(acc_ref)

    ec = ec_ref[...] * LOG2E
    dsuf = dsuf_ref[0] * LOG2E
    for hh in range(FOX_HEADS):
        cols = slice(hh * FOX_HD, (hh + 1) * FOX_HD)
        s = _nt_dot(q_ref[hh], kc_ref[0, :, cols].astype(BF)) + dsuf[hh:hh + 1, :]
        _softmax_step(s, ec[:, hh:hh + 1], vc_ref[0, :, cols].astype(BF),
                      m_ref.at[hh], l_ref.at[hh], acc_ref.at[hh])

    @pl.when(t == nt - 1)
    def _():
        er = er_ref[0] * LOG2E
        r = lax.broadcasted_iota(jnp.int32, (tq, tq), 0)
        c = lax.broadcasted_iota(jnp.int32, (tq, tq), 1)
        for hh in range(FOX_HEADS):
            s = _nt_dot(q_ref[hh], kn_ref[hh]) - er[hh:hh + 1, :]
            s = jnp.where(c <= r, s, NEG)
            _softmax_step(s, ec[:, hh:hh + 1], vn_ref[hh], m_ref.at[hh], l_ref.at[hh], acc_ref.at[hh])
            o_ref[:, hh * FOX_HD:(hh + 1) * FOX_HD] = (acc_ref[hh] / l_ref[hh]).astype(BF)


def _fox_decode(qh, kh, vh, kc, vc, dsuf, ec, er, *, B, tq, ts=512):
    H = FOX_HEADS
    S = kc.shape[1]
    nt = S // ts
    vmem = 4 * ts * D * 4 + 6 * H * tq * FOX_HD * 2 + 3 * H * tq * 128 * 4 + 2 * ts * D * 2 + 8 * tq * ts * 4
    heads = pl.BlockSpec((H, tq, FOX_HD), lambda b, t: (0, b, 0))
    return pl.pallas_call(
        functools.partial(_fox_decode_body, nt=nt, tq=tq),
        grid=(B, nt),
        in_specs=[
            heads, heads, heads,
            pl.BlockSpec((1, ts, D), lambda b, t: (b, t, 0)),
            pl.BlockSpec((1, ts, D), lambda b, t: (b, t, 0)),
            pl.BlockSpec((1, H, ts), lambda b, t: (b, 0, t)),
            pl.BlockSpec((tq, H), lambda b, t: (b, 0)),
            pl.BlockSpec((1, H, tq), lambda b, t: (b, 0, 0)),
        ],
        out_specs=pl.BlockSpec((tq, D), lambda b, t: (b, 0)),
        out_shape=jax.ShapeDtypeStruct((B * tq, D), BF),
        scratch_shapes=[pltpu.VMEM((H, tq, 1), F32), pltpu.VMEM((H, tq, 1), F32), pltpu.VMEM((H, tq, FOX_HD), F32)],
        compiler_params=_params(("arbitrary", "arbitrary"), vmem),
        name="fox_decode_attention",
    )(qh, kh, vh, kc, vc, dsuf, ec, er)


def _mlstm_proj_body(x_ref, g_ref, w_ref, wg_ref, wgt_ref, bg_ref, bgt_ref,
                     q_ref, k_ref, v_ref, o_ref, gc_ref, gr_ref, h_ref, *, qscale):
    j = pl.program_id(1)

    @pl.when(j == 0)
    def _():
        h = _rms(x_ref[...], g_ref[...]).astype(BF)
        h_ref[...] = h
        zc = jnp.dot(h, wg_ref[...], preferred_element_type=F32) + bg_ref[...]
        lane = lax.broadcasted_iota(jnp.int32, zc.shape, 1)
        gc_ref[...] = jnp.where(lane < ML_HEADS, zc, _log_sigmoid(zc))
        zr = _nt_dot(wgt_ref[...], h) + bgt_ref[...]
        sub = lax.broadcasted_iota(jnp.int32, zr.shape, 0)
        gr_ref[...] = jnp.where(sub < ML_HEADS, zr, _log_sigmoid(zr))

    y = jnp.dot(h_ref[...], w_ref[...], preferred_element_type=F32)

    @pl.when(j == 0)
    def _():
        q_ref[...] = (y[:, :ML_QK] * qscale).astype(BF)
        k_ref[...] = y[:, ML_QK:]

    @pl.when(j == 1)
    def _():
        v_ref[...] = y.astype(BF)

    @pl.when(j == 2)
    def _():
        o_ref[...] = y


def _mlstm_proj(x, g, w, wg, wgt, bg, bgt, *, tm=512):
    R = x.shape[0]
    G = 2 * ML_HEADS
    vmem = 2 * tm * D * 4 + tm * D * 2 + 2 * D * D * 2 + 2 * tm * D * 4 + 8 * tm * D * 2 + 4 * tm * D * 4
    return pl.pallas_call(
        functools.partial(_mlstm_proj_body, qscale=ML_DK ** -0.5),
        grid=(R // tm, 3),
        in_specs=[
            pl.BlockSpec((tm, D), lambda i, j: (i, 0)),
            pl.BlockSpec((1, D), lambda i, j: (0, 0)),
            pl.BlockSpec((D, D), lambda i, j: (0, j)),
            pl.BlockSpec((D, G), lambda i, j: (0, 0)),
            pl.BlockSpec((G, D), lambda i, j: (0, 0)),
            pl.BlockSpec((1, G), lambda i, j: (0, 0)),
            pl.BlockSpec((G, 1), lambda i, j: (0, 0)),
        ],
        out_specs=[
            pl.BlockSpec((tm, ML_QK), lambda i, j: (i, 0)),
            pl.BlockSpec((tm, ML_QK), lambda i, j: (i, 0)),
            pl.BlockSpec((tm, D), lambda i, j: (i, 0)),
            pl.BlockSpec((tm, D), lambda i, j: (i, 0)),
            pl.BlockSpec((tm, G), lambda i, j: (i, 0)),
            pl.BlockSpec((G, tm), lambda i, j: (0, i)),
        ],
        out_shape=[
            jax.ShapeDtypeStruct((R, ML_QK), BF),
            jax.ShapeDtypeStruct((R, ML_QK), F32),
            jax.ShapeDtypeStruct((R, D), BF),
            jax.ShapeDtypeStruct((R, D), F32),
            jax.ShapeDtypeStruct((R, G), F32),
            jax.ShapeDtypeStruct((G, R), F32),
        ],
        scratch_shapes=[pltpu.VMEM((tm, D), BF)],
        compiler_params=_params(("arbitrary", "arbitrary"), vmem),
        name="mlstm_proj",
    )(x, g, w, wg, wgt, bg, bgt)


def _mlstm_body(q_ref, k_ref, v_ref, o_ref, gc_ref, gr_ref, gn_ref, c0_ref, n0_ref, m0_ref,
                y_ref, co_ref, no_ref, mo_ref, c_ref, n_ref, m_ref, *, L):
    t = pl.program_id(1)

    @pl.when(t == 0)
    def _():
        c_ref[...] = c0_ref[0]
        n_ref[...] = n0_ref[0]
        m_ref[...] = m0_ref[0]

    gc = gc_ref[...]
    gr = gr_ref[0]
    ra = lax.broadcasted_iota(jnp.int32, (L, L), 0)
    cb = lax.broadcasted_iota(jnp.int32, (L, L), 1)
    causal = cb <= ra
    bcs = jnp.dot(_tri(L, "le"), gc, precision=HIGHEST, preferred_element_type=F32)
    brs = jnp.dot(gr, _tri(L, "ge"), precision=HIGHEST, preferred_element_type=F32)
    for hh in range(ML_HEADS):
        b_c = bcs[:, ML_HEADS + hh:ML_HEADS + hh + 1]
        g_c = gc[:, hh:hh + 1] - b_c
        g_r = gr[hh:hh + 1, :] - brs[ML_HEADS + hh:ML_HEADS + hh + 1, :]
        m_h = m_ref[hh][:, 0:1]
        am = jnp.where(causal, b_c + g_r, NEG)
        mt = jnp.maximum(b_c + m_h, jnp.max(am, axis=-1, keepdims=True))
        d = jnp.exp(am - mt)
        qh = q_ref[:, hh * ML_DK:(hh + 1) * ML_DK]
        kf = k_ref[:, hh * ML_DK:(hh + 1) * ML_DK]
        vh = v_ref[:, hh * ML_DV:(hh + 1) * ML_DV]
        sc = _nt_dot(qh, kf.astype(BF)) * d
        inter = jnp.exp(b_c + m_h - mt)
        c_old = c_ref[hh]
        n_old = n_ref[hh]
        num = (jnp.dot(sc.astype(BF), vh, preferred_element_type=F32)
               + inter * jnp.dot(qh, c_old.astype(BF), preferred_element_type=F32))
        den = (jnp.sum(sc, axis=-1, keepdims=True)
               + inter * jnp.sum(qh.astype(F32) * n_old, axis=-1, keepdims=True))
        hout = num / jnp.maximum(jnp.abs(den), jnp.exp(-mt))
        m_new = mt[L - 1:L, :]
        b_last = b_c[L - 1:L, :]
        decay = jnp.exp(b_last + m_h - m_new)
        wk = jnp.exp(b_last + g_c - m_new) * kf
        c_ref[hh] = decay * c_old + lax.dot_general(
            wk.astype(BF), vh, (((0,), (0,)), ((), ())), preferred_element_type=F32)
        n_ref[hh] = decay * n_old + jnp.sum(wk, axis=0, keepdims=True)
        m_ref[hh] = jnp.broadcast_to(m_new, (1, 128))
        hn = hout * lax.rsqrt(jnp.mean(hout * hout, axis=-1, keepdims=True) + EPS)
        cols = slice(hh * ML_DV, (hh + 1) * ML_DV)
        y_ref[:, cols] = (hn * gn_ref[:, cols] * jax.nn.sigmoid(o_ref[:, cols])).astype(BF)

    co_ref[0] = c_ref[...]
    no_ref[0] = n_ref[...]
    mo_ref[0] = m_ref[...]


def _mlstm_core(q, k, v, o, gc, gr, gn, c0, n0, m0, *, B, T, L):
    nt = T // L
    G = 2 * ML_HEADS
    H = ML_HEADS
    vmem = (2 * L * (ML_QK * 6 + D * 8) + 6 * H * ML_DK * ML_DV * 4 + 16 * L * L * 4 + 12 * L * ML_DV * 4
            + 4 * 1024 * 1024)
    rows = lambda w: pl.BlockSpec((L, w), lambda b, t: (b * nt + t, 0))
    st_c = pl.BlockSpec((1, H, ML_DK, ML_DV), lambda b, t: (b, 0, 0, 0))
    st_n = pl.BlockSpec((1, H, 1, ML_DK), lambda b, t: (b, 0, 0, 0))
    st_m = pl.BlockSpec((1, H, 1, 128), lambda b, t: (b, 0, 0, 0))
    return pl.pallas_call(
        functools.partial(_mlstm_body, L=L),
        grid=(B, nt),
        in_specs=[
            rows(ML_QK), rows(ML_QK), rows(D), rows(D), rows(G),
            pl.BlockSpec((1, G, L), lambda b, t: (b, 0, t)),
            pl.BlockSpec((1, D), lambda b, t: (0, 0)),
            st_c, st_n, st_m,
        ],
        out_specs=[rows(D), st_c, st_n, st_m],
        out_shape=[
            jax.ShapeDtypeStruct((B * T, D), BF),
            jax.ShapeDtypeStruct((B, H, ML_DK, ML_DV), F32),
            jax.ShapeDtypeStruct((B, H, 1, ML_DK), F32),
            jax.ShapeDtypeStruct((B, H, 1, 128), F32),
        ],
        scratch_shapes=[
            pltpu.VMEM((H, ML_DK, ML_DV), F32),
            pltpu.VMEM((H, 1, ML_DK), F32),
            pltpu.VMEM((H, 1, 128), F32),
        ],
        compiler_params=_params(("arbitrary", "arbitrary"), vmem),
        name="mlstm_core",
    )(q, k, v, o, gc, gr, gn, c0, n0, m0)


def kernel(x_prompt, x_sample, state_pool, state_lru_conv, state_lru_h, cache_fox_k, cache_fox_v, cache_fox_logf, state_mlstm_c, state_mlstm_n, state_mlstm_m, ffn1_norm, ffn1_w_gate, ffn1_w_up, ffn1_w_down, mix_norm, ffn2_norm, ffn2_w_gate, ffn2_w_up, ffn2_w_down, pool_w, pool_scale, lru_w_in, lru_conv_w, lru_conv_b, lru_w_a, lru_b_a, lru_w_i, lru_b_i, lru_lambda, lru_w_out, fox_w_qkv, fox_w_f, fox_b_f, fox_w_o, mlstm_w_in, mlstm_b_i, mlstm_b_f, mlstm_norm, mlstm_w_out, final_norm):
    BP, TP, _ = x_prompt.shape
    BS, TS, _ = x_sample.shape
    assert BP == 1 and x_prompt.shape[2] == D and x_sample.shape[2] == D
    RS = BS * TS
    past = cache_fox_k.shape[1]
    row = lambda p: p.reshape(1, -1).astype(F32)

    xp = x_prompt.reshape(TP, D)
    xs = x_sample.reshape(RS, D)

    def both(fn, *args_ps, **kw):
        return fn(*[a[0] for a in args_ps], **kw), fn(*[a[1] for a in args_ps], **kw)

    def ffn(xp, xs, g, wg, wu, wd):
        g, wg, wu, wd = row(g), wg.astype(BF), wu.astype(BF), wd.astype(BF)
        return _ffn(xp, g, wg, wu, wd), _ffn(xs, g, wg, wu, wd)

    xp, xs = ffn(xp, xs, ffn1_norm[0], ffn1_w_gate[0], ffn1_w_up[0], ffn1_w_down[0])
    pw = pool_w.astype(BF)
    hist_s = jnp.pad(state_pool.astype(F32), ((0, 0), (POOL_PAD - state_pool.shape[1], 0), (0, 0)))
    xp, pool_p = _pool_mixer(xp, row(mix_norm[0]), jnp.zeros((BP, POOL_PAD, D), F32), pw, row(pool_scale),
                             B=BP, T=TP, tm=512, pos0=0)
    xs, pool_s = _pool_mixer(xs, row(mix_norm[0]), hist_s, pw, row(pool_scale), B=BS, T=TS, tm=TS, pos0=past)
    pool_p = pool_p[:, 1:]
    pool_s = pool_s[:, 1:]
    xp, xs = ffn(xp, xs, ffn2_norm[0], ffn2_w_gate[0], ffn2_w_up[0], ffn2_w_down[0])

    xp, xs = ffn(xp, xs, ffn1_norm[1], ffn1_w_gate[1], ffn1_w_up[1], ffn1_w_down[1])
    w_in = lru_w_in.astype(BF)
    lru_args = (lru_conv_w.astype(F32), row(lru_conv_b), lru_w_a.astype(BF), row(lru_b_a),
                lru_w_i.astype(BF), row(lru_b_i), row(lru_lambda))
    gx_p = _nmm(xp, row(mix_norm[1]), w_in)
    gx_s = _nmm(xs, row(mix_norm[1]), w_in)
    cst_s = jnp.pad(state_lru_conv.astype(F32), ((0, 0), (CONV_PAD - (CONV_W - 1), 0), (0, 0)))
    yp, conv_p, h_p = _lru_core(gx_p, jnp.zeros((BP, CONV_PAD, D), F32), jnp.zeros((BP, 1, D), F32), *lru_args,
                                B=BP, T=TP, tm=256)
    ys, conv_s, h_s = _lru_core(gx_s, cst_s, state_lru_h.astype(F32).reshape(BS, 1, D), *lru_args,
                                B=BS, T=TS, tm=TS)
    w_out = lru_w_out.astype(BF)
    xp = _mm_res(yp, w_out, xp)
    xs = _mm_res(ys, w_out, xs)
    lru_conv_p, lru_conv_s = conv_p[:, CONV_PAD - (CONV_W - 1):], conv_s[:, CONV_PAD - (CONV_W - 1):]
    lru_h_p, lru_h_s = h_p.reshape(BP, D), h_s.reshape(BS, D)
    xp, xs = ffn(xp, xs, ffn2_norm[1], ffn2_w_gate[1], ffn2_w_up[1], ffn2_w_down[1])

    xp, xs = ffn(xp, xs, ffn1_norm[2], ffn1_w_gate[2], ffn1_w_up[2], ffn1_w_down[2])
    w_qkv = fox_w_qkv.astype(BF)
    wf = fox_w_f.astype(BF)
    wft = wf.T
    bf_r = fox_b_f.astype(F32).reshape(1, FOX_HEADS)
    bf_c = fox_b_f.astype(F32).reshape(FOX_HEADS, 1)
    qh_p, k_p, v_p, kh_p, vh_p, lf_p, cc_p, cr_p = _fox_proj(
        xp, row(mix_norm[2]), w_qkv, wf, wft, bf_r, bf_c, tm=256, seg=256, carry=True)
    qh_s, k_s, v_s, kh_s, vh_s, lf_s, cc_s, cr_s = _fox_proj(
        xs, row(mix_norm[2]), w_qkv, wf, wft, bf_r, bf_c, tm=256, seg=TS, carry=False)
    o_p = _fox_attn(qh_p, kh_p, vh_p, cc_p, cr_p, T=TP)
    dsuf = _suffix_sum(jnp.swapaxes(cache_fox_logf.astype(F32), 1, 2))
    er_s = jnp.swapaxes(cr_s.reshape(FOX_HEADS, BS, TS), 0, 1)
    o_s = _fox_decode(qh_s, kh_s, vh_s, cache_fox_k.reshape(BS, past, D), cache_fox_v.reshape(BS, past, D),
                      dsuf, cc_s, er_s, B=BS, tq=TS)
    w_o = fox_w_o.astype(BF)
    xp = _mm_res(o_p, w_o, xp)
    xs = _mm_res(o_s, w_o, xs)
    fox_k_p = k_p.reshape(BP, TP, FOX_HEADS, FOX_HD)
    fox_v_p = v_p.reshape(BP, TP, FOX_HEADS, FOX_HD)
    fox_k_s = k_s.reshape(BS, TS, FOX_HEADS, FOX_HD)
    fox_v_s = v_s.reshape(BS, TS, FOX_HEADS, FOX_HD)
    fox_lf_p = lf_p.reshape(BP, TP, FOX_HEADS)
    fox_lf_s = lf_s.reshape(BS, TS, FOX_HEADS)
    xp, xs = ffn(xp, xs, ffn2_norm[2], ffn2_w_gate[2], ffn2_w_up[2], ffn2_w_down[2])

    xp, xs = ffn(xp, xs, ffn1_norm[3], ffn1_w_gate[3], ffn1_w_up[3], ffn1_w_down[3])
    n_main = 2 * ML_QK + 2 * D
    w_main = mlstm_w_in[:, :n_main].astype(BF)
    w_gates = mlstm_w_in[:, n_main:].astype(BF)
    b_gates = jnp.concatenate([mlstm_b_i, mlstm_b_f]).astype(F32)
    ml_w = (row(mix_norm[3]), w_main, w_gates, w_gates.T, b_gates.reshape(1, -1), b_gates.reshape(-1, 1))
    q_p, kk_p, vv_p, og_p, gc_p, gr_p = _mlstm_proj(xp, *ml_w)
    q_s, kk_s, vv_s, og_s, gc_s, gr_s = _mlstm_proj(xs, *ml_w)
    G = 2 * ML_HEADS
    gn = row(mlstm_norm)
    yp, ml_c_p, ml_n_p, ml_m_p = _mlstm_core(
        q_p, kk_p, vv_p, og_p, gc_p, gr_p.reshape(1, G, TP), gn,
        jnp.zeros((BP, ML_HEADS, ML_DK, ML_DV), F32), jnp.zeros((BP, ML_HEADS, 1, ML_DK), F32),
        jnp.zeros((BP, ML_HEADS, 1, 128), F32), B=BP, T=TP, L=256)
    m0_s = jnp.broadcast_to(state_mlstm_m.astype(F32)[:, :, None, None], (BS, ML_HEADS, 1, 128))
    ys, ml_c_s, ml_n_s, ml_m_s = _mlstm_core(
        q_s, kk_s, vv_s, og_s, gc_s, jnp.swapaxes(gr_s.reshape(G, BS, TS), 0, 1), gn,
        state_mlstm_c.astype(F32), state_mlstm_n.astype(F32).reshape(BS, ML_HEADS, 1, ML_DK), m0_s,
        B=BS, T=TS, L=TS)
    w_out = mlstm_w_out.astype(BF)
    xp = _mm_res(yp, w_out, xp)
    xs = _mm_res(ys, w_out, xs)
    ml_n_p, ml_n_s = ml_n_p.reshape(BP, ML_HEADS, ML_DK), ml_n_s.reshape(BS, ML_HEADS, ML_DK)
    ml_m_p, ml_m_s = ml_m_p[:, :, 0, 0], ml_m_s[:, :, 0, 0]
    xp, xs = ffn(xp, xs, ffn2_norm[3], ffn2_w_gate[3], ffn2_w_up[3], ffn2_w_down[3])

    y_prompt = _norm(xp, row(final_norm)).reshape(BP, TP, D)
    y_sample = _norm(xs, row(final_norm)).reshape(BS, TS, D)
    return (y_prompt, y_sample, pool_p, pool_s, lru_conv_p, lru_conv_s, lru_h_p, lru_h_s,
            fox_k_p, fox_k_s, fox_v_p, fox_v_s, fox_lf_p, fox_lf_s,
            ml_c_p, ml_c_s, ml_n_p, ml_n_s, ml_m_p, ml_m_s)
```

```python
import functools
import math

import jax
import jax.numpy as jnp
from jax import lax
from jax.experimental import pallas as pl
from jax.experimental.pallas import tpu as pltpu

F32 = jnp.float32
BF = jnp.bfloat16
HIGHEST = lax.Precision.HIGHEST

D = 2048
D_FF = 5632
EPS = 1e-6
NEG = -1e30
LOG2E = math.log2(math.e)

POOL_WINDOWS = (2, 4, 8, 16)
POOL_GW = D // len(POOL_WINDOWS)
POOL_PAD = 16
CONV_W = 4
CONV_PAD = 8
LRU_BLOCKS = 8
LRU_BW = D // LRU_BLOCKS
LRU_C = 8.0
FOX_HEADS = 16
FOX_HD = D // FOX_HEADS
ML_HEADS = 8
ML_DV = D // ML_HEADS
ML_DK = ML_DV // 2
ML_QK = ML_HEADS * ML_DK

V7X_VMEM_LIMIT = 56 * 1024 * 1024


def _params(semantics, vmem_bytes):
    return pltpu.CompilerParams(dimension_semantics=semantics,
                                vmem_limit_bytes=min(int(vmem_bytes), V7X_VMEM_LIMIT))


def _rms(xf, g):
    ms = jnp.mean(xf * xf, axis=-1, keepdims=True)
    return xf * lax.rsqrt(ms + EPS) * g


def _nt_dot(a, b):
    return lax.dot_general(a, b, (((1,), (1,)), ((), ())), preferred_element_type=F32)


def _log_sigmoid(z):
    return jnp.minimum(z, 0.0) - jnp.log1p(jnp.exp(-jnp.abs(z)))


def _tri(n, kind, seg=None):
    a = lax.broadcasted_iota(jnp.int32, (n, n), 0)
    b = lax.broadcasted_iota(jnp.int32, (n, n), 1)
    if kind == "le":
        m = b <= a
        if seg is not None:
            m = jnp.logical_and(m, b >= jnp.bitwise_and(a, -seg))
    elif kind == "ge":
        m = b >= a
        if seg is not None:
            m = jnp.logical_and(m, a >= jnp.bitwise_and(b, -seg))
    else:
        m = a > b
    return jnp.where(m, 1.0, 0.0).astype(F32)


def _ffn_body(x_ref, g_ref, wg_ref, wu_ref, wd_ref, o_ref, h_ref, *, nj):
    j = pl.program_id(1)

    @pl.when(j == 0)
    def _():
        h_ref[...] = _rms(x_ref[...], g_ref[...]).astype(BF)

    h = h_ref[...]
    gt = jnp.dot(h, wg_ref[...], preferred_element_type=F32)
    up = jnp.dot(h, wu_ref[...], preferred_element_type=F32)
    a = (gt * jax.nn.sigmoid(gt) * up).astype(BF)
    c = jnp.dot(a, wd_ref[...], preferred_element_type=F32)

    @pl.when(j == 0)
    def _():
        o_ref[...] = c

    @pl.when(j > 0)
    def _():
        o_ref[...] += c

    @pl.when(j == nj - 1)
    def _():
        o_ref[...] = x_ref[...] + 0.5 * o_ref[...]


def _ffn(x, g, wg, wu, wd, *, tm=1024, tf=512):
    R = x.shape[0]
    tm = min(tm, R)
    nj = D_FF // tf
    vmem = 3 * tm * D * 4 + tm * D * 2 + 6 * D * tf * 2 + 3 * tm * tf * 4 + tm * D * 4
    return pl.pallas_call(
        functools.partial(_ffn_body, nj=nj),
        grid=(R // tm, nj),
        in_specs=[
            pl.BlockSpec((tm, D), lambda i, j: (i, 0), pipeline_mode=pl.Buffered(1)),
            pl.BlockSpec((1, D), lambda i, j: (0, 0)),
            pl.BlockSpec((D, tf), lambda i, j: (0, j)),
            pl.BlockSpec((D, tf), lambda i, j: (0, j)),
            pl.BlockSpec((tf, D), lambda i, j: (j, 0)),
        ],
        out_specs=pl.BlockSpec((tm, D), lambda i, j: (i, 0)),
        out_shape=jax.ShapeDtypeStruct((R, D), F32),
        scratch_shapes=[pltpu.VMEM((tm, D), BF)],
        compiler_params=_params(("arbitrary", "arbitrary"), vmem),
        name="ffn",
    )(x, g, wg, wu, wd)


def _mm_res_body(a_ref, w_ref, x_ref, o_ref):
    o_ref[...] = x_ref[...] + jnp.dot(a_ref[...], w_ref[...], preferred_element_type=F32)


def _mm_res(a, w, x, *, tm=512):
    R = x.shape[0]
    vmem = 2 * tm * D * 2 + 2 * D * D * 2 + 5 * tm * D * 4
    return pl.pallas_call(
        _mm_res_body,
        grid=(R // tm,),
        in_specs=[
            pl.BlockSpec((tm, D), lambda i: (i, 0)),
            pl.BlockSpec((D, D), lambda i: (0, 0)),
            pl.BlockSpec((tm, D), lambda i: (i, 0)),
        ],
        out_specs=pl.BlockSpec((tm, D), lambda i: (i, 0)),
        out_shape=jax.ShapeDtypeStruct((R, D), F32),
        compiler_params=_params(("arbitrary",), vmem),
        name="mm_res",
    )(a, w, x)


def _nmm_body(x_ref, g_ref, w_ref, o_ref, h_ref):
    @pl.when(pl.program_id(1) == 0)
    def _():
        h_ref[...] = _rms(x_ref[...], g_ref[...]).astype(BF)

    o_ref[...] = jnp.dot(h_ref[...], w_ref[...], preferred_element_type=F32)


def _nmm(x, g, w, *, tm=512, tn=2048):
    R = x.shape[0]
    N = w.shape[1]
    vmem = 2 * tm * D * 4 + tm * D * 2 + 2 * D * tn * 2 + 3 * tm * tn * 4 + tm * D * 4
    return pl.pallas_call(
        _nmm_body,
        grid=(R // tm, N // tn),
        in_specs=[
            pl.BlockSpec((tm, D), lambda i, j: (i, 0)),
            pl.BlockSpec((1, D), lambda i, j: (0, 0)),
            pl.BlockSpec((D, tn), lambda i, j: (0, j)),
        ],
        out_specs=pl.BlockSpec((tm, tn), lambda i, j: (i, j)),
        out_shape=jax.ShapeDtypeStruct((R, N), F32),
        scratch_shapes=[pltpu.VMEM((tm, D), BF)],
        compiler_params=_params(("arbitrary", "arbitrary"), vmem),
        name="norm_matmul",
    )(x, g, w)


def _norm_body(x_ref, g_ref, o_ref):
    o_ref[...] = _rms(x_ref[...], g_ref[...])


def _norm(x, g, *, tm=512):
    R = x.shape[0]
    return pl.pallas_call(
        _norm_body,
        grid=(R // tm,),
        in_specs=[pl.BlockSpec((tm, D), lambda i: (i, 0)), pl.BlockSpec((1, D), lambda i: (0, 0))],
        out_specs=pl.BlockSpec((tm, D), lambda i: (i, 0)),
        out_shape=jax.ShapeDtypeStruct((R, D), F32),
        compiler_params=_params(("arbitrary",), 8 * tm * D * 4),
        name="final_norm",
    )(x, g)


def _pool_body(x_ref, g_ref, hist_ref, w_ref, sc_ref, o_ref, st_ref, xe_ref, *, tm, pos0):
    t = pl.program_id(1)

    @pl.when(t == 0)
    def _():
        xe_ref[0:POOL_PAD, :] = hist_ref[0]

    xf = x_ref[...]
    h = _rms(xf, g_ref[...])
    xe_ref[POOL_PAD:POOL_PAD + tm, :] = h
    row = lax.broadcasted_iota(jnp.int32, (tm, POOL_GW), 0)
    pos1 = (row + (pos0 + 1) + t * tm).astype(F32)
    for gi, w in enumerate(POOL_WINDOWS):
        c0 = gi * POOL_GW
        hg = h[:, c0:c0 + POOL_GW]
        s = hg
        for back in range(1, w):
            s = s + xe_ref[POOL_PAD - back:POOL_PAD - back + tm, c0:c0 + POOL_GW]
        cnt = jnp.minimum(pos1, float(w))
        pooled = s / cnt - hg
        y = jnp.dot(pooled.astype(BF), w_ref[gi], preferred_element_type=F32)
        o_ref[:, c0:c0 + POOL_GW] = xf[:, c0:c0 + POOL_GW] + y * sc_ref[:, c0:c0 + POOL_GW]
    tail = xe_ref[tm:tm + POOL_PAD, :]
    st_ref[0] = tail
    xe_ref[0:POOL_PAD, :] = tail


def _pool_mixer(x, g, hist, w, scale, *, B, T, tm, pos0):
    nt = T // tm
    vmem = 4 * tm * D * 4 + (tm + POOL_PAD) * D * 4 + 2 * 4 * POOL_GW * POOL_GW * 2 + 6 * tm * D * 4
    return pl.pallas_call(
        functools.partial(_pool_body, tm=tm, pos0=pos0),
        grid=(B, nt),
        in_specs=[
            pl.BlockSpec((tm, D), lambda b, t: (b * nt + t, 0)),
            pl.BlockSpec((1, D), lambda b, t: (0, 0)),
            pl.BlockSpec((1, POOL_PAD, D), lambda b, t: (b, 0, 0)),
            pl.BlockSpec((len(POOL_WINDOWS), POOL_GW, POOL_GW), lambda b, t: (0, 0, 0)),
            pl.BlockSpec((1, D), lambda b, t: (0, 0)),
        ],
        out_specs=[
            pl.BlockSpec((tm, D), lambda b, t: (b * nt + t, 0)),
            pl.BlockSpec((1, POOL_PAD, D), lambda b, t: (b, 0, 0)),
        ],
        out_shape=[jax.ShapeDtypeStruct((B * T, D), F32), jax.ShapeDtypeStruct((B, POOL_PAD, D), F32)],
        scratch_shapes=[pltpu.VMEM((tm + POOL_PAD, D), F32)],
        compiler_params=_params(("arbitrary", "arbitrary"), vmem),
        name="pool_mixer",
    )(x, g, hist, w, scale)


def _lru_body(gate_ref, xr_ref, cst_ref, h0_ref, cw_ref, cb_ref, wa_ref, ba_ref, wi_ref, bi_ref, lam_ref,
              y_ref, cso_ref, ho_ref, xe_ref, a_ref, u_ref, hc_ref, *, tm):
    t = pl.program_id(1)

    @pl.when(t == 0)
    def _():
        xe_ref[0:CONV_PAD, :] = cst_ref[0]
        hc_ref[...] = h0_ref[0]

    xe_ref[CONV_PAD:CONV_PAD + tm, :] = xr_ref[...]
    base = CONV_PAD - (CONV_W - 1)
    xc = cb_ref[...] + xe_ref[base:base + tm, :] * cw_ref[0:1, :]
    for j in range(1, CONV_W):
        xc = xc + xe_ref[base + j:base + j + tm, :] * cw_ref[j:j + 1, :]
    tail = xe_ref[tm:tm + CONV_PAD, :]
    cso_ref[0] = tail
    xe_ref[0:CONV_PAD, :] = tail

    xcb = xc.astype(BF)
    nlam = -lam_ref[...]
    sp = jnp.maximum(nlam, 0.0) + jnp.log1p(jnp.exp(-jnp.abs(nlam)))
    for n in range(LRU_BLOCKS):
        blk = slice(n * LRU_BW, (n + 1) * LRU_BW)
        ra = jnp.dot(xcb[:, blk], wa_ref[n], preferred_element_type=F32) + ba_ref[:, blk]
        ia = jnp.dot(xcb[:, blk], wi_ref[n], preferred_element_type=F32) + bi_ref[:, blk]
        r = jax.nn.sigmoid(ra)
        ig = jax.nn.sigmoid(ia)
        log_a = -LRU_C * r * sp[:, blk]
        a_ref[:, blk] = jnp.exp(log_a)
        th = jnp.tanh(log_a)
        one_minus_a2 = -2.0 * th / (1.0 - th)
        u_ref[:, blk] = jnp.sqrt(one_minus_a2) * ig * xc[:, blk]

    rowi = lax.broadcasted_iota(jnp.int32, (8, D), 0)

    def group(gi, carry):
        r0 = pl.multiple_of(gi * 8, 8)
        a8 = a_ref[pl.ds(r0, 8), :]
        u8 = u_ref[pl.ds(r0, 8), :]
        for d in (1, 2, 4):
            keep = rowi >= d
            u8 = jnp.where(keep, a8 * pltpu.roll(u8, d, axis=0) + u8, u8)
            a8 = jnp.where(keep, a8 * pltpu.roll(a8, d, axis=0), a8)
        hs8 = a8 * carry + u8
        u_ref[pl.ds(r0, 8), :] = hs8
        return hs8[7:8, :]

    carry = lax.fori_loop(0, tm // 8, group, hc_ref[...])
    hc_ref[...] = carry
    ho_ref[0] = carry
    y_ref[...] = (u_ref[...] * jax.nn.gelu(gate_ref[...])).astype(BF)


def _lru_core(gx, cst, h0, cw, cb, wa, ba, wi, bi, lam, *, B, T, tm):
    nt = T // tm
    vmem = 4 * tm * D * 4 + 2 * tm * D * 2 + (3 * tm + CONV_PAD) * D * 4 + 4 * 8 * LRU_BW * LRU_BW * 2 + 6 * tm * D * 4
    vec = pl.BlockSpec((1, D), lambda b, t: (0, 0))
    wblk = pl.BlockSpec((LRU_BLOCKS, LRU_BW, LRU_BW), lambda b, t: (0, 0, 0))
    return pl.pallas_call(
        functools.partial(_lru_body, tm=tm),
        grid=(B, nt),
        in_specs=[
            pl.BlockSpec((tm, D), lambda b, t: (b * nt + t, 0)),
            pl.BlockSpec((tm, D), lambda b, t: (b * nt + t, 1)),
            pl.BlockSpec((1, CONV_PAD, D), lambda b, t: (b, 0, 0)),
            pl.BlockSpec((1, 1, D), lambda b, t: (b, 0, 0)),
            pl.BlockSpec((CONV_W, D), lambda b, t: (0, 0)),
            vec, wblk, vec, wblk, vec, vec,
        ],
        out_specs=[
            pl.BlockSpec((tm, D), lambda b, t: (b * nt + t, 0)),
            pl.BlockSpec((1, CONV_PAD, D), lambda b, t: (b, 0, 0)),
            pl.BlockSpec((1, 1, D), lambda b, t: (b, 0, 0)),
        ],
        out_shape=[
            jax.ShapeDtypeStruct((B * T, D), BF),
            jax.ShapeDtypeStruct((B, CONV_PAD, D), F32),
            jax.ShapeDtypeStruct((B, 1, D), F32),
        ],
        scratch_shapes=[
            pltpu.VMEM((tm + CONV_PAD, D), F32),
            pltpu.VMEM((tm, D), F32),
            pltpu.VMEM((tm, D), F32),
            pltpu.VMEM((1, D), F32),
        ],
        compiler_params=_params(("arbitrary", "arbitrary"), vmem),
        name="rglru_core",
    )(gx, gx, cst, h0, cw, cb, wa, ba, wi, bi, lam)


def _split3_bf16(c):
    hi = c.astype(BF)
    r1 = c - hi.astype(F32)
    mid = r1.astype(BF)
    lo = (r1 - mid.astype(F32)).astype(BF)
    return hi, mid, lo


def _fox_proj_body(x_ref, g_ref, w_ref, wf_ref, wft_ref, bf_ref, bft_ref,
                   q_ref, k_ref, v_ref, ka_ref, vt_ref, lf_ref, cc_ref, cr_ref,
                   h_ref, carc_ref, carr_ref, ccs_ref, *, tm, seg, carry, qscale):
    i = pl.program_id(0)
    j = pl.program_id(1)

    @pl.when(j == 0)
    def _():
        h = _rms(x_ref[...], g_ref[...]).astype(BF)
        h_ref[...] = h
        lf = _log_sigmoid(jnp.dot(h, wf_ref[...], preferred_element_type=F32) + bf_ref[...])
        lft = _log_sigmoid(_nt_dot(wft_ref[...], h) + bft_ref[...])
        lf_ref[...] = lf
        cc = jnp.dot(_tri(tm, "le", seg), lf, precision=HIGHEST, preferred_element_type=F32)
        cr = jnp.dot(lft, _tri(tm, "ge", seg), precision=HIGHEST, preferred_element_type=F32)
        if carry:
            @pl.when(i == 0)
            def _():
                carc_ref[...] = jnp.zeros_like(carc_ref)
                carr_ref[...] = jnp.zeros_like(carr_ref)

            cc = cc + carc_ref[...]
            cr = cr + carr_ref[...]
            carc_ref[...] = cc[tm - 1:tm, :]
            carr_ref[...] = cr[:, tm - 1:tm]
        cc_ref[...] = cc
        cr_ref[...] = cr
        ccs_ref[...] = cc

    y = jnp.dot(h_ref[...], w_ref[...], preferred_element_type=F32)

    @pl.when(j == 0)
    def _():
        yq = (y * qscale).astype(BF)
        for hh in range(FOX_HEADS):
            q_ref[hh] = yq[:, hh * FOX_HD:(hh + 1) * FOX_HD]

    @pl.when(j == 1)
    def _():
        k_ref[...] = y
        yb = y.astype(BF)
        lane = lax.broadcasted_iota(jnp.int32, (tm, FOX_HD), 1)
        nck = ccs_ref[...] * (-LOG2E)
        for hh in range(FOX_HEADS):
            hi, mid, lo = _split3_bf16(nck[:, hh:hh + 1])
            aug = jnp.where(lane == 0, hi.astype(F32),
                            jnp.where(lane == 1, mid.astype(F32), jnp.where(lane == 2, lo.astype(F32), 0.0)))
            ka_ref[hh, :, 0:FOX_HD] = yb[:, hh * FOX_HD:(hh + 1) * FOX_HD]
            ka_ref[hh, :, FOX_HD:2 * FOX_HD] = aug.astype(BF)

    @pl.when(j == 2)
    def _():
        v_ref[...] = y
        for hh in range(FOX_HEADS):
            vt_ref[hh] = y[:, hh * FOX_HD:(hh + 1) * FOX_HD].T.astype(BF)


def _fox_proj(x, g, w, wf, wft, bf, bft, *, tm, seg, carry):
    R = x.shape[0]
    H = FOX_HEADS
    qscale = (FOX_HD ** -0.5) * LOG2E
    vmem = (2 * tm * D * 4 + tm * D * 2 + 2 * D * D * 2 + 4 * tm * D * 4 + 6 * tm * D * 2
            + 3 * tm * tm * 4 + 3 * tm * D * 4)
    row = pl.BlockSpec((tm, D), lambda i, j: (i, 0))
    heads = pl.BlockSpec((H, tm, FOX_HD), lambda i, j: (0, i, 0))
    return pl.pallas_call(
        functools.partial(_fox_proj_body, tm=tm, seg=seg, carry=carry, qscale=qscale),
        grid=(R // tm, 3),
        in_specs=[
            row,
            pl.BlockSpec((1, D), lambda i, j: (0, 0)),
            pl.BlockSpec((D, D), lambda i, j: (0, j)),
            pl.BlockSpec((D, H), lambda i, j: (0, 0)),
            pl.BlockSpec((H, D), lambda i, j: (0, 0)),
            pl.BlockSpec((1, H), lambda i, j: (0, 0)),
            pl.BlockSpec((H, 1), lambda i, j: (0, 0)),
        ],
        out_specs=[
            heads, row, row,
            pl.BlockSpec((H, tm, 2 * FOX_HD), lambda i, j: (0, i, 0)),
            pl.BlockSpec((H, FOX_HD, tm), lambda i, j: (0, 0, i)),
            pl.BlockSpec((tm, H), lambda i, j: (i, 0)),
            pl.BlockSpec((tm, H), lambda i, j: (i, 0)),
            pl.BlockSpec((H, tm), lambda i, j: (0, i)),
        ],
        out_shape=[
            jax.ShapeDtypeStruct((H, R, FOX_HD), BF),
            jax.ShapeDtypeStruct((R, D), F32),
            jax.ShapeDtypeStruct((R, D), F32),
            jax.ShapeDtypeStruct((H, R, 2 * FOX_HD), BF),
            jax.ShapeDtypeStruct((H, FOX_HD, R), BF),
            jax.ShapeDtypeStruct((R, H), F32),
            jax.ShapeDtypeStruct((R, H), F32),
            jax.ShapeDtypeStruct((H, R), F32),
        ],
        scratch_shapes=[pltpu.VMEM((tm, D), BF), pltpu.VMEM((1, H), F32), pltpu.VMEM((H, 1), F32),
                        pltpu.VMEM((tm, H), F32)],
        compiler_params=_params(("arbitrary", "arbitrary"), vmem),
        name="fox_proj",
    )(x, g, w, wf, wft, bf, bft)


def _softmax_step(s_biased, cq, pv, m_ref, l_ref, acc_ref):
    m_old = m_ref[...]
    m_new = jnp.maximum(m_old, jnp.max(s_biased, axis=-1, keepdims=True) + cq)
    p = jnp.exp2(s_biased - (m_new - cq))
    alpha = jnp.exp2(m_old - m_new)
    l_ref[...] = alpha * l_ref[...] + jnp.sum(p, axis=-1, keepdims=True)
    acc_ref[...] = alpha * acc_ref[...] + pv(p.astype(BF))
    m_ref[...] = m_new


def _fox_attn_body(q_ref, ka_ref, vt_ref, cr_ref, o_ref, qt_ref, m_ref, l_ref, acc_ref, sa_ref, sb_ref, *, tq):
    h = pl.program_id(0)
    qi = pl.program_id(1)
    q0 = pl.multiple_of(qi * tq, tq)
    qt_ref[0:FOX_HD, :] = q_ref[0].astype(F32).T.astype(BF)
    sub = lax.broadcasted_iota(jnp.int32, (FOX_HD, tq), 0)
    qt_ref[FOX_HD:2 * FOX_HD, :] = jnp.where(sub < 3, 1.0, 0.0).astype(BF)
    cq = cr_ref[pl.ds(h, 1), pl.ds(q0, tq)] * LOG2E
    m_ref[...] = jnp.full_like(m_ref, NEG)
    l_ref[...] = jnp.zeros_like(l_ref)
    acc_ref[...] = jnp.zeros_like(acc_ref)

    def scores(kk, st_ref):
        k0 = pl.multiple_of(kk * tq, tq)
        st_ref[...] = jnp.dot(ka_ref[0, pl.ds(k0, tq), :], qt_ref[...], preferred_element_type=F32)

    def fold(kk, st_ref, masked):
        k0 = pl.multiple_of(kk * tq, tq)
        st = st_ref[...]
        if masked:
            kj = lax.broadcasted_iota(jnp.int32, (tq, tq), 0)
            qcol = lax.broadcasted_iota(jnp.int32, (tq, tq), 1)
            st = jnp.where(kj <= qcol, st, NEG)
        m_old = m_ref[...]
        m_new = jnp.maximum(m_old, jnp.max(st, axis=0, keepdims=True) + cq)
        p = jnp.exp2(st - (m_new - cq))
        alpha = jnp.exp2(m_old - m_new)
        l_ref[...] = alpha * l_ref[...] + jnp.sum(p, axis=0, keepdims=True)
        acc_ref[...] = alpha * acc_ref[...] + jnp.dot(
            vt_ref[0, :, pl.ds(k0, tq)], p.astype(BF), preferred_element_type=F32)
        m_ref[...] = m_new

    scores(0, sa_ref)

    def pair(pp, _):
        scores(2 * pp + 1, sb_ref)
        fold(2 * pp, sa_ref, False)
        scores(2 * pp + 2, sa_ref)
        fold(2 * pp + 1, sb_ref, False)
        return 0

    lax.fori_loop(0, qi // 2, pair, 0)

    @pl.when(qi % 2 == 1)
    def _():
        scores(qi, sb_ref)
        fold(qi - 1, sa_ref, False)
        fold(qi, sb_ref, True)

    @pl.when(qi % 2 == 0)
    def _():
        fold(qi, sa_ref, True)

    o_ref[...] = (acc_ref[...] / l_ref[...]).T.astype(BF)


def _fox_attn(qh, ka, vt, cr, *, T, tq=512):
    H = FOX_HEADS
    vmem = 2 * T * 2 * FOX_HD * 2 + 2 * T * FOX_HD * 2 + 2 * H * T * 4 + 8 * tq * tq * 4 + 16 * tq * FOX_HD * 4
    return pl.pallas_call(
        functools.partial(_fox_attn_body, tq=tq),
        grid=(H, T // tq),
        in_specs=[
            pl.BlockSpec((1, tq, FOX_HD), lambda h, i: (h, i, 0)),
            pl.BlockSpec((1, T, 2 * FOX_HD), lambda h, i: (h, 0, 0)),
            pl.BlockSpec((1, FOX_HD, T), lambda h, i: (h, 0, 0)),
            pl.BlockSpec((H, T), lambda h, i: (0, 0)),
        ],
        out_specs=pl.BlockSpec((tq, FOX_HD), lambda h, i: (i, h)),
        out_shape=jax.ShapeDtypeStruct((T, D), BF),
        scratch_shapes=[pltpu.VMEM((2 * FOX_HD, tq), BF), pltpu.VMEM((1, tq), F32), pltpu.VMEM((1, tq), F32),
                        pltpu.VMEM((FOX_HD, tq), F32), pltpu.VMEM((tq, tq), F32), pltpu.VMEM((tq, tq), F32)],
        compiler_params=_params(("arbitrary", "arbitrary"), vmem),
        name="fox_attention",
    )(qh, ka, vt, cr)


def _suffix_body(x_ref, o_ref, car_ref, *, tc):
    @pl.when(pl.program_id(1) == 0)
    def _():
        car_ref[...] = jnp.zeros_like(car_ref)

    x = x_ref[0]
    o_ref[0] = jnp.dot(x, _tri(tc, "gt"), precision=HIGHEST, preferred_element_type=F32) + car_ref[...]
    car_ref[...] = car_ref[...] + jnp.sum(x, axis=1, keepdims=True)


def _suffix_sum(x, *, tc=512):
    B, H, S = x.shape
    nc = S // tc
    return pl.pallas_call(
        functools.partial(_suffix_body, tc=tc),
        grid=(B, nc),
        in_specs=[pl.BlockSpec((1, H, tc), lambda b, t: (b, 0, nc - 1 - t))],
        out_specs=pl.BlockSpec((1, H, tc), lambda b, t: (b, 0, nc - 1 - t)),
        out_shape=jax.ShapeDtypeStruct((B, H, S), F32),
        scratch_shapes=[pltpu.VMEM((H, 1), F32)],
        compiler_params=_params(("arbitrary", "arbitrary"), 8 * tc * tc * 4),
        name="suffix_sum",
    )(x)


def _fox_decode_body(q_ref, ka_ref, vt_ref, kc_ref, vc_ref, dsuf_ref, ec_ref, er_ref, o_ref,
                     m_ref, l_ref, acc_ref, *, nt, tq, ts):
    b = pl.program_id(0)
    t = pl.program_id(1)
    H = FOX_HEADS

    @pl.when(t == 0)
    def _():
        m_ref[...] = jnp.full_like(m_ref, NEG)
        l_ref[...] = jnp.zeros_like(l_ref)
        acc_ref[...] = jnp.zeros_like(acc_ref)

    ec = ec_ref[...] * LOG2E
    dsuf = dsuf_ref[0] * LOG2E
    for hh in range(H):
        kt = kc_ref[0, pl.ds(hh, ts, stride=H), :].astype(BF)
        vt = vc_ref[0, pl.ds(hh, ts, stride=H), :].astype(BF)
        s = _nt_dot(q_ref[hh], kt) + dsuf[hh:hh + 1, :]
        _softmax_step(s, ec[:, hh:hh + 1], lambda p, vt=vt: jnp.dot(p, vt, preferred_element_type=F32),
                      m_ref.at[hh], l_ref.at[hh], acc_ref.at[hh])

    @pl.when(t == nt - 1)
    def _():
        n_new = er_ref.shape[1]
        er = er_ref[...] * LOG2E
        r = lax.broadcasted_iota(jnp.int32, (tq, n_new), 0)
        c = lax.broadcasted_iota(jnp.int32, (tq, n_new), 1) - b * tq
        valid = jnp.logical_and(c >= 0, c <= r)
        for hh in range(H):
            s = _nt_dot(q_ref[hh], ka_ref[hh][:, 0:FOX_HD]) - er[hh:hh + 1, :]
            s = jnp.where(valid, s, NEG)
            _softmax_step(s, ec[:, hh:hh + 1], lambda p, hh=hh: _nt_dot(p, vt_ref[hh]),
                          m_ref.at[hh], l_ref.at[hh], acc_ref.at[hh])
            o_ref[:, hh * FOX_HD:(hh + 1) * FOX_HD] = (acc_ref[hh] / l_ref[hh]).astype(BF)


def _fox_decode(qh, ka, vt, kc, vc, dsuf, ec, er, *, B, tq, ts=512):
    H = FOX_HEADS
    S = kc.shape[1] // H
    R = B * tq
    nt = S // ts
    vmem = (4 * ts * D * 4 + 2 * H * R * 3 * FOX_HD * 2 + 2 * H * tq * FOX_HD * 2 + 3 * H * tq * 128 * 4
            + 4 * ts * FOX_HD * 2 + 8 * tq * max(ts, R) * 4)
    return pl.pallas_call(
        functools.partial(_fox_decode_body, nt=nt, tq=tq, ts=ts),
        grid=(B, nt),
        in_specs=[
            pl.BlockSpec((H, tq, FOX_HD), lambda b, t: (0, b, 0)),
            pl.BlockSpec((H, R, 2 * FOX_HD), lambda b, t: (0, 0, 0)),
            pl.BlockSpec((H, FOX_HD, R), lambda b, t: (0, 0, 0)),
            pl.BlockSpec((1, ts * H, FOX_HD), lambda b, t: (b, t, 0)),
            pl.BlockSpec((1, ts * H, FOX_HD), lambda b, t: (b, t, 0)),
            pl.BlockSpec((1, H, ts), lambda b, t: (b, 0, t)),
            pl.BlockSpec((tq, H), lambda b, t: (b, 0)),
            pl.BlockSpec((H, R), lambda b, t: (0, 0)),
        ],
        out_specs=pl.BlockSpec((tq, D), lambda b, t: (b, 0)),
        out_shape=jax.ShapeDtypeStruct((R, D), BF),
        scratch_shapes=[pltpu.VMEM((H, tq, 1), F32), pltpu.VMEM((H, tq, 1), F32), pltpu.VMEM((H, tq, FOX_HD), F32)],
        compiler_params=_params(("arbitrary", "arbitrary"), vmem),
        name="fox_decode_attention",
    )(qh, ka, vt, kc, vc, dsuf, ec, er)


def _mlstm_proj_body(x_ref, g_ref, w_ref, wg_ref, wgt_ref, bg_ref, bgt_ref,
                     q_ref, k_ref, v_ref, o_ref, gc_ref, gr_ref, h_ref, *, qscale):
    j = pl.program_id(1)

    @pl.when(j == 0)
    def _():
        h = _rms(x_ref[...], g_ref[...]).astype(BF)
        h_ref[...] = h
        zc = jnp.dot(h, wg_ref[...], preferred_element_type=F32) + bg_ref[...]
        lane = lax.broadcasted_iota(jnp.int32, zc.shape, 1)
        gc_ref[...] = jnp.where(lane < ML_HEADS, zc, _log_sigmoid(zc))
        zr = _nt_dot(wgt_ref[...], h) + bgt_ref[...]
        sub = lax.broadcasted_iota(jnp.int32, zr.shape, 0)
        gr_ref[...] = jnp.where(sub < ML_HEADS, zr, _log_sigmoid(zr))

    y = jnp.dot(h_ref[...], w_ref[...], preferred_element_type=F32)

    @pl.when(j == 0)
    def _():
        q_ref[...] = (y[:, :ML_QK] * qscale).astype(BF)
        k_ref[...] = y[:, ML_QK:]

    @pl.when(j == 1)
    def _():
        v_ref[...] = y.astype(BF)

    @pl.when(j == 2)
    def _():
        o_ref[...] = y


def _mlstm_proj(x, g, w, wg, wgt, bg, bgt, *, tm=512):
    R = x.shape[0]
    G = 2 * ML_HEADS
    vmem = 2 * tm * D * 4 + tm * D * 2 + 2 * D * D * 2 + 2 * tm * D * 4 + 8 * tm * D * 2 + 4 * tm * D * 4
    return pl.pallas_call(
        functools.partial(_mlstm_proj_body, qscale=ML_DK ** -0.5),
        grid=(R // tm, 3),
        in_specs=[
            pl.BlockSpec((tm, D), lambda i, j: (i, 0)),
            pl.BlockSpec((1, D), lambda i, j: (0, 0)),
            pl.BlockSpec((D, D), lambda i, j: (0, j)),
            pl.BlockSpec((D, G), lambda i, j: (0, 0)),
            pl.BlockSpec((G, D), lambda i, j: (0, 0)),
            pl.BlockSpec((1, G), lambda i, j: (0, 0)),
            pl.BlockSpec((G, 1), lambda i, j: (0, 0)),
        ],
        out_specs=[
            pl.BlockSpec((tm, ML_QK), lambda i, j: (i, 0)),
            pl.BlockSpec((tm, ML_QK), lambda i, j: (i, 0)),
            pl.BlockSpec((tm, D), lambda i, j: (i, 0)),
            pl.BlockSpec((tm, D), lambda i, j: (i, 0)),
            pl.BlockSpec((tm, G), lambda i, j: (i, 0)),
            pl.BlockSpec((G, tm), lambda i, j: (0, i)),
        ],
        out_shape=[
            jax.ShapeDtypeStruct((R, ML_QK), BF),
            jax.ShapeDtypeStruct((R, ML_QK), F32),
            jax.ShapeDtypeStruct((R, D), BF),
            jax.ShapeDtypeStruct((R, D), F32),
            jax.ShapeDtypeStruct((R, G), F32),
            jax.ShapeDtypeStruct((G, R), F32),
        ],
        scratch_shapes=[pltpu.VMEM((tm, D), BF)],
        compiler_params=_params(("arbitrary", "arbitrary"), vmem),
        name="mlstm_proj",
    )(x, g, w, wg, wgt, bg, bgt)


def _mlstm_body(q_ref, k_ref, v_ref, o_ref, gc_ref, gr_ref, gn_ref, c0_ref, n0_ref, m0_ref,
                y_ref, co_ref, no_ref, mo_ref, c_ref, n_ref, m_ref, *, L):
    t = pl.program_id(1)

    @pl.when(t == 0)
    def _():
        c_ref[...] = c0_ref[0]
        n_ref[...] = n0_ref[0]
        m_ref[...] = m0_ref[0]

    gc = gc_ref[...]
    gr = gr_ref[0]
    ra = lax.broadcasted_iota(jnp.int32, (L, L), 0)
    cb = lax.broadcasted_iota(jnp.int32, (L, L), 1)
    causal = cb <= ra
    bcs = jnp.dot(_tri(L, "le"), gc, precision=HIGHEST, preferred_element_type=F32)
    brs = jnp.dot(gr, _tri(L, "ge"), precision=HIGHEST, preferred_element_type=F32)
    for hh in range(ML_HEADS):
        b_c = bcs[:, ML_HEADS + hh:ML_HEADS + hh + 1]
        g_c = gc[:, hh:hh + 1] - b_c
        g_r = gr[hh:hh + 1, :] - brs[ML_HEADS + hh:ML_HEADS + hh + 1, :]
        m_h = m_ref[hh][:, 0:1]
        am = jnp.where(causal, b_c + g_r, NEG)
        mt = jnp.maximum(b_c + m_h, jnp.max(am, axis=-1, keepdims=True))
        d = jnp.exp(am - mt)
        qh = q_ref[:, hh * ML_DK:(hh + 1) * ML_DK]
        kf = k_ref[:, hh * ML_DK:(hh + 1) * ML_DK]
        vh = v_ref[:, hh * ML_DV:(hh + 1) * ML_DV]
        sc = _nt_dot(qh, kf.astype(BF)) * d
        inter = jnp.exp(b_c + m_h - mt)
        c_old = c_ref[hh]
        n_old = n_ref[hh]
        num = (jnp.dot(sc.astype(BF), vh, preferred_element_type=F32)
               + inter * jnp.dot(qh, c_old.astype(BF), preferred_element_type=F32))
        den = (jnp.sum(sc, axis=-1, keepdims=True)
               + inter * jnp.sum(qh.astype(F32) * n_old, axis=-1, keepdims=True))
        hout = num / jnp.maximum(jnp.abs(den), jnp.exp(-mt))
        m_new = mt[L - 1:L, :]
        b_last = b_c[L - 1:L, :]
        decay = jnp.exp(b_last + m_h - m_new)
        wk = jnp.exp(b_last + g_c - m_new) * kf
        c_ref[hh] = decay * c_old + lax.dot_general(
            wk.astype(BF), vh, (((0,), (0,)), ((), ())), preferred_element_type=F32)
        n_ref[hh] = decay * n_old + jnp.sum(wk, axis=0, keepdims=True)
        m_ref[hh] = jnp.broadcast_to(m_new, (1, 128))
        hn = hout * lax.rsqrt(jnp.mean(hout * hout, axis=-1, keepdims=True) + EPS)
        cols = slice(hh * ML_DV, (hh + 1) * ML_DV)
        y_ref[:, cols] = (hn * gn_ref[:, cols] * jax.nn.sigmoid(o_ref[:, cols])).astype(BF)

    co_ref[0] = c_ref[...]
    no_ref[0] = n_ref[...]
    mo_ref[0] = m_ref[...]


def _mlstm_core(q, k, v, o, gc, gr, gn, c0, n0, m0, *, B, T, L):
    nt = T // L
    G = 2 * ML_HEADS
    H = ML_HEADS
    vmem = (2 * L * (ML_QK * 6 + D * 8) + 6 * H * ML_DK * ML_DV * 4 + 16 * L * L * 4 + 12 * L * ML_DV * 4
            + 4 * 1024 * 1024)
    rows = lambda w: pl.BlockSpec((L, w), lambda b, t: (b * nt + t, 0))
    st_c = pl.BlockSpec((1, H, ML_DK, ML_DV), lambda b, t: (b, 0, 0, 0))
    st_n = pl.BlockSpec((1, H, 1, ML_DK), lambda b, t: (b, 0, 0, 0))
    st_m = pl.BlockSpec((1, H, 1, 128), lambda b, t: (b, 0, 0, 0))
    return pl.pallas_call(
        functools.partial(_mlstm_body, L=L),
        grid=(B, nt),
        in_specs=[
            rows(ML_QK), rows(ML_QK), rows(D), rows(D), rows(G),
            pl.BlockSpec((1, G, L), lambda b, t: (b, 0, t)),
            pl.BlockSpec((1, D), lambda b, t: (0, 0)),
            st_c, st_n, st_m,
        ],
        out_specs=[rows(D), st_c, st_n, st_m],
        out_shape=[
            jax.ShapeDtypeStruct((B * T, D), BF),
            jax.ShapeDtypeStruct((B, H, ML_DK, ML_DV), F32),
            jax.ShapeDtypeStruct((B, H, 1, ML_DK), F32),
            jax.ShapeDtypeStruct((B, H, 1, 128), F32),
        ],
        scratch_shapes=[
            pltpu.VMEM((H, ML_DK, ML_DV), F32),
            pltpu.VMEM((H, 1, ML_DK), F32),
            pltpu.VMEM((H, 1, 128), F32),
        ],
        compiler_params=_params(("arbitrary", "arbitrary"), vmem),
        name="mlstm_core",
    )(q, k, v, o, gc, gr, gn, c0, n0, m0)


def kernel(x_prompt, x_sample, state_pool, state_lru_conv, state_lru_h, cache_fox_k, cache_fox_v, cache_fox_logf, state_mlstm_c, state_mlstm_n, state_mlstm_m, ffn1_norm, ffn1_w_gate, ffn1_w_up, ffn1_w_down, mix_norm, ffn2_norm, ffn2_w_gate, ffn2_w_up, ffn2_w_down, pool_w, pool_scale, lru_w_in, lru_conv_w, lru_conv_b, lru_w_a, lru_b_a, lru_w_i, lru_b_i, lru_lambda, lru_w_out, fox_w_qkv, fox_w_f, fox_b_f, fox_w_o, mlstm_w_in, mlstm_b_i, mlstm_b_f, mlstm_norm, mlstm_w_out, final_norm):
    BP, TP, _ = x_prompt.shape
    BS, TS, _ = x_sample.shape
    assert BP == 1 and x_prompt.shape[2] == D and x_sample.shape[2] == D
    RS = BS * TS
    past = cache_fox_k.shape[1]
    row = lambda p: p.reshape(1, -1).astype(F32)

    xp = x_prompt.reshape(TP, D)
    xs = x_sample.reshape(RS, D)

    def both(fn, *args_ps, **kw):
        return fn(*[a[0] for a in args_ps], **kw), fn(*[a[1] for a in args_ps], **kw)

    def ffn(xp, xs, g, wg, wu, wd):
        g, wg, wu, wd = row(g), wg.astype(BF), wu.astype(BF), wd.astype(BF)
        return _ffn(xp, g, wg, wu, wd), _ffn(xs, g, wg, wu, wd)

    xp, xs = ffn(xp, xs, ffn1_norm[0], ffn1_w_gate[0], ffn1_w_up[0], ffn1_w_down[0])
    pw = pool_w.astype(BF)
    hist_s = jnp.pad(state_pool.astype(F32), ((0, 0), (POOL_PAD - state_pool.shape[1], 0), (0, 0)))
    xp, pool_p = _pool_mixer(xp, row(mix_norm[0]), jnp.zeros((BP, POOL_PAD, D), F32), pw, row(pool_scale),
                             B=BP, T=TP, tm=512, pos0=0)
    xs, pool_s = _pool_mixer(xs, row(mix_norm[0]), hist_s, pw, row(pool_scale), B=BS, T=TS, tm=TS, pos0=past)
    pool_p = pool_p[:, 1:]
    pool_s = pool_s[:, 1:]
    xp, xs = ffn(xp, xs, ffn2_norm[0], ffn2_w_gate[0], ffn2_w_up[0], ffn2_w_down[0])

    xp, xs = ffn(xp, xs, ffn1_norm[1], ffn1_w_gate[1], ffn1_w_up[1], ffn1_w_down[1])
    w_in = lru_w_in.astype(BF)
    lru_args = (lru_conv_w.astype(F32), row(lru_conv_b), lru_w_a.astype(BF), row(lru_b_a),
                lru_w_i.astype(BF), row(lru_b_i), row(lru_lambda))
    gx_p = _nmm(xp, row(mix_norm[1]), w_in)
    gx_s = _nmm(xs, row(mix_norm[1]), w_in)
    cst_s = jnp.pad(state_lru_conv.astype(F32), ((0, 0), (CONV_PAD - (CONV_W - 1), 0), (0, 0)))
    yp, conv_p, h_p = _lru_core(gx_p, jnp.zeros((BP, CONV_PAD, D), F32), jnp.zeros((BP, 1, D), F32), *lru_args,
                                B=BP, T=TP, tm=256)
    ys, conv_s, h_s = _lru_core(gx_s, cst_s, state_lru_h.astype(F32).reshape(BS, 1, D), *lru_args,
                                B=BS, T=TS, tm=TS)
    w_out = lru_w_out.astype(BF)
    xp = _mm_res(yp, w_out, xp)
    xs = _mm_res(ys, w_out, xs)
    lru_conv_p, lru_conv_s = conv_p[:, CONV_PAD - (CONV_W - 1):], conv_s[:, CONV_PAD - (CONV_W - 1):]
    lru_h_p, lru_h_s = h_p.reshape(BP, D), h_s.reshape(BS, D)
    xp, xs = ffn(xp, xs, ffn2_norm[1], ffn2_w_gate[1], ffn2_w_up[1], ffn2_w_down[1])

    xp, xs = ffn(xp, xs, ffn1_norm[2], ffn1_w_gate[2], ffn1_w_up[2], ffn1_w_down[2])
    w_qkv = fox_w_qkv.astype(BF)
    wf = fox_w_f.astype(BF)
    wft = wf.T
    bf_r = fox_b_f.astype(F32).reshape(1, FOX_HEADS)
    bf_c = fox_b_f.astype(F32).reshape(FOX_HEADS, 1)
    qh_p, k_p, v_p, ka_p, vt_p, lf_p, cc_p, cr_p = _fox_proj(
        xp, row(mix_norm[2]), w_qkv, wf, wft, bf_r, bf_c, tm=256, seg=256, carry=True)
    qh_s, k_s, v_s, ka_s, vt_s, lf_s, cc_s, cr_s = _fox_proj(
        xs, row(mix_norm[2]), w_qkv, wf, wft, bf_r, bf_c, tm=256, seg=TS, carry=False)
    o_p = _fox_attn(qh_p, ka_p, vt_p, cr_p, T=TP)
    dsuf = _suffix_sum(jnp.swapaxes(cache_fox_logf.astype(F32), 1, 2))
    o_s = _fox_decode(qh_s, ka_s, vt_s,
                      cache_fox_k.reshape(BS, past * FOX_HEADS, FOX_HD),
                      cache_fox_v.reshape(BS, past * FOX_HEADS, FOX_HD),
                      dsuf, cc_s, cr_s, B=BS, tq=TS)
    w_o = fox_w_o.astype(BF)
    xp = _mm_res(o_p, w_o, xp)
    xs = _mm_res(o_s, w_o, xs)
    fox_k_p = k_p.reshape(BP, TP, FOX_HEADS, FOX_HD)
    fox_v_p = v_p.reshape(BP, TP, FOX_HEADS, FOX_HD)
    fox_k_s = k_s.reshape(BS, TS, FOX_HEADS, FOX_HD)
    fox_v_s = v_s.reshape(BS, TS, FOX_HEADS, FOX_HD)
    fox_lf_p = lf_p.reshape(BP, TP, FOX_HEADS)
    fox_lf_s = lf_s.reshape(BS, TS, FOX_HEADS)
    xp, xs = ffn(xp, xs, ffn2_norm[2], ffn2_w_gate[2], ffn2_w_up[2], ffn2_w_down[2])

    xp, xs = ffn(xp, xs, ffn1_norm[3], ffn1_w_gate[3], ffn1_w_up[3], ffn1_w_down[3])
    n_main = 2 * ML_QK + 2 * D
    w_main = mlstm_w_in[:, :n_main].astype(BF)
    w_gates = mlstm_w_in[:, n_main:].astype(BF)
    b_gates = jnp.concatenate([mlstm_b_i, mlstm_b_f]).astype(F32)
    ml_w = (row(mix_norm[3]), w_main, w_gates, w_gates.T, b_gates.reshape(1, -1), b_gates.reshape(-1, 1))
    q_p, kk_p, vv_p, og_p, gc_p, gr_p = _mlstm_proj(xp, *ml_w)
    q_s, kk_s, vv_s, og_s, gc_s, gr_s = _mlstm_proj(xs, *ml_w)
    G = 2 * ML_HEADS
    gn = row(mlstm_norm)
    yp, ml_c_p, ml_n_p, ml_m_p = _mlstm_core(
        q_p, kk_p, vv_p, og_p, gc_p, gr_p.reshape(1, G, TP), gn,
        jnp.zeros((BP, ML_HEADS, ML_DK, ML_DV), F32), jnp.zeros((BP, ML_HEADS, 1, ML_DK), F32),
        jnp.zeros((BP, ML_HEADS, 1, 128), F32), B=BP, T=TP, L=256)
    m0_s = jnp.broadcast_to(state_mlstm_m.astype(F32)[:, :, None, None], (BS, ML_HEADS, 1, 128))
    ys, ml_c_s, ml_n_s, ml_m_s = _mlstm_core(
        q_s, kk_s, vv_s, og_s, gc_s, jnp.swapaxes(gr_s.reshape(G, BS, TS), 0, 1), gn,
        state_mlstm_c.astype(F32), state_mlstm_n.astype(F32).reshape(BS, ML_HEADS, 1, ML_DK), m0_s,
        B=BS, T=TS, L=TS)
    w_out = mlstm_w_out.astype(BF)
    xp = _mm_res(yp, w_out, xp)
    xs = _mm_res(ys, w_out, xs)
    ml_n_p, ml_n_s = ml_n_p.reshape(BP, ML_HEADS, ML_DK), ml_n_s.reshape(BS, ML_HEADS, ML_DK)
    ml_m_p, ml_m_s = ml_m_p[:, :, 0, 0], ml_m_s[:, :, 0, 0]
    xp, xs = ffn(xp, xs, ffn2_norm[3], ffn2_w_gate[3], ffn2_w_up[3], ffn2_w_down[3])

    y_prompt = _norm(xp, row(final_norm)).reshape(BP, TP, D)
    y_sample = _norm(xs, row(final_norm)).reshape(BS, TS, D)
    return (y_prompt, y_sample, pool_p, pool_s, lru_conv_p, lru_conv_s, lru_h_p, lru_h_s,
            fox_k_p, fox_k_s, fox_v_p, fox_v_s, fox_lf_p, fox_lf_s,
            ml_c_p, ml_c_s, ml_n_p, ml_n_s, ml_m_p, ml_m_s)
```

```python
import functools
import math

import jax
import jax.numpy as jnp
from jax import lax
from jax.experimental import pallas as pl
from jax.experimental.pallas import tpu as pltpu

F32 = jnp.float32
BF = jnp.bfloat16
HIGHEST = lax.Precision.HIGHEST

D = 2048
D_FF = 5632
EPS = 1e-6
NEG = -1e30
LOG2E = math.log2(math.e)

POOL_WINDOWS = (2, 4, 8, 16)
POOL_GW = D // len(POOL_WINDOWS)
POOL_PAD = 16
CONV_W = 4
CONV_PAD = 8
LRU_BLOCKS = 8
LRU_BW = D // LRU_BLOCKS
LRU_C = 8.0
FOX_HEADS = 16
FOX_HD = D // FOX_HEADS
ML_HEADS = 8
ML_DV = D // ML_HEADS
ML_DK = ML_DV // 2
ML_QK = ML_HEADS * ML_DK

V7X_VMEM_LIMIT = 56 * 1024 * 1024


def _params(semantics, vmem_bytes):
    return pltpu.CompilerParams(dimension_semantics=semantics,
                                vmem_limit_bytes=min(int(vmem_bytes), V7X_VMEM_LIMIT))


def _rms(xf, g):
    ms = jnp.mean(xf * xf, axis=-1, keepdims=True)
    return xf * lax.rsqrt(ms + EPS) * g


def _nt_dot(a, b):
    return lax.dot_general(a, b, (((1,), (1,)), ((), ())), preferred_element_type=F32)


def _log_sigmoid(z):
    return jnp.minimum(z, 0.0) - jnp.log1p(jnp.exp(-jnp.abs(z)))


def _tri(n, kind, seg=None):
    a = lax.broadcasted_iota(jnp.int32, (n, n), 0)
    b = lax.broadcasted_iota(jnp.int32, (n, n), 1)
    if kind == "le":
        m = b <= a
        if seg is not None:
            m = jnp.logical_and(m, b >= jnp.bitwise_and(a, -seg))
    elif kind == "ge":
        m = b >= a
        if seg is not None:
            m = jnp.logical_and(m, a >= jnp.bitwise_and(b, -seg))
    else:
        m = a > b
    return jnp.where(m, 1.0, 0.0).astype(F32)


def _ffn_body(x_ref, g_ref, wg_ref, wu_ref, wd_ref, go_ref, o_ref, h_ref, *, nj, normalize_out):
    j = pl.program_id(1)

    @pl.when(j == 0)
    def _():
        xf = x_ref[...]
        h_ref[...] = _rms(xf, g_ref[...]).astype(BF)
        o_ref[...] = xf

    h = h_ref[...]
    gt = jnp.dot(h, wg_ref[...], preferred_element_type=F32)
    up = jnp.dot(h, wu_ref[...], preferred_element_type=F32)
    a = (gt * jax.nn.sigmoid(gt) * up).astype(BF)
    o_ref[...] += jnp.dot(a, wd_ref[...], preferred_element_type=F32)

    if normalize_out:
        @pl.when(j == nj - 1)
        def _():
            o_ref[...] = _rms(o_ref[...], go_ref[...])


def _ffn(x, g, wg, wu, wd_half, out_gain, *, normalize_out, tm=1024, tf=512):
    R = x.shape[0]
    tm = min(tm, R)
    nj = D_FF // tf
    vmem = 4 * tm * D * 4 + tm * D * 2 + 6 * D * tf * 2 + 5 * tm * tf * 4
    return pl.pallas_call(
        functools.partial(_ffn_body, nj=nj, normalize_out=normalize_out),
        grid=(R // tm, nj),
        in_specs=[
            pl.BlockSpec((tm, D), lambda i, j: (i, 0)),
            pl.BlockSpec((1, D), lambda i, j: (0, 0)),
            pl.BlockSpec((D, tf), lambda i, j: (0, j)),
            pl.BlockSpec((D, tf), lambda i, j: (0, j)),
            pl.BlockSpec((tf, D), lambda i, j: (j, 0)),
            pl.BlockSpec((1, D), lambda i, j: (0, 0)),
        ],
        out_specs=pl.BlockSpec((tm, D), lambda i, j: (i, 0)),
        out_shape=jax.ShapeDtypeStruct((R, D), F32),
        scratch_shapes=[pltpu.VMEM((tm, D), BF)],
        compiler_params=_params(("arbitrary", "arbitrary"), vmem),
        name="ffn",
    )(x, g, wg, wu, wd_half, out_gain)


def _mm_res_body(a_ref, w_ref, x_ref, o_ref):
    o_ref[...] = x_ref[...] + jnp.dot(a_ref[...], w_ref[...], preferred_element_type=F32)


def _mm_res(a, w, x, *, tm=512):
    R = x.shape[0]
    vmem = 2 * tm * D * 2 + 2 * D * D * 2 + 5 * tm * D * 4
    return pl.pallas_call(
        _mm_res_body,
        grid=(R // tm,),
        in_specs=[
            pl.BlockSpec((tm, D), lambda i: (i, 0)),
            pl.BlockSpec((D, D), lambda i: (0, 0)),
            pl.BlockSpec((tm, D), lambda i: (i, 0)),
        ],
        out_specs=pl.BlockSpec((tm, D), lambda i: (i, 0)),
        out_shape=jax.ShapeDtypeStruct((R, D), F32),
        compiler_params=_params(("arbitrary",), vmem),
        name="mm_res",
    )(a, w, x)


def _pool_body(x_ref, g_ref, hist_ref, w_ref, sc_ref, o_ref, st_ref, xe_ref, *, tm, pos0):
    t = pl.program_id(1)

    @pl.when(t == 0)
    def _():
        xe_ref[0:POOL_PAD, :] = hist_ref[0]

    xf = x_ref[...]
    h = _rms(xf, g_ref[...])
    xe_ref[POOL_PAD:POOL_PAD + tm, :] = h
    row = lax.broadcasted_iota(jnp.int32, (tm, POOL_GW), 0)
    pos1 = (row + (pos0 + 1) + t * tm).astype(F32)
    for gi, w in enumerate(POOL_WINDOWS):
        c0 = gi * POOL_GW
        hg = h[:, c0:c0 + POOL_GW]
        s = hg
        for back in range(1, w):
            s = s + xe_ref[POOL_PAD - back:POOL_PAD - back + tm, c0:c0 + POOL_GW]
        cnt = jnp.minimum(pos1, float(w))
        pooled = s / cnt - hg
        y = jnp.dot(pooled.astype(BF), w_ref[gi], preferred_element_type=F32)
        o_ref[:, c0:c0 + POOL_GW] = xf[:, c0:c0 + POOL_GW] + y * sc_ref[:, c0:c0 + POOL_GW]
    tail = xe_ref[tm:tm + POOL_PAD, :]
    st_ref[0] = tail
    xe_ref[0:POOL_PAD, :] = tail


def _pool_mixer(x, g, hist, w, scale, *, B, T, tm, pos0):
    nt = T // tm
    vmem = 4 * tm * D * 4 + (tm + POOL_PAD) * D * 4 + 2 * 4 * POOL_GW * POOL_GW * 2 + 6 * tm * D * 4
    return pl.pallas_call(
        functools.partial(_pool_body, tm=tm, pos0=pos0),
        grid=(B, nt),
        in_specs=[
            pl.BlockSpec((tm, D), lambda b, t: (b * nt + t, 0)),
            pl.BlockSpec((1, D), lambda b, t: (0, 0)),
            pl.BlockSpec((1, POOL_PAD, D), lambda b, t: (b, 0, 0)),
            pl.BlockSpec((len(POOL_WINDOWS), POOL_GW, POOL_GW), lambda b, t: (0, 0, 0)),
            pl.BlockSpec((1, D), lambda b, t: (0, 0)),
        ],
        out_specs=[
            pl.BlockSpec((tm, D), lambda b, t: (b * nt + t, 0)),
            pl.BlockSpec((1, POOL_PAD, D), lambda b, t: (b, 0, 0)),
        ],
        out_shape=[jax.ShapeDtypeStruct((B * T, D), F32), jax.ShapeDtypeStruct((B, POOL_PAD, D), F32)],
        scratch_shapes=[pltpu.VMEM((tm + POOL_PAD, D), F32)],
        compiler_params=_params(("arbitrary", "arbitrary"), vmem),
        name="pool_mixer",
    )(x, g, hist, w, scale)


def _lru_body(gate_ref, xr_ref, cst_ref, h0_ref, cw_ref, cb_ref, wa_ref, ba_ref, wi_ref, bi_ref, lam_ref,
              y_ref, cso_ref, ho_ref, xe_ref, a_ref, u_ref, hc_ref, *, tm):
    t = pl.program_id(1)

    @pl.when(t == 0)
    def _():
        xe_ref[0:CONV_PAD, :] = cst_ref[0]
        hc_ref[...] = h0_ref[0]

    xe_ref[CONV_PAD:CONV_PAD + tm, :] = xr_ref[...]
    base = CONV_PAD - (CONV_W - 1)
    xc = cb_ref[...] + xe_ref[base:base + tm, :] * cw_ref[0:1, :]
    for j in range(1, CONV_W):
        xc = xc + xe_ref[base + j:base + j + tm, :] * cw_ref[j:j + 1, :]
    tail = xe_ref[tm:tm + CONV_PAD, :]
    cso_ref[0] = tail
    xe_ref[0:CONV_PAD, :] = tail

    xcb = xc.astype(BF)
    nlam = -lam_ref[...]
    sp = jnp.maximum(nlam, 0.0) + jnp.log1p(jnp.exp(-jnp.abs(nlam)))
    for n in range(LRU_BLOCKS):
        blk = slice(n * LRU_BW, (n + 1) * LRU_BW)
        ra = jnp.dot(xcb[:, blk], wa_ref[n], preferred_element_type=F32) + ba_ref[:, blk]
        ia = jnp.dot(xcb[:, blk], wi_ref[n], preferred_element_type=F32) + bi_ref[:, blk]
        r = jax.nn.sigmoid(ra)
        ig = jax.nn.sigmoid(ia)
        log_a = -LRU_C * r * sp[:, blk]
        a_ref[:, blk] = jnp.exp(log_a)
        th = jnp.tanh(log_a)
        one_minus_a2 = -2.0 * th / (1.0 - th)
        u_ref[:, blk] = jnp.sqrt(one_minus_a2) * ig * xc[:, blk]

    rowi = lax.broadcasted_iota(jnp.int32, (8, D), 0)

    def group(gi, carry):
        r0 = pl.multiple_of(gi * 8, 8)
        a8 = a_ref[pl.ds(r0, 8), :]
        u8 = u_ref[pl.ds(r0, 8), :]
        for d in (1, 2, 4):
            keep = rowi >= d
            u8 = jnp.where(keep, a8 * pltpu.roll(u8, d, axis=0) + u8, u8)
            a8 = jnp.where(keep, a8 * pltpu.roll(a8, d, axis=0), a8)
        hs8 = a8 * carry + u8
        u_ref[pl.ds(r0, 8), :] = hs8
        return hs8[7:8, :]

    carry = lax.fori_loop(0, tm // 8, group, hc_ref[...])
    hc_ref[...] = carry
    ho_ref[0] = carry
    y_ref[...] = (u_ref[...] * jax.nn.gelu(gate_ref[...])).astype(BF)


def _lru_core(gate, xr, cst, h0, cw, cb, wa, ba, wi, bi, lam, *, B, T, tm):
    nt = T // tm
    vmem = 4 * tm * D * 4 + 2 * tm * D * 2 + (3 * tm + CONV_PAD) * D * 4 + 4 * 8 * LRU_BW * LRU_BW * 2 + 6 * tm * D * 4
    vec = pl.BlockSpec((1, D), lambda b, t: (0, 0))
    wblk = pl.BlockSpec((LRU_BLOCKS, LRU_BW, LRU_BW), lambda b, t: (0, 0, 0))
    return pl.pallas_call(
        functools.partial(_lru_body, tm=tm),
        grid=(B, nt),
        in_specs=[
            pl.BlockSpec((tm, D), lambda b, t: (b * nt + t, 0)),
            pl.BlockSpec((tm, D), lambda b, t: (b * nt + t, 0)),
            pl.BlockSpec((1, CONV_PAD, D), lambda b, t: (b, 0, 0)),
            pl.BlockSpec((1, 1, D), lambda b, t: (b, 0, 0)),
            pl.BlockSpec((CONV_W, D), lambda b, t: (0, 0)),
            vec, wblk, vec, wblk, vec, vec,
        ],
        out_specs=[
            pl.BlockSpec((tm, D), lambda b, t: (b * nt + t, 0)),
            pl.BlockSpec((1, CONV_PAD, D), lambda b, t: (b, 0, 0)),
            pl.BlockSpec((1, 1, D), lambda b, t: (b, 0, 0)),
        ],
        out_shape=[
            jax.ShapeDtypeStruct((B * T, D), BF),
            jax.ShapeDtypeStruct((B, CONV_PAD, D), F32),
            jax.ShapeDtypeStruct((B, 1, D), F32),
        ],
        scratch_shapes=[
            pltpu.VMEM((tm + CONV_PAD, D), F32),
            pltpu.VMEM((tm, D), F32),
            pltpu.VMEM((tm, D), F32),
            pltpu.VMEM((1, D), F32),
        ],
        compiler_params=_params(("arbitrary", "arbitrary"), vmem),
        name="rglru_core",
    )(gate, xr, cst, h0, cw, cb, wa, ba, wi, bi, lam)


def _split3_bf16(c):
    hi = c.astype(BF)
    r1 = c - hi.astype(F32)
    mid = r1.astype(BF)
    lo = (r1 - mid.astype(F32)).astype(BF)
    return hi, mid, lo


def _proj_call(body, x, g, w, jcol, extra, extra_specs, out_shape, out_specs, scratch, *, tm, vmem, name):
    R = x.shape[0]
    return pl.pallas_call(
        body,
        grid=(R // tm,),
        in_specs=[
            pl.BlockSpec((tm, D), lambda i: (i, 0)),
            pl.BlockSpec((1, D), lambda i: (0, 0)),
            pl.BlockSpec((D, D), lambda i: (0, jcol), pipeline_mode=pl.Buffered(1)),
            *extra_specs,
        ],
        out_specs=out_specs,
        out_shape=out_shape,
        scratch_shapes=scratch,
        compiler_params=_params(("arbitrary",), vmem),
        name=name,
    )(x, g, w, *extra)


def _normed_dot(x_ref, g_ref, w_ref):
    h = _rms(x_ref[...], g_ref[...]).astype(BF)
    return h, jnp.dot(h, w_ref[...], preferred_element_type=F32)


def _store_heads_interleaved(ref, y, tm):
    for hh in range(FOX_HEADS):
        ref[pl.ds(hh, tm, stride=FOX_HEADS), :] = y[:, hh * FOX_HD:(hh + 1) * FOX_HD]


def _fox_q_body(x_ref, g_ref, w_ref, q_ref, *, qscale):
    _, y = _normed_dot(x_ref, g_ref, w_ref)
    yq = (y * qscale).astype(BF)
    for hh in range(FOX_HEADS):
        q_ref[hh] = yq[:, hh * FOX_HD:(hh + 1) * FOX_HD]


def _fox_k_body(x_ref, g_ref, w_ref, wf_ref, wft_ref, bf_ref, bft_ref,
                k_ref, ka_ref, lf_ref, cc_ref, cr_ref, carc_ref, carr_ref, *, tm, seg, carry):
    h, y = _normed_dot(x_ref, g_ref, w_ref)
    lf = _log_sigmoid(jnp.dot(h, wf_ref[...], preferred_element_type=F32) + bf_ref[...])
    lft = _log_sigmoid(_nt_dot(wft_ref[...], h) + bft_ref[...])
    lf_ref[...] = lf
    cc = jnp.dot(_tri(tm, "le", seg), lf, precision=HIGHEST, preferred_element_type=F32)
    cr = jnp.dot(lft, _tri(tm, "ge", seg), precision=HIGHEST, preferred_element_type=F32)
    if carry:
        @pl.when(pl.program_id(0) == 0)
        def _():
            carc_ref[...] = jnp.zeros_like(carc_ref)
            carr_ref[...] = jnp.zeros_like(carr_ref)

        cc = cc + carc_ref[...]
        cr = cr + carr_ref[...]
        carc_ref[...] = cc[tm - 1:tm, :]
        carr_ref[...] = cr[:, tm - 1:tm]
    cc_ref[...] = cc
    cr_ref[...] = cr

    _store_heads_interleaved(k_ref, y, tm)
    yb = y.astype(BF)
    lane = lax.broadcasted_iota(jnp.int32, (tm, FOX_HD), 1)
    nck = cc * (-LOG2E)
    for hh in range(FOX_HEADS):
        hi, mid, lo = _split3_bf16(nck[:, hh:hh + 1])
        aug = jnp.where(lane == 0, hi.astype(F32),
                        jnp.where(lane == 1, mid.astype(F32), jnp.where(lane == 2, lo.astype(F32), 0.0)))
        ka_ref[hh, :, 0:FOX_HD] = yb[:, hh * FOX_HD:(hh + 1) * FOX_HD]
        ka_ref[hh, :, FOX_HD:2 * FOX_HD] = aug.astype(BF)


def _fox_v_body(x_ref, g_ref, w_ref, v_ref, vt_ref, *, tm):
    _, y = _normed_dot(x_ref, g_ref, w_ref)
    _store_heads_interleaved(v_ref, y, tm)
    for hh in range(FOX_HEADS):
        vt_ref[hh] = y[:, hh * FOX_HD:(hh + 1) * FOX_HD].T.astype(BF)


def _fox_proj(x, g, w, wf, wft, bf, bft, *, tm, seg, carry):
    R = x.shape[0]
    H = FOX_HEADS
    qscale = (FOX_HD ** -0.5) * LOG2E
    base = 2 * tm * D * 4 + D * D * 2 + 2 * tm * D * 4 + tm * D * 2
    const = lambda shape: pl.BlockSpec(shape, lambda i: (0,) * len(shape))
    native = pl.BlockSpec((tm * H, FOX_HD), lambda i: (i, 0))
    qh = _proj_call(
        functools.partial(_fox_q_body, qscale=qscale), x, g, w, 0, (), (),
        jax.ShapeDtypeStruct((H, R, FOX_HD), BF), pl.BlockSpec((H, tm, FOX_HD), lambda i: (0, i, 0)), (),
        tm=tm, vmem=base + 3 * tm * D * 2, name="fox_proj_q")
    k, ka, lf, cc, cr = _proj_call(
        functools.partial(_fox_k_body, tm=tm, seg=seg, carry=carry), x, g, w, 1,
        (wf, wft, bf, bft), (const((D, H)), const((H, D)), const((1, H)), const((H, 1))),
        [
            jax.ShapeDtypeStruct((R * H, FOX_HD), F32),
            jax.ShapeDtypeStruct((H, R, 2 * FOX_HD), BF),
            jax.ShapeDtypeStruct((R, H), F32),
            jax.ShapeDtypeStruct((R, H), F32),
            jax.ShapeDtypeStruct((H, R), F32),
        ],
        [
            native,
            pl.BlockSpec((H, tm, 2 * FOX_HD), lambda i: (0, i, 0)),
            pl.BlockSpec((tm, H), lambda i: (i, 0)),
            pl.BlockSpec((tm, H), lambda i: (i, 0)),
            pl.BlockSpec((H, tm), lambda i: (0, i)),
        ],
        [pltpu.VMEM((1, H), F32), pltpu.VMEM((H, 1), F32)],
        tm=tm, vmem=base + 2 * tm * D * 4 + 5 * tm * D * 2 + 3 * tm * tm * 4 + 4 * tm * 128 * 4, name="fox_proj_k")
    v, vt = _proj_call(
        functools.partial(_fox_v_body, tm=tm), x, g, w, 2, (), (),
        [jax.ShapeDtypeStruct((R * H, FOX_HD), F32), jax.ShapeDtypeStruct((H, FOX_HD, R), BF)],
        [native, pl.BlockSpec((H, FOX_HD, tm), lambda i: (0, 0, i))], (),
        tm=tm, vmem=base + 2 * tm * D * 4 + 3 * tm * D * 2 + tm * D * 4, name="fox_proj_v")
    return qh, k, v, ka, vt, lf, cc, cr


def _softmax_step(s_biased, cq, pv, m_ref, l_ref, acc_ref):
    m_old = m_ref[...]
    m_new = jnp.maximum(m_old, jnp.max(s_biased, axis=-1, keepdims=True) + cq)
    p = jnp.exp2(s_biased - (m_new - cq))
    alpha = jnp.exp2(m_old - m_new)
    l_ref[...] = alpha * l_ref[...] + jnp.sum(p, axis=-1, keepdims=True)
    acc_ref[...] = alpha * acc_ref[...] + pv(p.astype(BF))
    m_ref[...] = m_new


def _fox_attn_body(q_ref, ka_ref, vt_ref, cr_ref, o_ref, qt_ref, m_ref, l_ref, acc_ref, sa_ref, sb_ref, *, tq):
    h = pl.program_id(0)
    qi = pl.program_id(1)
    q0 = pl.multiple_of(qi * tq, tq)
    qt_ref[0:FOX_HD, :] = q_ref[0].astype(F32).T.astype(BF)
    sub = lax.broadcasted_iota(jnp.int32, (FOX_HD, tq), 0)
    qt_ref[FOX_HD:2 * FOX_HD, :] = jnp.where(sub < 3, 1.0, 0.0).astype(BF)
    cq = cr_ref[pl.ds(h, 1), pl.ds(q0, tq)] * LOG2E
    m_ref[...] = jnp.full_like(m_ref, NEG)
    l_ref[...] = jnp.zeros_like(l_ref)
    acc_ref[...] = jnp.zeros_like(acc_ref)

    def scores(kk, st_ref):
        k0 = pl.multiple_of(kk * tq, tq)
        st_ref[...] = jnp.dot(ka_ref[0, pl.ds(k0, tq), :], qt_ref[...], preferred_element_type=F32)

    def fold(kk, st_ref, masked):
        k0 = pl.multiple_of(kk * tq, tq)
        st = st_ref[...]
        if masked:
            kj = lax.broadcasted_iota(jnp.int32, (tq, tq), 0)
            qcol = lax.broadcasted_iota(jnp.int32, (tq, tq), 1)
            st = jnp.where(kj <= qcol, st, NEG)
        m_old = m_ref[...]
        m_new = jnp.maximum(m_old, jnp.max(st, axis=0, keepdims=True) + cq)
        p = jnp.exp2(st - (m_new - cq))
        alpha = jnp.exp2(m_old - m_new)
        l_ref[...] = alpha * l_ref[...] + jnp.sum(p, axis=0, keepdims=True)
        acc_ref[...] = alpha * acc_ref[...] + jnp.dot(
            vt_ref[0, :, pl.ds(k0, tq)], p.astype(BF), preferred_element_type=F32)
        m_ref[...] = m_new

    scores(0, sa_ref)

    def pair(pp, _):
        scores(2 * pp + 1, sb_ref)
        fold(2 * pp, sa_ref, False)
        scores(2 * pp + 2, sa_ref)
        fold(2 * pp + 1, sb_ref, False)
        return 0

    lax.fori_loop(0, qi // 2, pair, 0)

    @pl.when(qi % 2 == 1)
    def _():
        scores(qi, sb_ref)
        fold(qi - 1, sa_ref, False)
        fold(qi, sb_ref, True)

    @pl.when(qi % 2 == 0)
    def _():
        fold(qi, sa_ref, True)

    o_ref[...] = (acc_ref[...] / l_ref[...]).T.astype(BF)


def _fox_attn(qh, ka, vt, cr, *, T, tq=512):
    H = FOX_HEADS
    vmem = 2 * T * 2 * FOX_HD * 2 + 2 * T * FOX_HD * 2 + 2 * H * T * 4 + 8 * tq * tq * 4 + 16 * tq * FOX_HD * 4
    return pl.pallas_call(
        functools.partial(_fox_attn_body, tq=tq),
        grid=(H, T // tq),
        in_specs=[
            pl.BlockSpec((1, tq, FOX_HD), lambda h, i: (h, i, 0)),
            pl.BlockSpec((1, T, 2 * FOX_HD), lambda h, i: (h, 0, 0)),
            pl.BlockSpec((1, FOX_HD, T), lambda h, i: (h, 0, 0)),
            pl.BlockSpec((H, T), lambda h, i: (0, 0)),
        ],
        out_specs=pl.BlockSpec((tq, FOX_HD), lambda h, i: (i, h)),
        out_shape=jax.ShapeDtypeStruct((T, D), BF),
        scratch_shapes=[pltpu.VMEM((2 * FOX_HD, tq), BF), pltpu.VMEM((1, tq), F32), pltpu.VMEM((1, tq), F32),
                        pltpu.VMEM((FOX_HD, tq), F32), pltpu.VMEM((tq, tq), F32), pltpu.VMEM((tq, tq), F32)],
        compiler_params=_params(("arbitrary", "arbitrary"), vmem),
        name="fox_attention",
    )(qh, ka, vt, cr)


def _suffix_body(x_ref, o_ref, car_ref, *, tc):
    @pl.when(pl.program_id(1) == 0)
    def _():
        car_ref[...] = jnp.zeros_like(car_ref)

    x = x_ref[0]
    o_ref[0] = jnp.dot(x, _tri(tc, "gt"), precision=HIGHEST, preferred_element_type=F32) + car_ref[...]
    car_ref[...] = car_ref[...] + jnp.sum(x, axis=1, keepdims=True)


def _suffix_sum(x, *, tc=512):
    B, H, S = x.shape
    nc = S // tc
    return pl.pallas_call(
        functools.partial(_suffix_body, tc=tc),
        grid=(B, nc),
        in_specs=[pl.BlockSpec((1, H, tc), lambda b, t: (b, 0, nc - 1 - t))],
        out_specs=pl.BlockSpec((1, H, tc), lambda b, t: (b, 0, nc - 1 - t)),
        out_shape=jax.ShapeDtypeStruct((B, H, S), F32),
        scratch_shapes=[pltpu.VMEM((H, 1), F32)],
        compiler_params=_params(("arbitrary", "arbitrary"), 8 * tc * tc * 4),
        name="suffix_sum",
    )(x)


def _fox_decode_body(q_ref, ka_ref, vt_ref, kc_ref, vc_ref, dsuf_ref, ec_ref, er_ref, o_ref,
                     m_ref, l_ref, acc_ref, *, nt, tq, ts):
    b = pl.program_id(0)
    t = pl.program_id(1)
    H = FOX_HEADS

    @pl.when(t == 0)
    def _():
        m_ref[...] = jnp.full_like(m_ref, NEG)
        l_ref[...] = jnp.zeros_like(l_ref)
        acc_ref[...] = jnp.zeros_like(acc_ref)

    ec = ec_ref[...] * LOG2E
    dsuf = dsuf_ref[0] * LOG2E
    for hh in range(H):
        kt = kc_ref[0, pl.ds(hh, ts, stride=H), :].astype(BF)
        vt = vc_ref[0, pl.ds(hh, ts, stride=H), :].astype(BF)
        s = _nt_dot(q_ref[hh], kt) + dsuf[hh:hh + 1, :]
        _softmax_step(s, ec[:, hh:hh + 1], lambda p, vt=vt: jnp.dot(p, vt, preferred_element_type=F32),
                      m_ref.at[hh], l_ref.at[hh], acc_ref.at[hh])

    @pl.when(t == nt - 1)
    def _():
        n_new = er_ref.shape[1]
        er = er_ref[...] * LOG2E
        r = lax.broadcasted_iota(jnp.int32, (tq, n_new), 0)
        c = lax.broadcasted_iota(jnp.int32, (tq, n_new), 1) - b * tq
        valid = jnp.logical_and(c >= 0, c <= r)
        for hh in range(H):
            s = _nt_dot(q_ref[hh], ka_ref[hh][:, 0:FOX_HD]) - er[hh:hh + 1, :]
            s = jnp.where(valid, s, NEG)
            _softmax_step(s, ec[:, hh:hh + 1], lambda p, hh=hh: _nt_dot(p, vt_ref[hh]),
                          m_ref.at[hh], l_ref.at[hh], acc_ref.at[hh])
            o_ref[:, hh * FOX_HD:(hh + 1) * FOX_HD] = (acc_ref[hh] / l_ref[hh]).astype(BF)


def _fox_decode(qh, ka, vt, kc, vc, dsuf, ec, er, *, B, tq, ts=512):
    H = FOX_HEADS
    S = kc.shape[1] // H
    R = B * tq
    nt = S // ts
    vmem = (4 * ts * D * 4 + 2 * H * R * 3 * FOX_HD * 2 + 2 * H * tq * FOX_HD * 2 + 3 * H * tq * 128 * 4
            + 4 * ts * FOX_HD * 2 + 8 * tq * max(ts, R) * 4)
    return pl.pallas_call(
        functools.partial(_fox_decode_body, nt=nt, tq=tq, ts=ts),
        grid=(B, nt),
        in_specs=[
            pl.BlockSpec((H, tq, FOX_HD), lambda b, t: (0, b, 0)),
            pl.BlockSpec((H, R, 2 * FOX_HD), lambda b, t: (0, 0, 0)),
            pl.BlockSpec((H, FOX_HD, R), lambda b, t: (0, 0, 0)),
            pl.BlockSpec((1, ts * H, FOX_HD), lambda b, t: (b, t, 0)),
            pl.BlockSpec((1, ts * H, FOX_HD), lambda b, t: (b, t, 0)),
            pl.BlockSpec((1, H, ts), lambda b, t: (b, 0, t)),
            pl.BlockSpec((tq, H), lambda b, t: (b, 0)),
            pl.BlockSpec((H, R), lambda b, t: (0, 0)),
        ],
        out_specs=pl.BlockSpec((tq, D), lambda b, t: (b, 0)),
        out_shape=jax.ShapeDtypeStruct((R, D), BF),
        scratch_shapes=[pltpu.VMEM((H, tq, 1), F32), pltpu.VMEM((H, tq, 1), F32), pltpu.VMEM((H, tq, FOX_HD), F32)],
        compiler_params=_params(("arbitrary", "arbitrary"), vmem),
        name="fox_decode_attention",
    )(qh, ka, vt, kc, vc, dsuf, ec, er)


def _mlstm_qk_body(x_ref, g_ref, w_ref, wg_ref, wgt_ref, bg_ref, bgt_ref, q_ref, k_ref, gc_ref, gr_ref, *, qscale):
    h, y = _normed_dot(x_ref, g_ref, w_ref)
    zc = jnp.dot(h, wg_ref[...], preferred_element_type=F32) + bg_ref[...]
    lane = lax.broadcasted_iota(jnp.int32, zc.shape, 1)
    gc_ref[...] = jnp.where(lane < ML_HEADS, zc, _log_sigmoid(zc))
    zr = _nt_dot(wgt_ref[...], h) + bgt_ref[...]
    sub = lax.broadcasted_iota(jnp.int32, zr.shape, 0)
    gr_ref[...] = jnp.where(sub < ML_HEADS, zr, _log_sigmoid(zr))
    q_ref[...] = (y[:, :ML_QK] * qscale).astype(BF)
    k_ref[...] = y[:, ML_QK:]


def _proj_bf16_body(x_ref, g_ref, w_ref, o_ref):
    o_ref[...] = _normed_dot(x_ref, g_ref, w_ref)[1].astype(BF)


def _proj_f32_body(x_ref, g_ref, w_ref, o_ref):
    o_ref[...] = _normed_dot(x_ref, g_ref, w_ref)[1]


def _proj_plain(x, g, w, jcol, dtype, *, tm=512, name):
    R = x.shape[0]
    body = _proj_bf16_body if dtype == BF else _proj_f32_body
    vmem = 2 * tm * D * 4 + D * D * 2 + 2 * tm * D * 4 + tm * D * 2 + 2 * tm * D * 4
    return _proj_call(body, x, g, w, jcol, (), (), jax.ShapeDtypeStruct((R, D), dtype),
                      pl.BlockSpec((tm, D), lambda i: (i, 0)), (), tm=tm, vmem=vmem, name=name)


def _mlstm_proj(x, g, w, wg, wgt, bg, bgt, *, tm=512):
    R = x.shape[0]
    G = 2 * ML_HEADS
    const = lambda shape: pl.BlockSpec(shape, lambda i: (0,) * len(shape))
    vmem = 2 * tm * D * 4 + D * D * 2 + 2 * tm * D * 4 + tm * D * 2 + 2 * tm * D * 4
    q, k, gc, gr = _proj_call(
        functools.partial(_mlstm_qk_body, qscale=ML_DK ** -0.5), x, g, w, 0,
        (wg, wgt, bg, bgt), (const((D, G)), const((G, D)), const((1, G)), const((G, 1))),
        [
            jax.ShapeDtypeStruct((R, ML_QK), BF),
            jax.ShapeDtypeStruct((R, ML_QK), F32),
            jax.ShapeDtypeStruct((R, G), F32),
            jax.ShapeDtypeStruct((G, R), F32),
        ],
        [
            pl.BlockSpec((tm, ML_QK), lambda i: (i, 0)),
            pl.BlockSpec((tm, ML_QK), lambda i: (i, 0)),
            pl.BlockSpec((tm, G), lambda i: (i, 0)),
            pl.BlockSpec((G, tm), lambda i: (0, i)),
        ],
        (), tm=tm, vmem=vmem, name="mlstm_proj_qk")
    v = _proj_plain(x, g, w, 1, BF, tm=tm, name="mlstm_proj_v")
    o = _proj_plain(x, g, w, 2, F32, tm=tm, name="mlstm_proj_o")
    return q, k, v, o, gc, gr


def _mlstm_body(q_ref, k_ref, v_ref, o_ref, gc_ref, gr_ref, gn_ref, c0_ref, n0_ref, m0_ref,
                y_ref, co_ref, no_ref, mo_ref, c_ref, n_ref, m_ref, *, L):
    t = pl.program_id(1)

    @pl.when(t == 0)
    def _():
        c_ref[...] = c0_ref[0]
        n_ref[...] = n0_ref[0]
        m_ref[...] = m0_ref[0]

    gc = gc_ref[...]
    gr = gr_ref[0]
    ra = lax.broadcasted_iota(jnp.int32, (L, L), 0)
    cb = lax.broadcasted_iota(jnp.int32, (L, L), 1)
    causal = cb <= ra
    bcs = jnp.dot(_tri(L, "le"), gc, precision=HIGHEST, preferred_element_type=F32)
    brs = jnp.dot(gr, _tri(L, "ge"), precision=HIGHEST, preferred_element_type=F32)
    for hh in range(ML_HEADS):
        b_c = bcs[:, ML_HEADS + hh:ML_HEADS + hh + 1]
        g_c = gc[:, hh:hh + 1] - b_c
        g_r = gr[hh:hh + 1, :] - brs[ML_HEADS + hh:ML_HEADS + hh + 1, :]
        m_h = m_ref[hh][:, 0:1]
        am = jnp.where(causal, b_c + g_r, NEG)
        mt = jnp.maximum(b_c + m_h, jnp.max(am, axis=-1, keepdims=True))
        d = jnp.exp(am - mt)
        qh = q_ref[:, hh * ML_DK:(hh + 1) * ML_DK]
        kf = k_ref[:, hh * ML_DK:(hh + 1) * ML_DK]
        vh = v_ref[:, hh * ML_DV:(hh + 1) * ML_DV]
        sc = _nt_dot(qh, kf.astype(BF)) * d
        inter = jnp.exp(b_c + m_h - mt)
        c_old = c_ref[hh]
        n_old = n_ref[hh]
        num = (jnp.dot(sc.astype(BF), vh, preferred_element_type=F32)
               + inter * jnp.dot(qh, c_old.astype(BF), preferred_element_type=F32))
        den = (jnp.sum(sc, axis=-1, keepdims=True)
               + inter * jnp.sum(qh.astype(F32) * n_old, axis=-1, keepdims=True))
        hout = num / jnp.maximum(jnp.abs(den), jnp.exp(-mt))
        m_new = mt[L - 1:L, :]
        b_last = b_c[L - 1:L, :]
        decay = jnp.exp(b_last + m_h - m_new)
        wk = jnp.exp(b_last + g_c - m_new) * kf
        c_ref[hh] = decay * c_old + lax.dot_general(
            wk.astype(BF), vh, (((0,), (0,)), ((), ())), preferred_element_type=F32)
        n_ref[hh] = decay * n_old + jnp.sum(wk, axis=0, keepdims=True)
        m_ref[hh] = jnp.broadcast_to(m_new, (1, 128))
        hn = hout * lax.rsqrt(jnp.mean(hout * hout, axis=-1, keepdims=True) + EPS)
        cols = slice(hh * ML_DV, (hh + 1) * ML_DV)
        y_ref[:, cols] = (hn * gn_ref[:, cols] * jax.nn.sigmoid(o_ref[:, cols])).astype(BF)

    co_ref[0] = c_ref[...]
    no_ref[0] = n_ref[...]
    mo_ref[0] = m_ref[...]


def _mlstm_core(q, k, v, o, gc, gr, gn, c0, n0, m0, *, B, T, L):
    nt = T // L
    G = 2 * ML_HEADS
    H = ML_HEADS
    vmem = (2 * L * (ML_QK * 6 + D * 8) + 6 * H * ML_DK * ML_DV * 4 + 16 * L * L * 4 + 12 * L * ML_DV * 4
            + 4 * 1024 * 1024)
    rows = lambda w: pl.BlockSpec((L, w), lambda b, t: (b * nt + t, 0))
    st_c = pl.BlockSpec((1, H, ML_DK, ML_DV), lambda b, t: (b, 0, 0, 0))
    st_n = pl.BlockSpec((1, H, 1, ML_DK), lambda b, t: (b, 0, 0, 0))
    st_m = pl.BlockSpec((1, H, 1, 128), lambda b, t: (b, 0, 0, 0))
    return pl.pallas_call(
        functools.partial(_mlstm_body, L=L),
        grid=(B, nt),
        in_specs=[
            rows(ML_QK), rows(ML_QK), rows(D), rows(D), rows(G),
            pl.BlockSpec((1, G, L), lambda b, t: (b, 0, t)),
            pl.BlockSpec((1, D), lambda b, t: (0, 0)),
            st_c, st_n, st_m,
        ],
        out_specs=[rows(D), st_c, st_n, st_m],
        out_shape=[
            jax.ShapeDtypeStruct((B * T, D), BF),
            jax.ShapeDtypeStruct((B, H, ML_DK, ML_DV), F32),
            jax.ShapeDtypeStruct((B, H, 1, ML_DK), F32),
            jax.ShapeDtypeStruct((B, H, 1, 128), F32),
        ],
        scratch_shapes=[
            pltpu.VMEM((H, ML_DK, ML_DV), F32),
            pltpu.VMEM((H, 1, ML_DK), F32),
            pltpu.VMEM((H, 1, 128), F32),
        ],
        compiler_params=_params(("arbitrary", "arbitrary"), vmem),
        name="mlstm_core",
    )(q, k, v, o, gc, gr, gn, c0, n0, m0)


def kernel(x_prompt, x_sample, state_pool, state_lru_conv, state_lru_h, cache_fox_k, cache_fox_v, cache_fox_logf, state_mlstm_c, state_mlstm_n, state_mlstm_m, ffn1_norm, ffn1_w_gate, ffn1_w_up, ffn1_w_down, mix_norm, ffn2_norm, ffn2_w_gate, ffn2_w_up, ffn2_w_down, pool_w, pool_scale, lru_w_in, lru_conv_w, lru_conv_b, lru_w_a, lru_b_a, lru_w_i, lru_b_i, lru_lambda, lru_w_out, fox_w_qkv, fox_w_f, fox_b_f, fox_w_o, mlstm_w_in, mlstm_b_i, mlstm_b_f, mlstm_norm, mlstm_w_out, final_norm):
    BP, TP, _ = x_prompt.shape
    BS, TS, _ = x_sample.shape
    assert BP == 1 and x_prompt.shape[2] == D and x_sample.shape[2] == D
    RS = BS * TS
    past = cache_fox_k.shape[1]
    row = lambda p: p.reshape(1, -1).astype(F32)

    xp = x_prompt.reshape(TP, D)
    xs = x_sample.reshape(RS, D)

    def ffn(xp, xs, g, wg, wu, wd, last=False):
        g, wg, wu, wd = row(g), wg.astype(BF), wu.astype(BF), (0.5 * wd).astype(BF)
        gf = row(final_norm)
        return (_ffn(xp, g, wg, wu, wd, gf, normalize_out=last), _ffn(xs, g, wg, wu, wd, gf, normalize_out=last))

    xp, xs = ffn(xp, xs, ffn1_norm[0], ffn1_w_gate[0], ffn1_w_up[0], ffn1_w_down[0])
    pw = pool_w.astype(BF)
    hist_s = jnp.pad(state_pool.astype(F32), ((0, 0), (POOL_PAD - state_pool.shape[1], 0), (0, 0)))
    xp, pool_p = _pool_mixer(xp, row(mix_norm[0]), jnp.zeros((BP, POOL_PAD, D), F32), pw, row(pool_scale),
                             B=BP, T=TP, tm=512, pos0=0)
    xs, pool_s = _pool_mixer(xs, row(mix_norm[0]), hist_s, pw, row(pool_scale), B=BS, T=TS, tm=TS, pos0=past)
    pool_p = pool_p[:, 1:]
    pool_s = pool_s[:, 1:]
    xp, xs = ffn(xp, xs, ffn2_norm[0], ffn2_w_gate[0], ffn2_w_up[0], ffn2_w_down[0])

    xp, xs = ffn(xp, xs, ffn1_norm[1], ffn1_w_gate[1], ffn1_w_up[1], ffn1_w_down[1])
    w_in = lru_w_in.astype(BF)
    lru_args = (lru_conv_w.astype(F32), row(lru_conv_b), lru_w_a.astype(BF), row(lru_b_a),
                lru_w_i.astype(BF), row(lru_b_i), row(lru_lambda))
    lru_in = lambda x, jcol: _proj_plain(x, row(mix_norm[1]), w_in, jcol, F32, name="rglru_proj")
    cst_s = jnp.pad(state_lru_conv.astype(F32), ((0, 0), (CONV_PAD - (CONV_W - 1), 0), (0, 0)))
    yp, conv_p, h_p = _lru_core(lru_in(xp, 0), lru_in(xp, 1), jnp.zeros((BP, CONV_PAD, D), F32),
                                jnp.zeros((BP, 1, D), F32), *lru_args, B=BP, T=TP, tm=256)
    ys, conv_s, h_s = _lru_core(lru_in(xs, 0), lru_in(xs, 1), cst_s, state_lru_h.astype(F32).reshape(BS, 1, D),
                                *lru_args, B=BS, T=TS, tm=TS)
    w_out = lru_w_out.astype(BF)
    xp = _mm_res(yp, w_out, xp)
    xs = _mm_res(ys, w_out, xs)
    lru_conv_p, lru_conv_s = conv_p[:, CONV_PAD - (CONV_W - 1):], conv_s[:, CONV_PAD - (CONV_W - 1):]
    lru_h_p, lru_h_s = h_p.reshape(BP, D), h_s.reshape(BS, D)
    xp, xs = ffn(xp, xs, ffn2_norm[1], ffn2_w_gate[1], ffn2_w_up[1], ffn2_w_down[1])

    xp, xs = ffn(xp, xs, ffn1_norm[2], ffn1_w_gate[2], ffn1_w_up[2], ffn1_w_down[2])
    w_qkv = fox_w_qkv.astype(BF)
    wf = fox_w_f.astype(BF)
    wft = wf.T
    bf_r = fox_b_f.astype(F32).reshape(1, FOX_HEADS)
    bf_c = fox_b_f.astype(F32).reshape(FOX_HEADS, 1)
    qh_p, k_p, v_p, ka_p, vt_p, lf_p, cc_p, cr_p = _fox_proj(
        xp, row(mix_norm[2]), w_qkv, wf, wft, bf_r, bf_c, tm=512, seg=512, carry=True)
    qh_s, k_s, v_s, ka_s, vt_s, lf_s, cc_s, cr_s = _fox_proj(
        xs, row(mix_norm[2]), w_qkv, wf, wft, bf_r, bf_c, tm=RS, seg=TS, carry=False)
    o_p = _fox_attn(qh_p, ka_p, vt_p, cr_p, T=TP)
    dsuf = _suffix_sum(jnp.swapaxes(cache_fox_logf.astype(F32), 1, 2))
    o_s = _fox_decode(qh_s, ka_s, vt_s,
                      cache_fox_k.reshape(BS, past * FOX_HEADS, FOX_HD),
                      cache_fox_v.reshape(BS, past * FOX_HEADS, FOX_HD),
                      dsuf, cc_s, cr_s, B=BS, tq=TS)
    w_o = fox_w_o.astype(BF)
    xp = _mm_res(o_p, w_o, xp)
    xs = _mm_res(o_s, w_o, xs)
    fox_k_p = k_p.reshape(BP, TP, FOX_HEADS, FOX_HD)
    fox_v_p = v_p.reshape(BP, TP, FOX_HEADS, FOX_HD)
    fox_k_s = k_s.reshape(BS, TS, FOX_HEADS, FOX_HD)
    fox_v_s = v_s.reshape(BS, TS, FOX_HEADS, FOX_HD)
    fox_lf_p = lf_p.reshape(BP, TP, FOX_HEADS)
    fox_lf_s = lf_s.reshape(BS, TS, FOX_HEADS)
    xp, xs = ffn(xp, xs, ffn2_norm[2], ffn2_w_gate[2], ffn2_w_up[2], ffn2_w_down[2])

    xp, xs = ffn(xp, xs, ffn1_norm[3], ffn1_w_gate[3], ffn1_w_up[3], ffn1_w_down[3])
    n_main = 2 * ML_QK + 2 * D
    w_main = mlstm_w_in[:, :n_main].astype(BF)
    w_gates = mlstm_w_in[:, n_main:].astype(BF)
    b_gates = jnp.concatenate([mlstm_b_i, mlstm_b_f]).astype(F32)
    ml_w = (row(mix_norm[3]), w_main, w_gates, w_gates.T, b_gates.reshape(1, -1), b_gates.reshape(-1, 1))
    q_p, kk_p, vv_p, og_p, gc_p, gr_p = _mlstm_proj(xp, *ml_w)
    q_s, kk_s, vv_s, og_s, gc_s, gr_s = _mlstm_proj(xs, *ml_w)
    G = 2 * ML_HEADS
    gn = row(mlstm_norm)
    yp, ml_c_p, ml_n_p, ml_m_p = _mlstm_core(
        q_p, kk_p, vv_p, og_p, gc_p, gr_p.reshape(1, G, TP), gn,
        jnp.zeros((BP, ML_HEADS, ML_DK, ML_DV), F32), jnp.zeros((BP, ML_HEADS, 1, ML_DK), F32),
        jnp.zeros((BP, ML_HEADS, 1, 128), F32), B=BP, T=TP, L=256)
    m0_s = jnp.broadcast_to(state_mlstm_m.astype(F32)[:, :, None, None], (BS, ML_HEADS, 1, 128))
    ys, ml_c_s, ml_n_s, ml_m_s = _mlstm_core(
        q_s, kk_s, vv_s, og_s, gc_s, jnp.swapaxes(gr_s.reshape(G, BS, TS), 0, 1), gn,
        state_mlstm_c.astype(F32), state_mlstm_n.astype(F32).reshape(BS, ML_HEADS, 1, ML_DK), m0_s,
        B=BS, T=TS, L=TS)
    w_out = mlstm_w_out.astype(BF)
    xp = _mm_res(yp, w_out, xp)
    xs = _mm_res(ys, w_out, xs)
    ml_n_p, ml_n_s = ml_n_p.reshape(BP, ML_HEADS, ML_DK), ml_n_s.reshape(BS, ML_HEADS, ML_DK)
    ml_m_p, ml_m_s = ml_m_p[:, :, 0, 0], ml_m_s[:, :, 0, 0]
    xp, xs = ffn(xp, xs, ffn2_norm[3], ffn2_w_gate[3], ffn2_w_up[3], ffn2_w_down[3], last=True)

    y_prompt = xp.reshape(BP, TP, D)
    y_sample = xs.reshape(BS, TS, D)
    return (y_prompt, y_sample, pool_p, pool_s, lru_conv_p, lru_conv_s, lru_h_p, lru_h_s,
            fox_k_p, fox_k_s, fox_v_p, fox_v_s, fox_lf_p, fox_lf_s,
            ml_c_p, ml_c_s, ml_n_p, ml_n_s, ml_m_p, ml_m_s)
```

```python
import functools
import math

import jax
import jax.numpy as jnp
from jax import lax
from jax.experimental import pallas as pl
from jax.experimental.pallas import tpu as pltpu

F32 = jnp.float32
BF = jnp.bfloat16

D = 2048
D_FF = 5632
EPS = 1e-6
NEG = -1e30
LOG2E = math.log2(math.e)

POOL_WINDOWS = (2, 4, 8, 16)
POOL_GW = D // len(POOL_WINDOWS)
POOL_PAD = 16
CONV_W = 4
CONV_PAD = 8
LRU_BLOCKS = 8
LRU_BW = D // LRU_BLOCKS
LRU_C = 8.0
FOX_HEADS = 16
FOX_HD = D // FOX_HEADS
ML_HEADS = 8
ML_DV = D // ML_HEADS
ML_DK = ML_DV // 2
ML_QK = ML_HEADS * ML_DK

V7X_VMEM_LIMIT = 56 * 1024 * 1024


def _params(semantics, vmem_bytes):
    return pltpu.CompilerParams(dimension_semantics=semantics,
                                vmem_limit_bytes=min(int(vmem_bytes), V7X_VMEM_LIMIT))


def _rms(xf, g):
    ms = jnp.mean(xf * xf, axis=-1, keepdims=True)
    return xf * lax.rsqrt(ms + EPS) * g


def _nt_dot(a, b):
    return lax.dot_general(a, b, (((1,), (1,)), ((), ())), preferred_element_type=F32)


def _log_sigmoid(z):
    return jnp.minimum(z, 0.0) - jnp.log1p(jnp.exp(-jnp.abs(z)))


def _split3_bf16(c):
    hi = c.astype(BF)
    r1 = c - hi.astype(F32)
    mid = r1.astype(BF)
    lo = (r1 - mid.astype(F32)).astype(BF)
    return hi, mid, lo


def _sum_dot_left(tri, x):
    return sum(jnp.dot(tri, part, preferred_element_type=F32) for part in _split3_bf16(x))


def _sum_dot_right(x, tri):
    return sum(jnp.dot(part, tri, preferred_element_type=F32) for part in _split3_bf16(x))


def _tri(n, kind, seg=None):
    a = lax.broadcasted_iota(jnp.int32, (n, n), 0)
    b = lax.broadcasted_iota(jnp.int32, (n, n), 1)
    if kind == "le":
        m = b <= a
        if seg is not None:
            m = jnp.logical_and(m, b >= jnp.bitwise_and(a, -seg))
    elif kind == "ge":
        m = b >= a
        if seg is not None:
            m = jnp.logical_and(m, a >= jnp.bitwise_and(b, -seg))
    else:
        m = a > b
    return jnp.where(m, 1.0, 0.0).astype(BF)


def _ffn_body(x_ref, g_ref, wg_ref, wu_ref, wd_ref, go_ref, o_ref, h_ref, *, nj, normalize_out):
    j = pl.program_id(1)

    @pl.when(j == 0)
    def _():
        xf = x_ref[...]
        h_ref[...] = _rms(xf, g_ref[...]).astype(BF)
        o_ref[...] = xf

    h = h_ref[...]
    gt = jnp.dot(h, wg_ref[...], preferred_element_type=F32)
    up = jnp.dot(h, wu_ref[...], preferred_element_type=F32)
    a = (gt * jax.nn.sigmoid(gt) * up).astype(BF)
    o_ref[...] += jnp.dot(a, wd_ref[...], preferred_element_type=F32)

    if normalize_out:
        @pl.when(j == nj - 1)
        def _():
            o_ref[...] = _rms(o_ref[...], go_ref[...])


def _ffn(x, g, wg, wu, wd_half, out_gain, *, layer, normalize_out, tm=1024, tf=512):
    R = x.shape[0]
    tm = min(tm, R)
    nj = D_FF // tf
    vmem = 4 * tm * D * 4 + tm * D * 2 + 6 * D * tf * 2 + 5 * tm * tf * 4
    return pl.pallas_call(
        functools.partial(_ffn_body, nj=nj, normalize_out=normalize_out),
        grid=(R // tm, nj),
        in_specs=[
            pl.BlockSpec((tm, D), lambda i, j: (i, 0)),
            pl.BlockSpec((1, D), lambda i, j: (0, 0)),
            pl.BlockSpec((None, D, tf), lambda i, j: (layer, 0, j)),
            pl.BlockSpec((None, D, tf), lambda i, j: (layer, 0, j)),
            pl.BlockSpec((None, tf, D), lambda i, j: (layer, j, 0)),
            pl.BlockSpec((1, D), lambda i, j: (0, 0)),
        ],
        out_specs=pl.BlockSpec((tm, D), lambda i, j: (i, 0)),
        out_shape=jax.ShapeDtypeStruct((R, D), F32),
        scratch_shapes=[pltpu.VMEM((tm, D), BF)],
        compiler_params=_params(("arbitrary", "arbitrary"), vmem),
        name="ffn",
    )(x, g, wg, wu, wd_half, out_gain)


def _mm_res_body(a_ref, w_ref, x_ref, o_ref):
    o_ref[...] = x_ref[...] + jnp.dot(a_ref[...], w_ref[...], preferred_element_type=F32)


def _mm_res(a, w, x, *, tm=512):
    R = x.shape[0]
    vmem = 2 * tm * D * 2 + 2 * D * D * 2 + 5 * tm * D * 4
    return pl.pallas_call(
        _mm_res_body,
        grid=(R // tm,),
        in_specs=[
            pl.BlockSpec((tm, D), lambda i: (i, 0)),
            pl.BlockSpec((D, D), lambda i: (0, 0)),
            pl.BlockSpec((tm, D), lambda i: (i, 0)),
        ],
        out_specs=pl.BlockSpec((tm, D), lambda i: (i, 0)),
        out_shape=jax.ShapeDtypeStruct((R, D), F32),
        compiler_params=_params(("arbitrary",), vmem),
        name="mm_res",
    )(a, w, x)


def _pool_body(x_ref, g_ref, hist_ref, w_ref, sc_ref, o_ref, st_ref, xe_ref, *, tm, pos0):
    t = pl.program_id(1)

    @pl.when(t == 0)
    def _():
        xe_ref[0:POOL_PAD, :] = hist_ref[0]

    xf = x_ref[...]
    h = _rms(xf, g_ref[...])
    xe_ref[POOL_PAD:POOL_PAD + tm, :] = h
    row = lax.broadcasted_iota(jnp.int32, (tm, POOL_GW), 0)
    pos1 = (row + (pos0 + 1) + t * tm).astype(F32)
    for gi, w in enumerate(POOL_WINDOWS):
        c0 = gi * POOL_GW
        hg = h[:, c0:c0 + POOL_GW]
        s = hg
        for back in range(1, w):
            s = s + xe_ref[POOL_PAD - back:POOL_PAD - back + tm, c0:c0 + POOL_GW]
        cnt = jnp.minimum(pos1, float(w))
        pooled = s / cnt - hg
        y = jnp.dot(pooled.astype(BF), w_ref[gi], preferred_element_type=F32)
        o_ref[:, c0:c0 + POOL_GW] = xf[:, c0:c0 + POOL_GW] + y * sc_ref[:, c0:c0 + POOL_GW]
    tail = xe_ref[tm:tm + POOL_PAD, :]
    st_ref[0] = tail
    xe_ref[0:POOL_PAD, :] = tail


def _pool_mixer(x, g, hist, w, scale, *, B, T, tm, pos0):
    nt = T // tm
    vmem = 4 * tm * D * 4 + (tm + POOL_PAD) * D * 4 + 2 * 4 * POOL_GW * POOL_GW * 2 + 6 * tm * D * 4
    return pl.pallas_call(
        functools.partial(_pool_body, tm=tm, pos0=pos0),
        grid=(B, nt),
        in_specs=[
            pl.BlockSpec((tm, D), lambda b, t: (b * nt + t, 0)),
            pl.BlockSpec((1, D), lambda b, t: (0, 0)),
            pl.BlockSpec((1, POOL_PAD, D), lambda b, t: (b, 0, 0)),
            pl.BlockSpec((len(POOL_WINDOWS), POOL_GW, POOL_GW), lambda b, t: (0, 0, 0)),
            pl.BlockSpec((1, D), lambda b, t: (0, 0)),
        ],
        out_specs=[
            pl.BlockSpec((tm, D), lambda b, t: (b * nt + t, 0)),
            pl.BlockSpec((1, POOL_PAD, D), lambda b, t: (b, 0, 0)),
        ],
        out_shape=[jax.ShapeDtypeStruct((B * T, D), F32), jax.ShapeDtypeStruct((B, POOL_PAD, D), F32)],
        scratch_shapes=[pltpu.VMEM((tm + POOL_PAD, D), F32)],
        compiler_params=_params(("arbitrary", "arbitrary"), vmem),
        name="pool_mixer",
    )(x, g, hist, w, scale)


def _lru_body(gate_ref, xr_ref, cst_ref, h0_ref, cw_ref, cb_ref, wa_ref, ba_ref, wi_ref, bi_ref, lam_ref,
              y_ref, cso_ref, ho_ref, xe_ref, a_ref, u_ref, hc_ref, *, tm):
    t = pl.program_id(1)

    @pl.when(t == 0)
    def _():
        xe_ref[0:CONV_PAD, :] = cst_ref[0]
        hc_ref[...] = h0_ref[0]

    xe_ref[CONV_PAD:CONV_PAD + tm, :] = xr_ref[...]
    base = CONV_PAD - (CONV_W - 1)
    xc = cb_ref[...] + xe_ref[base:base + tm, :] * cw_ref[0:1, :]
    for j in range(1, CONV_W):
        xc = xc + xe_ref[base + j:base + j + tm, :] * cw_ref[j:j + 1, :]
    tail = xe_ref[tm:tm + CONV_PAD, :]
    cso_ref[0] = tail
    xe_ref[0:CONV_PAD, :] = tail

    xcb = xc.astype(BF)
    nlam = -lam_ref[...]
    sp = jnp.maximum(nlam, 0.0) + jnp.log1p(jnp.exp(-jnp.abs(nlam)))
    for n in range(LRU_BLOCKS):
        blk = slice(n * LRU_BW, (n + 1) * LRU_BW)
        ra = jnp.dot(xcb[:, blk], wa_ref[n], preferred_element_type=F32) + ba_ref[:, blk]
        ia = jnp.dot(xcb[:, blk], wi_ref[n], preferred_element_type=F32) + bi_ref[:, blk]
        r = jax.nn.sigmoid(ra)
        ig = jax.nn.sigmoid(ia)
        log_a = -LRU_C * r * sp[:, blk]
        a_ref[:, blk] = jnp.exp(log_a)
        th = jnp.tanh(log_a)
        one_minus_a2 = -2.0 * th / (1.0 - th)
        u_ref[:, blk] = jnp.sqrt(one_minus_a2) * ig * xc[:, blk]

    rowi = lax.broadcasted_iota(jnp.int32, (8, D), 0)

    def group(gi, carry):
        r0 = pl.multiple_of(gi * 8, 8)
        a8 = a_ref[pl.ds(r0, 8), :]
        u8 = u_ref[pl.ds(r0, 8), :]
        for d in (1, 2, 4):
            keep = rowi >= d
            u8 = jnp.where(keep, a8 * pltpu.roll(u8, d, axis=0) + u8, u8)
            a8 = jnp.where(keep, a8 * pltpu.roll(a8, d, axis=0), a8)
        hs8 = a8 * carry + u8
        u_ref[pl.ds(r0, 8), :] = hs8
        return hs8[7:8, :]

    carry = lax.fori_loop(0, tm // 8, group, hc_ref[...])
    hc_ref[...] = carry
    ho_ref[0] = carry
    y_ref[...] = (u_ref[...] * jax.nn.gelu(gate_ref[...])).astype(BF)


def _lru_core(gate, xr, cst, h0, cw, cb, wa, ba, wi, bi, lam, *, B, T, tm):
    nt = T // tm
    vmem = 4 * tm * D * 4 + 2 * tm * D * 2 + (3 * tm + CONV_PAD) * D * 4 + 4 * 8 * LRU_BW * LRU_BW * 2 + 6 * tm * D * 4
    vec = pl.BlockSpec((1, D), lambda b, t: (0, 0))
    wblk = pl.BlockSpec((LRU_BLOCKS, LRU_BW, LRU_BW), lambda b, t: (0, 0, 0))
    return pl.pallas_call(
        functools.partial(_lru_body, tm=tm),
        grid=(B, nt),
        in_specs=[
            pl.BlockSpec((tm, D), lambda b, t: (b * nt + t, 0)),
            pl.BlockSpec((tm, D), lambda b, t: (b * nt + t, 0)),
            pl.BlockSpec((1, CONV_PAD, D), lambda b, t: (b, 0, 0)),
            pl.BlockSpec((1, 1, D), lambda b, t: (b, 0, 0)),
            pl.BlockSpec((CONV_W, D), lambda b, t: (0, 0)),
            vec, wblk, vec, wblk, vec, vec,
        ],
        out_specs=[
            pl.BlockSpec((tm, D), lambda b, t: (b * nt + t, 0)),
            pl.BlockSpec((1, CONV_PAD, D), lambda b, t: (b, 0, 0)),
            pl.BlockSpec((1, 1, D), lambda b, t: (b, 0, 0)),
        ],
        out_shape=[
            jax.ShapeDtypeStruct((B * T, D), BF),
            jax.ShapeDtypeStruct((B, CONV_PAD, D), F32),
            jax.ShapeDtypeStruct((B, 1, D), F32),
        ],
        scratch_shapes=[
            pltpu.VMEM((tm + CONV_PAD, D), F32),
            pltpu.VMEM((tm, D), F32),
            pltpu.VMEM((tm, D), F32),
            pltpu.VMEM((1, D), F32),
        ],
        compiler_params=_params(("arbitrary", "arbitrary"), vmem),
        name="rglru_core",
    )(gate, xr, cst, h0, cw, cb, wa, ba, wi, bi, lam)


def _proj_call(body, x, g, w, jcol, extra, extra_specs, out_shape, out_specs, scratch, *, tm, vmem, name):
    R = x.shape[0]
    return pl.pallas_call(
        body,
        grid=(R // tm,),
        in_specs=[
            pl.BlockSpec((tm, D), lambda i: (i, 0)),
            pl.BlockSpec((1, D), lambda i: (0, 0)),
            pl.BlockSpec((D, D), lambda i: (0, jcol), pipeline_mode=pl.Buffered(1)),
            *extra_specs,
        ],
        out_specs=out_specs,
        out_shape=out_shape,
        scratch_shapes=scratch,
        compiler_params=_params(("arbitrary",), vmem),
        name=name,
    )(x, g, w, *extra)


def _normed_dot(x_ref, g_ref, w_ref):
    h = _rms(x_ref[...], g_ref[...]).astype(BF)
    return h, jnp.dot(h, w_ref[...], preferred_element_type=F32)


def _store_heads_interleaved(ref, y, tm):
    for hh in range(FOX_HEADS):
        ref[pl.ds(hh, tm, stride=FOX_HEADS), :] = y[:, hh * FOX_HD:(hh + 1) * FOX_HD]


def _fox_q_body(x_ref, g_ref, w_ref, q_ref, *, qscale):
    _, y = _normed_dot(x_ref, g_ref, w_ref)
    yq = (y * qscale).astype(BF)
    for hh in range(FOX_HEADS):
        q_ref[hh] = yq[:, hh * FOX_HD:(hh + 1) * FOX_HD]


def _fox_k_body(x_ref, g_ref, w_ref, wf_ref, wft_ref, bf_ref, bft_ref,
                k_ref, ka_ref, lf_ref, cc_ref, cr_ref, carc_ref, carr_ref, *, tm, seg, carry):
    h, y = _normed_dot(x_ref, g_ref, w_ref)
    lf = _log_sigmoid(jnp.dot(h, wf_ref[...], preferred_element_type=F32) + bf_ref[...])
    lft = _log_sigmoid(_nt_dot(wft_ref[...], h) + bft_ref[...])
    lf_ref[...] = lf
    cc = _sum_dot_left(_tri(tm, "le", seg), lf)
    cr = _sum_dot_right(lft, _tri(tm, "ge", seg))
    if carry:
        @pl.when(pl.program_id(0) == 0)
        def _():
            carc_ref[...] = jnp.zeros_like(carc_ref)
            carr_ref[...] = jnp.zeros_like(carr_ref)

        cc = cc + carc_ref[...]
        cr = cr + carr_ref[...]
        carc_ref[...] = cc[tm - 1:tm, :]
        carr_ref[...] = cr[:, tm - 1:tm]
    cc_ref[...] = cc
    cr_ref[...] = cr

    _store_heads_interleaved(k_ref, y, tm)
    yb = y.astype(BF)
    lane = lax.broadcasted_iota(jnp.int32, (tm, FOX_HD), 1)
    nck = cc * (-LOG2E)
    for hh in range(FOX_HEADS):
        hi, mid, lo = _split3_bf16(nck[:, hh:hh + 1])
        aug = jnp.where(lane == 0, hi.astype(F32),
                        jnp.where(lane == 1, mid.astype(F32), jnp.where(lane == 2, lo.astype(F32), 0.0)))
        ka_ref[hh, :, 0:FOX_HD] = yb[:, hh * FOX_HD:(hh + 1) * FOX_HD]
        ka_ref[hh, :, FOX_HD:2 * FOX_HD] = aug.astype(BF)


def _fox_v_body(x_ref, g_ref, w_ref, v_ref, vt_ref, *, tm):
    _, y = _normed_dot(x_ref, g_ref, w_ref)
    _store_heads_interleaved(v_ref, y, tm)
    for hh in range(FOX_HEADS):
        vt_ref[hh] = y[:, hh * FOX_HD:(hh + 1) * FOX_HD].T.astype(BF)


def _fox_proj(x, g, w, wf, wft, bf, bft, *, tm, seg, carry):
    R = x.shape[0]
    H = FOX_HEADS
    qscale = (FOX_HD ** -0.5) * LOG2E
    base = 2 * tm * D * 4 + D * D * 2 + 2 * tm * D * 4 + tm * D * 2
    const = lambda shape: pl.BlockSpec(shape, lambda i: (0,) * len(shape))
    native = pl.BlockSpec((tm * H, FOX_HD), lambda i: (i, 0))
    qh = _proj_call(
        functools.partial(_fox_q_body, qscale=qscale), x, g, w, 0, (), (),
        jax.ShapeDtypeStruct((H, R, FOX_HD), BF), pl.BlockSpec((H, tm, FOX_HD), lambda i: (0, i, 0)), (),
        tm=tm, vmem=base + 3 * tm * D * 2, name="fox_proj_q")
    k, ka, lf, cc, cr = _proj_call(
        functools.partial(_fox_k_body, tm=tm, seg=seg, carry=carry), x, g, w, 1,
        (wf, wft, bf, bft), (const((D, H)), const((H, D)), const((1, H)), const((H, 1))),
        [
            jax.ShapeDtypeStruct((R * H, FOX_HD), F32),
            jax.ShapeDtypeStruct((H, R, 2 * FOX_HD), BF),
            jax.ShapeDtypeStruct((R, H), F32),
            jax.ShapeDtypeStruct((R, H), F32),
            jax.ShapeDtypeStruct((H, R), F32),
        ],
        [
            native,
            pl.BlockSpec((H, tm, 2 * FOX_HD), lambda i: (0, i, 0)),
            pl.BlockSpec((tm, H), lambda i: (i, 0)),
            pl.BlockSpec((tm, H), lambda i: (i, 0)),
            pl.BlockSpec((H, tm), lambda i: (0, i)),
        ],
        [pltpu.VMEM((1, H), F32), pltpu.VMEM((H, 1), F32)],
        tm=tm, vmem=base + 2 * tm * D * 4 + 5 * tm * D * 2 + 3 * tm * tm * 4 + 4 * tm * 128 * 4, name="fox_proj_k")
    v, vt = _proj_call(
        functools.partial(_fox_v_body, tm=tm), x, g, w, 2, (), (),
        [jax.ShapeDtypeStruct((R * H, FOX_HD), F32), jax.ShapeDtypeStruct((H, FOX_HD, R), BF)],
        [native, pl.BlockSpec((H, FOX_HD, tm), lambda i: (0, 0, i))], (),
        tm=tm, vmem=base + 2 * tm * D * 4 + 3 * tm * D * 2 + tm * D * 4, name="fox_proj_v")
    return qh, k, v, ka, vt, lf, cc, cr


def _softmax_step(s_biased, cq, pv, m_ref, l_ref, acc_ref):
    m_old = m_ref[...]
    m_new = jnp.maximum(m_old, jnp.max(s_biased, axis=-1, keepdims=True) + cq)
    p = jnp.exp2(s_biased - (m_new - cq))
    alpha = jnp.exp2(m_old - m_new)
    l_ref[...] = alpha * l_ref[...] + jnp.sum(p, axis=-1, keepdims=True)
    acc_ref[...] = alpha * acc_ref[...] + pv(p.astype(BF))
    m_ref[...] = m_new


def _fox_attn_body(q_ref, ka_ref, vt_ref, cr_ref, o_ref, qt_ref, m_ref, l_ref, acc_ref, sa_ref, sb_ref, *, tq, tk):
    h = pl.program_id(0)
    qi = pl.program_id(1)
    q0 = pl.multiple_of(qi * tq, tq)
    qt_ref[0:FOX_HD, :] = q_ref[0].astype(F32).T.astype(BF)
    sub = lax.broadcasted_iota(jnp.int32, (FOX_HD, tq), 0)
    qt_ref[FOX_HD:2 * FOX_HD, :] = jnp.where(sub < 3, 1.0, 0.0).astype(BF)
    cq = cr_ref[pl.ds(h, 1), pl.ds(q0, tq)] * LOG2E
    m_ref[...] = jnp.full_like(m_ref, NEG)
    l_ref[...] = jnp.zeros_like(l_ref)
    acc_ref[...] = jnp.zeros_like(acc_ref)

    def scores(t, st_ref):
        k0 = pl.multiple_of(t * tk, tk)
        st_ref[...] = jnp.dot(ka_ref[0, pl.ds(k0, tk), :], qt_ref[...], preferred_element_type=F32)

    def fold(t, st_ref, diag=None):
        k0 = pl.multiple_of(t * tk, tk)
        st = st_ref[...]
        if diag is not None:
            kj = lax.broadcasted_iota(jnp.int32, (tk, tq), 0) + diag * tk
            qcol = lax.broadcasted_iota(jnp.int32, (tk, tq), 1)
            st = jnp.where(kj <= qcol, st, NEG)
        m_old = m_ref[...]
        m_new = jnp.maximum(m_old, jnp.max(st, axis=0, keepdims=True) + cq)
        p = jnp.exp2(st - (m_new - cq))
        alpha = jnp.exp2(m_old - m_new)
        l_ref[...] = alpha * l_ref[...] + jnp.sum(p, axis=0, keepdims=True)
        acc_ref[...] = alpha * acc_ref[...] + jnp.dot(
            vt_ref[0, :, pl.ds(k0, tk)], p.astype(BF), preferred_element_type=F32)
        m_ref[...] = m_new

    scores(0, sa_ref)

    def quad(u, _):
        t = 4 * u
        scores(t + 1, sb_ref)
        fold(t, sa_ref)
        scores(t + 2, sa_ref)
        fold(t + 1, sb_ref)
        scores(t + 3, sb_ref)
        fold(t + 2, sa_ref)
        scores(t + 4, sa_ref)
        fold(t + 3, sb_ref)
        return 0

    lax.fori_loop(0, qi // 2, quad, 0)
    td = 2 * qi

    @pl.when(qi % 2 == 1)
    def _():
        scores(td - 1, sb_ref)
        fold(td - 2, sa_ref)
        scores(td, sa_ref)
        fold(td - 1, sb_ref)

    scores(td + 1, sb_ref)
    fold(td, sa_ref, diag=0)
    fold(td + 1, sb_ref, diag=1)
    o_ref[...] = (acc_ref[...] / l_ref[...]).T.astype(BF)


def _fox_attn(qh, ka, vt, cr, *, T, tq=1024):
    H = FOX_HEADS
    tk = tq // 2
    vmem = 2 * T * 2 * FOX_HD * 2 + 2 * T * FOX_HD * 2 + 2 * H * T * 4 + 8 * tk * tq * 4 + 16 * tq * FOX_HD * 4
    return pl.pallas_call(
        functools.partial(_fox_attn_body, tq=tq, tk=tk),
        grid=(H, T // tq),
        in_specs=[
            pl.BlockSpec((1, tq, FOX_HD), lambda h, i: (h, i, 0)),
            pl.BlockSpec((1, T, 2 * FOX_HD), lambda h, i: (h, 0, 0)),
            pl.BlockSpec((1, FOX_HD, T), lambda h, i: (h, 0, 0)),
            pl.BlockSpec((H, T), lambda h, i: (0, 0)),
        ],
        out_specs=pl.BlockSpec((tq, FOX_HD), lambda h, i: (i, h)),
        out_shape=jax.ShapeDtypeStruct((T, D), BF),
        scratch_shapes=[pltpu.VMEM((2 * FOX_HD, tq), BF), pltpu.VMEM((1, tq), F32), pltpu.VMEM((1, tq), F32),
                        pltpu.VMEM((FOX_HD, tq), F32), pltpu.VMEM((tk, tq), F32), pltpu.VMEM((tk, tq), F32)],
        compiler_params=_params(("arbitrary", "arbitrary"), vmem),
        name="fox_attention",
    )(qh, ka, vt, cr)


def _suffix_body(x_ref, o_ref, car_ref, *, tc):
    @pl.when(pl.program_id(1) == 0)
    def _():
        car_ref[...] = jnp.zeros_like(car_ref)

    x = x_ref[0]
    o_ref[0] = _sum_dot_right(x, _tri(tc, "gt")) + car_ref[...]
    car_ref[...] = car_ref[...] + jnp.sum(x, axis=1, keepdims=True)


def _suffix_sum(x, *, tc=512):
    B, H, S = x.shape
    nc = S // tc
    return pl.pallas_call(
        functools.partial(_suffix_body, tc=tc),
        grid=(B, nc),
        in_specs=[pl.BlockSpec((1, H, tc), lambda b, t: (b, 0, nc - 1 - t))],
        out_specs=pl.BlockSpec((1, H, tc), lambda b, t: (b, 0, nc - 1 - t)),
        out_shape=jax.ShapeDtypeStruct((B, H, S), F32),
        scratch_shapes=[pltpu.VMEM((H, 1), F32)],
        compiler_params=_params(("arbitrary", "arbitrary"), 8 * tc * tc * 4),
        name="suffix_sum",
    )(x)


def _fox_decode_body(q_ref, ka_ref, vt_ref, kc_ref, vc_ref, dsuf_ref, ec_ref, er_ref, o_ref,
                     m_ref, l_ref, acc_ref, *, nt, tq, ts):
    b = pl.program_id(0)
    t = pl.program_id(1)
    H = FOX_HEADS

    @pl.when(t == 0)
    def _():
        m_ref[...] = jnp.full_like(m_ref, NEG)
        l_ref[...] = jnp.zeros_like(l_ref)
        acc_ref[...] = jnp.zeros_like(acc_ref)

    ec = ec_ref[...] * LOG2E
    dsuf = dsuf_ref[0] * LOG2E
    for hh in range(H):
        kt = kc_ref[0, pl.ds(hh, ts, stride=H), :].astype(BF)
        vt = vc_ref[0, pl.ds(hh, ts, stride=H), :].astype(BF)
        s = _nt_dot(q_ref[hh], kt) + dsuf[hh:hh + 1, :]
        _softmax_step(s, ec[:, hh:hh + 1], lambda p, vt=vt: jnp.dot(p, vt, preferred_element_type=F32),
                      m_ref.at[hh], l_ref.at[hh], acc_ref.at[hh])

    @pl.when(t == nt - 1)
    def _():
        n_new = er_ref.shape[1]
        er = er_ref[...] * LOG2E
        r = lax.broadcasted_iota(jnp.int32, (tq, n_new), 0)
        c = lax.broadcasted_iota(jnp.int32, (tq, n_new), 1) - b * tq
        valid = jnp.logical_and(c >= 0, c <= r)
        for hh in range(H):
            s = _nt_dot(q_ref[hh], ka_ref[hh][:, 0:FOX_HD]) - er[hh:hh + 1, :]
            s = jnp.where(valid, s, NEG)
            _softmax_step(s, ec[:, hh:hh + 1], lambda p, hh=hh: _nt_dot(p, vt_ref[hh]),
                          m_ref.at[hh], l_ref.at[hh], acc_ref.at[hh])
            o_ref[:, hh * FOX_HD:(hh + 1) * FOX_HD] = (acc_ref[hh] / l_ref[hh]).astype(BF)


def _fox_decode(qh, ka, vt, kc, vc, dsuf, ec, er, *, B, tq, ts=512):
    H = FOX_HEADS
    S = kc.shape[1] // H
    R = B * tq
    nt = S // ts
    vmem = (4 * ts * D * 4 + 2 * H * R * 3 * FOX_HD * 2 + 2 * H * tq * FOX_HD * 2 + 3 * H * tq * 128 * 4
            + 4 * ts * FOX_HD * 2 + 8 * tq * max(ts, R) * 4)
    return pl.pallas_call(
        functools.partial(_fox_decode_body, nt=nt, tq=tq, ts=ts),
        grid=(B, nt),
        in_specs=[
            pl.BlockSpec((H, tq, FOX_HD), lambda b, t: (0, b, 0)),
            pl.BlockSpec((H, R, 2 * FOX_HD), lambda b, t: (0, 0, 0)),
            pl.BlockSpec((H, FOX_HD, R), lambda b, t: (0, 0, 0)),
            pl.BlockSpec((1, ts * H, FOX_HD), lambda b, t: (b, t, 0)),
            pl.BlockSpec((1, ts * H, FOX_HD), lambda b, t: (b, t, 0)),
            pl.BlockSpec((1, H, ts), lambda b, t: (b, 0, t)),
            pl.BlockSpec((tq, H), lambda b, t: (b, 0)),
            pl.BlockSpec((H, R), lambda b, t: (0, 0)),
        ],
        out_specs=pl.BlockSpec((tq, D), lambda b, t: (b, 0)),
        out_shape=jax.ShapeDtypeStruct((R, D), BF),
        scratch_shapes=[pltpu.VMEM((H, tq, 1), F32), pltpu.VMEM((H, tq, 1), F32), pltpu.VMEM((H, tq, FOX_HD), F32)],
        compiler_params=_params(("arbitrary", "arbitrary"), vmem),
        name="fox_decode_attention",
    )(qh, ka, vt, kc, vc, dsuf, ec, er)


def _mlstm_qk_body(x_ref, g_ref, w_ref, wg_ref, wgt_ref, bg_ref, bgt_ref, q_ref, k_ref, gc_ref, gr_ref, *, qscale):
    h, y = _normed_dot(x_ref, g_ref, w_ref)
    zc = jnp.dot(h, wg_ref[...], preferred_element_type=F32) + bg_ref[...]
    lane = lax.broadcasted_iota(jnp.int32, zc.shape, 1)
    gc_ref[...] = jnp.where(lane < ML_HEADS, zc, _log_sigmoid(zc))
    zr = _nt_dot(wgt_ref[...], h) + bgt_ref[...]
    sub = lax.broadcasted_iota(jnp.int32, zr.shape, 0)
    gr_ref[...] = jnp.where(sub < ML_HEADS, zr, _log_sigmoid(zr))
    q_ref[...] = (y[:, :ML_QK] * qscale).astype(BF)
    k_ref[...] = y[:, ML_QK:]


def _proj_bf16_body(x_ref, g_ref, w_ref, o_ref):
    o_ref[...] = _normed_dot(x_ref, g_ref, w_ref)[1].astype(BF)


def _proj_f32_body(x_ref, g_ref, w_ref, o_ref):
    o_ref[...] = _normed_dot(x_ref, g_ref, w_ref)[1]


def _proj_plain(x, g, w, jcol, dtype, *, tm=512, name):
    R = x.shape[0]
    body = _proj_bf16_body if dtype == BF else _proj_f32_body
    vmem = 2 * tm * D * 4 + D * D * 2 + 2 * tm * D * 4 + tm * D * 2 + 2 * tm * D * 4
    return _proj_call(body, x, g, w, jcol, (), (), jax.ShapeDtypeStruct((R, D), dtype),
                      pl.BlockSpec((tm, D), lambda i: (i, 0)), (), tm=tm, vmem=vmem, name=name)


def _mlstm_proj(x, g, w, wg, wgt, bg, bgt, *, tm=512):
    R = x.shape[0]
    G = 2 * ML_HEADS
    const = lambda shape: pl.BlockSpec(shape, lambda i: (0,) * len(shape))
    vmem = 2 * tm * D * 4 + D * D * 2 + 2 * tm * D * 4 + tm * D * 2 + 2 * tm * D * 4
    q, k, gc, gr = _proj_call(
        functools.partial(_mlstm_qk_body, qscale=ML_DK ** -0.5), x, g, w, 0,
        (wg, wgt, bg, bgt), (const((D, G)), const((G, D)), const((1, G)), const((G, 1))),
        [
            jax.ShapeDtypeStruct((R, ML_QK), BF),
            jax.ShapeDtypeStruct((R, ML_QK), F32),
            jax.ShapeDtypeStruct((R, G), F32),
            jax.ShapeDtypeStruct((G, R), F32),
        ],
        [
            pl.BlockSpec((tm, ML_QK), lambda i: (i, 0)),
            pl.BlockSpec((tm, ML_QK), lambda i: (i, 0)),
            pl.BlockSpec((tm, G), lambda i: (i, 0)),
            pl.BlockSpec((G, tm), lambda i: (0, i)),
        ],
        (), tm=tm, vmem=vmem, name="mlstm_proj_qk")
    v = _proj_plain(x, g, w, 1, BF, tm=tm, name="mlstm_proj_v")
    o = _proj_plain(x, g, w, 2, F32, tm=tm, name="mlstm_proj_o")
    return q, k, v, o, gc, gr


def _mlstm_body(q_ref, k_ref, v_ref, o_ref, gc_ref, gr_ref, gn_ref, c0_ref, n0_ref, m0_ref,
                y_ref, co_ref, no_ref, mo_ref, c_ref, n_ref, m_ref, *, L):
    t = pl.program_id(1)

    @pl.when(t == 0)
    def _():
        c_ref[...] = c0_ref[0]
        n_ref[...] = n0_ref[0]
        m_ref[...] = m0_ref[0]

    gc = gc_ref[...]
    gr = gr_ref[0]
    ra = lax.broadcasted_iota(jnp.int32, (L, L), 0)
    cb = lax.broadcasted_iota(jnp.int32, (L, L), 1)
    causal = cb <= ra
    bcs = _sum_dot_left(_tri(L, "le"), gc)
    brs = _sum_dot_right(gr, _tri(L, "ge"))
    for hh in range(ML_HEADS):
        b_c = bcs[:, ML_HEADS + hh:ML_HEADS + hh + 1]
        g_c = gc[:, hh:hh + 1] - b_c
        g_r = gr[hh:hh + 1, :] - brs[ML_HEADS + hh:ML_HEADS + hh + 1, :]
        m_h = m_ref[hh][:, 0:1]
        am = jnp.where(causal, b_c + g_r, NEG)
        mt = jnp.maximum(b_c + m_h, jnp.max(am, axis=-1, keepdims=True))
        d = jnp.exp(am - mt)
        qh = q_ref[:, hh * ML_DK:(hh + 1) * ML_DK]
        kf = k_ref[:, hh * ML_DK:(hh + 1) * ML_DK]
        vh = v_ref[:, hh * ML_DV:(hh + 1) * ML_DV]
        sc = _nt_dot(qh, kf.astype(BF)) * d
        inter = jnp.exp(b_c + m_h - mt)
        c_old = c_ref[hh]
        n_old = n_ref[hh]
        num = (jnp.dot(sc.astype(BF), vh, preferred_element_type=F32)
               + inter * jnp.dot(qh, c_old.astype(BF), preferred_element_type=F32))
        den = (jnp.sum(sc, axis=-1, keepdims=True)
               + inter * jnp.sum(qh.astype(F32) * n_old, axis=-1, keepdims=True))
        hout = num / jnp.maximum(jnp.abs(den), jnp.exp(-mt))
        m_new = mt[L - 1:L, :]
        b_last = b_c[L - 1:L, :]
        decay = jnp.exp(b_last + m_h - m_new)
        wk = jnp.exp(b_last + g_c - m_new) * kf
        c_ref[hh] = decay * c_old + lax.dot_general(
            wk.astype(BF), vh, (((0,), (0,)), ((), ())), preferred_element_type=F32)
        n_ref[hh] = decay * n_old + jnp.sum(wk, axis=0, keepdims=True)
        m_ref[hh] = jnp.broadcast_to(m_new, (1, 128))
        hn = hout * lax.rsqrt(jnp.mean(hout * hout, axis=-1, keepdims=True) + EPS)
        cols = slice(hh * ML_DV, (hh + 1) * ML_DV)
        y_ref[:, cols] = (hn * gn_ref[:, cols] * jax.nn.sigmoid(o_ref[:, cols])).astype(BF)

    co_ref[0] = c_ref[...]
    no_ref[0] = n_ref[...]
    mo_ref[0] = m_ref[...]


def _mlstm_core(q, k, v, o, gc, gr, gn, c0, n0, m0, *, B, T, L):
    nt = T // L
    G = 2 * ML_HEADS
    H = ML_HEADS
    vmem = (2 * L * (ML_QK * 6 + D * 8) + 6 * H * ML_DK * ML_DV * 4 + 16 * L * L * 4 + 12 * L * ML_DV * 4
            + 4 * 1024 * 1024)
    rows = lambda w: pl.BlockSpec((L, w), lambda b, t: (b * nt + t, 0))
    st_c = pl.BlockSpec((1, H, ML_DK, ML_DV), lambda b, t: (b, 0, 0, 0))
    st_n = pl.BlockSpec((1, H, 1, ML_DK), lambda b, t: (b, 0, 0, 0))
    st_m = pl.BlockSpec((1, H, 1, 128), lambda b, t: (b, 0, 0, 0))
    return pl.pallas_call(
        functools.partial(_mlstm_body, L=L),
        grid=(B, nt),
        in_specs=[
            rows(ML_QK), rows(ML_QK), rows(D), rows(D), rows(G),
            pl.BlockSpec((1, G, L), lambda b, t: (b, 0, t)),
            pl.BlockSpec((1, D), lambda b, t: (0, 0)),
            st_c, st_n, st_m,
        ],
        out_specs=[rows(D), st_c, st_n, st_m],
        out_shape=[
            jax.ShapeDtypeStruct((B * T, D), BF),
            jax.ShapeDtypeStruct((B, H, ML_DK, ML_DV), F32),
            jax.ShapeDtypeStruct((B, H, 1, ML_DK), F32),
            jax.ShapeDtypeStruct((B, H, 1, 128), F32),
        ],
        scratch_shapes=[
            pltpu.VMEM((H, ML_DK, ML_DV), F32),
            pltpu.VMEM((H, 1, ML_DK), F32),
            pltpu.VMEM((H, 1, 128), F32),
        ],
        compiler_params=_params(("arbitrary", "arbitrary"), vmem),
        name="mlstm_core",
    )(q, k, v, o, gc, gr, gn, c0, n0, m0)


def kernel(x_prompt, x_sample, state_pool, state_lru_conv, state_lru_h, cache_fox_k, cache_fox_v, cache_fox_logf, state_mlstm_c, state_mlstm_n, state_mlstm_m, ffn1_norm, ffn1_w_gate, ffn1_w_up, ffn1_w_down, mix_norm, ffn2_norm, ffn2_w_gate, ffn2_w_up, ffn2_w_down, pool_w, pool_scale, lru_w_in, lru_conv_w, lru_conv_b, lru_w_a, lru_b_a, lru_w_i, lru_b_i, lru_lambda, lru_w_out, fox_w_qkv, fox_w_f, fox_b_f, fox_w_o, mlstm_w_in, mlstm_b_i, mlstm_b_f, mlstm_norm, mlstm_w_out, final_norm):
    BP, TP, _ = x_prompt.shape
    BS, TS, _ = x_sample.shape
    assert BP == 1 and x_prompt.shape[2] == D and x_sample.shape[2] == D
    RS = BS * TS
    past = cache_fox_k.shape[1]
    row = lambda p: p.reshape(1, -1).astype(F32)

    xp = x_prompt.reshape(TP, D)
    xs = x_sample.reshape(RS, D)

    ffn_w = {
        1: (ffn1_norm, ffn1_w_gate.astype(BF), ffn1_w_up.astype(BF), (0.5 * ffn1_w_down).astype(BF)),
        2: (ffn2_norm, ffn2_w_gate.astype(BF), ffn2_w_up.astype(BF), (0.5 * ffn2_w_down).astype(BF)),
    }

    def ffn(xp, xs, which, layer, last=False):
        norm, wg, wu, wd = ffn_w[which]
        g, gf = row(norm[layer]), row(final_norm)
        return (_ffn(xp, g, wg, wu, wd, gf, layer=layer, normalize_out=last),
                _ffn(xs, g, wg, wu, wd, gf, layer=layer, normalize_out=last))

    xp, xs = ffn(xp, xs, 1, 0)
    pw = pool_w.astype(BF)
    hist_s = jnp.pad(state_pool.astype(F32), ((0, 0), (POOL_PAD - state_pool.shape[1], 0), (0, 0)))
    xp, pool_p = _pool_mixer(xp, row(mix_norm[0]), jnp.zeros((BP, POOL_PAD, D), F32), pw, row(pool_scale),
                             B=BP, T=TP, tm=512, pos0=0)
    xs, pool_s = _pool_mixer(xs, row(mix_norm[0]), hist_s, pw, row(pool_scale), B=BS, T=TS, tm=TS, pos0=past)
    pool_p = pool_p[:, 1:]
    pool_s = pool_s[:, 1:]
    xp, xs = ffn(xp, xs, 2, 0)

    xp, xs = ffn(xp, xs, 1, 1)
    w_in = lru_w_in.astype(BF)
    lru_args = (lru_conv_w.astype(F32), row(lru_conv_b), lru_w_a.astype(BF), row(lru_b_a),
                lru_w_i.astype(BF), row(lru_b_i), row(lru_lambda))
    lru_in = lambda x, jcol: _proj_plain(x, row(mix_norm[1]), w_in, jcol, F32, name="rglru_proj")
    cst_s = jnp.pad(state_lru_conv.astype(F32), ((0, 0), (CONV_PAD - (CONV_W - 1), 0), (0, 0)))
    yp, conv_p, h_p = _lru_core(lru_in(xp, 0), lru_in(xp, 1), jnp.zeros((BP, CONV_PAD, D), F32),
                                jnp.zeros((BP, 1, D), F32), *lru_args, B=BP, T=TP, tm=256)
    ys, conv_s, h_s = _lru_core(lru_in(xs, 0), lru_in(xs, 1), cst_s, state_lru_h.astype(F32).reshape(BS, 1, D),
                                *lru_args, B=BS, T=TS, tm=TS)
    w_out = lru_w_out.astype(BF)
    xp = _mm_res(yp, w_out, xp)
    xs = _mm_res(ys, w_out, xs)
    lru_conv_p, lru_conv_s = conv_p[:, CONV_PAD - (CONV_W - 1):], conv_s[:, CONV_PAD - (CONV_W - 1):]
    lru_h_p, lru_h_s = h_p.reshape(BP, D), h_s.reshape(BS, D)
    xp, xs = ffn(xp, xs, 2, 1)

    xp, xs = ffn(xp, xs, 1, 2)
    w_qkv = fox_w_qkv.astype(BF)
    wf = fox_w_f.astype(BF)
    wft = wf.T
    bf_r = fox_b_f.astype(F32).reshape(1, FOX_HEADS)
    bf_c = fox_b_f.astype(F32).reshape(FOX_HEADS, 1)
    qh_p, k_p, v_p, ka_p, vt_p, lf_p, cc_p, cr_p = _fox_proj(
        xp, row(mix_norm[2]), w_qkv, wf, wft, bf_r, bf_c, tm=512, seg=512, carry=True)
    qh_s, k_s, v_s, ka_s, vt_s, lf_s, cc_s, cr_s = _fox_proj(
        xs, row(mix_norm[2]), w_qkv, wf, wft, bf_r, bf_c, tm=RS, seg=TS, carry=False)
    o_p = _fox_attn(qh_p, ka_p, vt_p, cr_p, T=TP)
    dsuf = _suffix_sum(jnp.swapaxes(cache_fox_logf.astype(F32), 1, 2))
    o_s = _fox_decode(qh_s, ka_s, vt_s,
                      cache_fox_k.reshape(BS, past * FOX_HEADS, FOX_HD),
                      cache_fox_v.reshape(BS, past * FOX_HEADS, FOX_HD),
                      dsuf, cc_s, cr_s, B=BS, tq=TS)
    w_o = fox_w_o.astype(BF)
    xp = _mm_res(o_p, w_o, xp)
    xs = _mm_res(o_s, w_o, xs)
    fox_k_p = k_p.reshape(BP, TP, FOX_HEADS, FOX_HD)
    fox_v_p = v_p.reshape(BP, TP, FOX_HEADS, FOX_HD)
    fox_k_s = k_s.reshape(BS, TS, FOX_HEADS, FOX_HD)
    fox_v_s = v_s.reshape(BS, TS, FOX_HEADS, FOX_HD)
    fox_lf_p = lf_p.reshape(BP, TP, FOX_HEADS)
    fox_lf_s = lf_s.reshape(BS, TS, FOX_HEADS)
    xp, xs = ffn(xp, xs, 2, 2)

    xp, xs = ffn(xp, xs, 1, 3)
    n_main = 2 * ML_QK + 2 * D
    w_main = mlstm_w_in[:, :n_main].astype(BF)
    w_gates = mlstm_w_in[:, n_main:].astype(BF)
    b_gates = jnp.concatenate([mlstm_b_i, mlstm_b_f]).astype(F32)
    ml_w = (row(mix_norm[3]), w_main, w_gates, w_gates.T, b_gates.reshape(1, -1), b_gates.reshape(-1, 1))
    q_p, kk_p, vv_p, og_p, gc_p, gr_p = _mlstm_proj(xp, *ml_w)
    q_s, kk_s, vv_s, og_s, gc_s, gr_s = _mlstm_proj(xs, *ml_w)
    G = 2 * ML_HEADS
    gn = row(mlstm_norm)
    yp, ml_c_p, ml_n_p, ml_m_p = _mlstm_core(
        q_p, kk_p, vv_p, og_p, gc_p, gr_p.reshape(1, G, TP), gn,
        jnp.zeros((BP, ML_HEADS, ML_DK, ML_DV), F32), jnp.zeros((BP, ML_HEADS, 1, ML_DK), F32),
        jnp.zeros((BP, ML_HEADS, 1, 128), F32), B=BP, T=TP, L=256)
    m0_s = jnp.broadcast_to(state_mlstm_m.astype(F32)[:, :, None, None], (BS, ML_HEADS, 1, 128))
    ys, ml_c_s, ml_n_s, ml_m_s = _mlstm_core(
        q_s, kk_s, vv_s, og_s, gc_s, jnp.swapaxes(gr_s.reshape(G, BS, TS), 0, 1), gn,
        state_mlstm_c.astype(F32), state_mlstm_n.astype(F32).reshape(BS, ML_HEADS, 1, ML_DK), m0_s,
        B=BS, T=TS, L=TS)
    w_out = mlstm_w_out.astype(BF)
    xp = _mm_res(yp, w_out, xp)
    xs = _mm_res(ys, w_out, xs)
    ml_n_p, ml_n_s = ml_n_p.reshape(BP, ML_HEADS, ML_DK), ml_n_s.reshape(BS, ML_HEADS, ML_DK)
    ml_m_p, ml_m_s = ml_m_p[:, :, 0, 0], ml_m_s[:, :, 0, 0]
    xp, xs = ffn(xp, xs, 2, 3, last=True)

    y_prompt = xp.reshape(BP, TP, D)
    y_sample = xs.reshape(BS, TS, D)
    return (y_prompt, y_sample, pool_p, pool_s, lru_conv_p, lru_conv_s, lru_h_p, lru_h_s,
            fox_k_p, fox_k_s, fox_v_p, fox_v_s, fox_lf_p, fox_lf_s,
            ml_c_p, ml_c_s, ml_n_p, ml_n_s, ml_m_p, ml_m_s)
```

```python
import functools
import math

import jax
import jax.numpy as jnp
from jax import lax
from jax.experimental import pallas as pl
from jax.experimental.pallas import tpu as pltpu

F32 = jnp.float32
BF = jnp.bfloat16

D = 2048
D_FF = 5632
EPS = 1e-6
NEG = -1e30
LOG2E = math.log2(math.e)

POOL_WINDOWS = (2, 4, 8, 16)
POOL_GW = D // len(POOL_WINDOWS)
POOL_PAD = 16
CONV_W = 4
CONV_PAD = 8
LRU_BLOCKS = 8
LRU_BW = D // LRU_BLOCKS
LRU_C = 8.0
FOX_HEADS = 16
FOX_HD = D // FOX_HEADS
ML_HEADS = 8
ML_DV = D // ML_HEADS
ML_DK = ML_DV // 2
ML_QK = ML_HEADS * ML_DK

V7X_VMEM_LIMIT = 60000 * 1024


def _params(semantics, vmem_bytes):
    return pltpu.CompilerParams(dimension_semantics=semantics,
                                vmem_limit_bytes=min(int(vmem_bytes), V7X_VMEM_LIMIT))


def _rms(xf, g):
    ms = jnp.mean(xf * xf, axis=-1, keepdims=True)
    return xf * lax.rsqrt(ms + EPS) * g


def _nt_dot(a, b):
    return lax.dot_general(a, b, (((1,), (1,)), ((), ())), preferred_element_type=F32)


def _log_sigmoid(z):
    return jnp.minimum(z, 0.0) - jnp.log1p(jnp.exp(-jnp.abs(z)))


def _split3_bf16(c):
    hi = c.astype(BF)
    r1 = c - hi.astype(F32)
    mid = r1.astype(BF)
    lo = (r1 - mid.astype(F32)).astype(BF)
    return hi, mid, lo


def _sum_dot_left(tri, x):
    return sum(jnp.dot(tri, part, preferred_element_type=F32) for part in _split3_bf16(x))


def _sum_dot_right(x, tri):
    return sum(jnp.dot(part, tri, preferred_element_type=F32) for part in _split3_bf16(x))


def _tri(n, kind, seg=None):
    a = lax.broadcasted_iota(jnp.int32, (n, n), 0)
    b = lax.broadcasted_iota(jnp.int32, (n, n), 1)
    if kind == "le":
        m = b <= a
        if seg is not None:
            m = jnp.logical_and(m, b >= jnp.bitwise_and(a, -seg))
    elif kind == "ge":
        m = b >= a
        if seg is not None:
            m = jnp.logical_and(m, a >= jnp.bitwise_and(b, -seg))
    else:
        m = a > b
    return jnp.where(m, 1.0, 0.0).astype(BF)


def _ffn_chunk(x_ref, g_ref, wg, wu, wd_half, go_ref, o_ref, h_ref, *, nj, normalize_out):
    j = pl.program_id(1)

    @pl.when(j == 0)
    def _():
        xf = x_ref[...]
        h_ref[...] = _rms(xf, g_ref[...]).astype(BF)
        o_ref[...] = xf

    h = h_ref[...]
    gt = jnp.dot(h, wg, preferred_element_type=F32)
    up = jnp.dot(h, wu, preferred_element_type=F32)
    a = (gt * jax.nn.sigmoid(gt) * up).astype(BF)
    o_ref[...] += jnp.dot(a, wd_half, preferred_element_type=F32)

    if normalize_out:
        @pl.when(j == nj - 1)
        def _():
            o_ref[...] = _rms(o_ref[...], go_ref[...])


def _ffn_body(x_ref, g_ref, wg_ref, wu_ref, wd_ref, go_ref, o_ref, h_ref, **kw):
    _ffn_chunk(x_ref, g_ref, wg_ref[...], wu_ref[...], wd_ref[...], go_ref, o_ref, h_ref, **kw)


def _ffn_cast_body(x_ref, g_ref, wg_ref, wu_ref, wd_ref, go_ref, o_ref, wgb_ref, wub_ref, wdb_ref, h_ref, **kw):
    wgb_ref[...] = wg_ref[...].astype(BF)
    wub_ref[...] = wu_ref[...].astype(BF)
    wdb_ref[...] = (0.5 * wd_ref[...]).astype(BF)
    _ffn_chunk(x_ref, g_ref, wgb_ref[...], wub_ref[...], wdb_ref[...], go_ref, o_ref, h_ref, **kw)


def _ffn_cast(x, g, wg, wu, wd, out_gain, *, layer, normalize_out, tf=256):
    tm = x.shape[0]
    nj = D_FF // tf
    vmem = 4 * tm * D * 4 + tm * D * 2 + 6 * D * tf * 4 + 9 * D * tf * 2 + 5 * tm * tf * 4
    wcol = lambda dt: (pl.BlockSpec((None, D, tf), lambda i, j: (layer, 0, j)) if dt == F32
                       else pl.BlockSpec((None, D, tf), lambda i, j: (0, 0, j)))
    return pl.pallas_call(
        functools.partial(_ffn_cast_body, nj=nj, normalize_out=normalize_out),
        grid=(1, nj),
        in_specs=[
            pl.BlockSpec((tm, D), lambda i, j: (0, 0)),
            pl.BlockSpec((1, D), lambda i, j: (0, 0)),
            wcol(F32), wcol(F32),
            pl.BlockSpec((None, tf, D), lambda i, j: (layer, j, 0)),
            pl.BlockSpec((1, D), lambda i, j: (0, 0)),
        ],
        out_specs=[
            pl.BlockSpec((tm, D), lambda i, j: (0, 0)),
            wcol(BF), wcol(BF),
            pl.BlockSpec((None, tf, D), lambda i, j: (0, j, 0)),
        ],
        out_shape=[
            jax.ShapeDtypeStruct((tm, D), F32),
            jax.ShapeDtypeStruct((1, D, D_FF), BF),
            jax.ShapeDtypeStruct((1, D, D_FF), BF),
            jax.ShapeDtypeStruct((1, D_FF, D), BF),
        ],
        scratch_shapes=[pltpu.VMEM((tm, D), BF)],
        compiler_params=_params(("arbitrary", "arbitrary"), vmem),
        name="ffn_cast",
    )(x, g, wg, wu, wd, out_gain)


def _ffn(x, g, wg, wu, wd_half, out_gain, *, layer, normalize_out, tm=1024, tf=512):
    R = x.shape[0]
    tm = min(tm, R)
    nj = D_FF // tf
    vmem = 4 * tm * D * 4 + tm * D * 2 + 6 * D * tf * 2 + 5 * tm * tf * 4
    return pl.pallas_call(
        functools.partial(_ffn_body, nj=nj, normalize_out=normalize_out),
        grid=(R // tm, nj),
        in_specs=[
            pl.BlockSpec((tm, D), lambda i, j: (i, 0)),
            pl.BlockSpec((1, D), lambda i, j: (0, 0)),
            pl.BlockSpec((None, D, tf), lambda i, j: (layer, 0, j)),
            pl.BlockSpec((None, D, tf), lambda i, j: (layer, 0, j)),
            pl.BlockSpec((None, tf, D), lambda i, j: (layer, j, 0)),
            pl.BlockSpec((1, D), lambda i, j: (0, 0)),
        ],
        out_specs=pl.BlockSpec((tm, D), lambda i, j: (i, 0)),
        out_shape=jax.ShapeDtypeStruct((R, D), F32),
        scratch_shapes=[pltpu.VMEM((tm, D), BF)],
        compiler_params=_params(("arbitrary", "arbitrary"), vmem),
        name="ffn",
    )(x, g, wg, wu, wd_half, out_gain)


def _mm_res_body(a_ref, w_ref, x_ref, o_ref):
    o_ref[...] = x_ref[...] + jnp.dot(a_ref[...], w_ref[...], preferred_element_type=F32)


def _mm_res(a, w, x, *, tm=512):
    R = x.shape[0]
    vmem = 2 * tm * D * 2 + 2 * D * D * 2 + 5 * tm * D * 4
    return pl.pallas_call(
        _mm_res_body,
        grid=(R // tm,),
        in_specs=[
            pl.BlockSpec((tm, D), lambda i: (i, 0)),
            pl.BlockSpec((D, D), lambda i: (0, 0)),
            pl.BlockSpec((tm, D), lambda i: (i, 0)),
        ],
        out_specs=pl.BlockSpec((tm, D), lambda i: (i, 0)),
        out_shape=jax.ShapeDtypeStruct((R, D), F32),
        compiler_params=_params(("arbitrary",), vmem),
        name="mm_res",
    )(a, w, x)


def _pool_body(x_ref, g_ref, hist_ref, w_ref, sc_ref, o_ref, st_ref, xe_ref, *, tm, pos0):
    t = pl.program_id(1)

    @pl.when(t == 0)
    def _():
        xe_ref[0:POOL_PAD, :] = hist_ref[0]

    xf = x_ref[...]
    h = _rms(xf, g_ref[...])
    xe_ref[POOL_PAD:POOL_PAD + tm, :] = h
    row = lax.broadcasted_iota(jnp.int32, (tm, POOL_GW), 0)
    pos1 = (row + (pos0 + 1) + t * tm).astype(F32)
    for gi, w in enumerate(POOL_WINDOWS):
        c0 = gi * POOL_GW
        hg = h[:, c0:c0 + POOL_GW]
        s = hg
        for back in range(1, w):
            s = s + xe_ref[POOL_PAD - back:POOL_PAD - back + tm, c0:c0 + POOL_GW]
        cnt = jnp.minimum(pos1, float(w))
        pooled = s / cnt - hg
        y = jnp.dot(pooled.astype(BF), w_ref[gi], preferred_element_type=F32)
        o_ref[:, c0:c0 + POOL_GW] = xf[:, c0:c0 + POOL_GW] + y * sc_ref[:, c0:c0 + POOL_GW]
    tail = xe_ref[tm:tm + POOL_PAD, :]
    st_ref[0] = tail
    xe_ref[0:POOL_PAD, :] = tail


def _pool_mixer(x, g, hist, w, scale, *, B, T, tm, pos0):
    nt = T // tm
    vmem = 4 * tm * D * 4 + (tm + POOL_PAD) * D * 4 + 2 * 4 * POOL_GW * POOL_GW * 2 + 6 * tm * D * 4
    return pl.pallas_call(
        functools.partial(_pool_body, tm=tm, pos0=pos0),
        grid=(B, nt),
        in_specs=[
            pl.BlockSpec((tm, D), lambda b, t: (b * nt + t, 0)),
            pl.BlockSpec((1, D), lambda b, t: (0, 0)),
            pl.BlockSpec((1, POOL_PAD, D), lambda b, t: (b, 0, 0)),
            pl.BlockSpec((len(POOL_WINDOWS), POOL_GW, POOL_GW), lambda b, t: (0, 0, 0)),
            pl.BlockSpec((1, D), lambda b, t: (0, 0)),
        ],
        out_specs=[
            pl.BlockSpec((tm, D), lambda b, t: (b * nt + t, 0)),
            pl.BlockSpec((1, POOL_PAD, D), lambda b, t: (b, 0, 0)),
        ],
        out_shape=[jax.ShapeDtypeStruct((B * T, D), F32), jax.ShapeDtypeStruct((B, POOL_PAD, D), F32)],
        scratch_shapes=[pltpu.VMEM((tm + POOL_PAD, D), F32)],
        compiler_params=_params(("arbitrary", "arbitrary"), vmem),
        name="pool_mixer",
    )(x, g, hist, w, scale)


def _lru_body(gate_ref, xr_ref, cst_ref, h0_ref, cw_ref, cb_ref, wa_ref, ba_ref, wi_ref, bi_ref, lam_ref,
              y_ref, cso_ref, ho_ref, xe_ref, a_ref, u_ref, hc_ref, *, tm):
    t = pl.program_id(1)

    @pl.when(t == 0)
    def _():
        xe_ref[0:CONV_PAD, :] = cst_ref[0]
        hc_ref[...] = h0_ref[0]

    xe_ref[CONV_PAD:CONV_PAD + tm, :] = xr_ref[...]
    base = CONV_PAD - (CONV_W - 1)
    xc = cb_ref[...] + xe_ref[base:base + tm, :] * cw_ref[0:1, :]
    for j in range(1, CONV_W):
        xc = xc + xe_ref[base + j:base + j + tm, :] * cw_ref[j:j + 1, :]
    tail = xe_ref[tm:tm + CONV_PAD, :]
    cso_ref[0] = tail
    xe_ref[0:CONV_PAD, :] = tail

    xcb = xc.astype(BF)
    nlam = -lam_ref[...]
    sp = jnp.maximum(nlam, 0.0) + jnp.log1p(jnp.exp(-jnp.abs(nlam)))
    for n in range(LRU_BLOCKS):
        blk = slice(n * LRU_BW, (n + 1) * LRU_BW)
        ra = jnp.dot(xcb[:, blk], wa_ref[n], preferred_element_type=F32) + ba_ref[:, blk]
        ia = jnp.dot(xcb[:, blk], wi_ref[n], preferred_element_type=F32) + bi_ref[:, blk]
        r = jax.nn.sigmoid(ra)
        ig = jax.nn.sigmoid(ia)
        log_a = -LRU_C * r * sp[:, blk]
        a_ref[:, blk] = jnp.exp(log_a)
        th = jnp.tanh(log_a)
        one_minus_a2 = -2.0 * th / (1.0 - th)
        u_ref[:, blk] = jnp.sqrt(one_minus_a2) * ig * xc[:, blk]

    rowi = lax.broadcasted_iota(jnp.int32, (8, D), 0)

    def group(gi, carry):
        r0 = pl.multiple_of(gi * 8, 8)
        a8 = a_ref[pl.ds(r0, 8), :]
        u8 = u_ref[pl.ds(r0, 8), :]
        for d in (1, 2, 4):
            keep = rowi >= d
            u8 = jnp.where(keep, a8 * pltpu.roll(u8, d, axis=0) + u8, u8)
            a8 = jnp.where(keep, a8 * pltpu.roll(a8, d, axis=0), a8)
        hs8 = a8 * carry + u8
        u_ref[pl.ds(r0, 8), :] = hs8
        return hs8[7:8, :]

    carry = lax.fori_loop(0, tm // 8, group, hc_ref[...])
    hc_ref[...] = carry
    ho_ref[0] = carry
    y_ref[...] = (u_ref[...] * jax.nn.gelu(gate_ref[...])).astype(BF)


def _lru_core(gate, xr, cst, h0, cw, cb, wa, ba, wi, bi, lam, *, B, T, tm):
    nt = T // tm
    vmem = 4 * tm * D * 4 + 2 * tm * D * 2 + (3 * tm + CONV_PAD) * D * 4 + 4 * 8 * LRU_BW * LRU_BW * 2 + 6 * tm * D * 4
    vec = pl.BlockSpec((1, D), lambda b, t: (0, 0))
    wblk = pl.BlockSpec((LRU_BLOCKS, LRU_BW, LRU_BW), lambda b, t: (0, 0, 0))
    return pl.pallas_call(
        functools.partial(_lru_body, tm=tm),
        grid=(B, nt),
        in_specs=[
            pl.BlockSpec((tm, D), lambda b, t: (b * nt + t, 0)),
            pl.BlockSpec((tm, D), lambda b, t: (b * nt + t, 0)),
            pl.BlockSpec((1, CONV_PAD, D), lambda b, t: (b, 0, 0)),
            pl.BlockSpec((1, 1, D), lambda b, t: (b, 0, 0)),
            pl.BlockSpec((CONV_W, D), lambda b, t: (0, 0)),
            vec, wblk, vec, wblk, vec, vec,
        ],
        out_specs=[
            pl.BlockSpec((tm, D), lambda b, t: (b * nt + t, 0)),
            pl.BlockSpec((1, CONV_PAD, D), lambda b, t: (b, 0, 0)),
            pl.BlockSpec((1, 1, D), lambda b, t: (b, 0, 0)),
        ],
        out_shape=[
            jax.ShapeDtypeStruct((B * T, D), BF),
            jax.ShapeDtypeStruct((B, CONV_PAD, D), F32),
            jax.ShapeDtypeStruct((B, 1, D), F32),
        ],
        scratch_shapes=[
            pltpu.VMEM((tm + CONV_PAD, D), F32),
            pltpu.VMEM((tm, D), F32),
            pltpu.VMEM((tm, D), F32),
            pltpu.VMEM((1, D), F32),
        ],
        compiler_params=_params(("arbitrary", "arbitrary"), vmem),
        name="rglru_core",
    )(gate, xr, cst, h0, cw, cb, wa, ba, wi, bi, lam)


def _proj_call(body, x, g, w, jcol, extra, extra_specs, out_shape, out_specs, scratch, *, tm, vmem, name):
    R = x.shape[0]
    return pl.pallas_call(
        body,
        grid=(R // tm,),
        in_specs=[
            pl.BlockSpec((tm, D), lambda i: (i, 0)),
            pl.BlockSpec((1, D), lambda i: (0, 0)),
            pl.BlockSpec((D, D), lambda i: (0, jcol), pipeline_mode=pl.Buffered(1)),
            *extra_specs,
        ],
        out_specs=out_specs,
        out_shape=out_shape,
        scratch_shapes=scratch,
        compiler_params=_params(("arbitrary",), vmem),
        name=name,
    )(x, g, w, *extra)


def _normed_dot(x_ref, g_ref, w_ref):
    h = _rms(x_ref[...], g_ref[...]).astype(BF)
    return h, jnp.dot(h, w_ref[...], preferred_element_type=F32)


def _store_heads_interleaved(ref, y, tm):
    for hh in range(FOX_HEADS):
        ref[pl.ds(hh, tm, stride=FOX_HEADS), :] = y[:, hh * FOX_HD:(hh + 1) * FOX_HD]


def _fox_q_body(x_ref, g_ref, w_ref, q_ref, *, qscale):
    _, y = _normed_dot(x_ref, g_ref, w_ref)
    yq = (y * qscale).astype(BF)
    for hh in range(FOX_HEADS):
        q_ref[hh] = yq[:, hh * FOX_HD:(hh + 1) * FOX_HD]


def _fox_k_body(x_ref, g_ref, w_ref, wf_ref, wft_ref, bf_ref, bft_ref,
                k_ref, ka_ref, lf_ref, cc_ref, cr_ref, carc_ref, carr_ref, *, tm, seg, carry):
    h, y = _normed_dot(x_ref, g_ref, w_ref)
    lf = _log_sigmoid(jnp.dot(h, wf_ref[...], preferred_element_type=F32) + bf_ref[...])
    lft = _log_sigmoid(_nt_dot(wft_ref[...], h) + bft_ref[...])
    lf_ref[...] = lf
    cc = _sum_dot_left(_tri(tm, "le", seg), lf)
    cr = _sum_dot_right(lft, _tri(tm, "ge", seg))
    if carry:
        @pl.when(pl.program_id(0) == 0)
        def _():
            carc_ref[...] = jnp.zeros_like(carc_ref)
            carr_ref[...] = jnp.zeros_like(carr_ref)

        cc = cc + carc_ref[...]
        cr = cr + carr_ref[...]
        carc_ref[...] = cc[tm - 1:tm, :]
        carr_ref[...] = cr[:, tm - 1:tm]
    cc_ref[...] = cc
    cr_ref[...] = cr

    _store_heads_interleaved(k_ref, y, tm)
    yb = y.astype(BF)
    lane = lax.broadcasted_iota(jnp.int32, (tm, FOX_HD), 1)
    nck = cc * (-LOG2E)
    for hh in range(FOX_HEADS):
        hi, mid, lo = _split3_bf16(nck[:, hh:hh + 1])
        aug = jnp.where(lane == 0, hi.astype(F32),
                        jnp.where(lane == 1, mid.astype(F32), jnp.where(lane == 2, lo.astype(F32), 0.0)))
        ka_ref[hh, :, 0:FOX_HD] = yb[:, hh * FOX_HD:(hh + 1) * FOX_HD]
        ka_ref[hh, :, FOX_HD:2 * FOX_HD] = aug.astype(BF)


def _fox_v_body(x_ref, g_ref, w_ref, v_ref, vt_ref, *, tm):
    _, y = _normed_dot(x_ref, g_ref, w_ref)
    _store_heads_interleaved(v_ref, y, tm)
    for hh in range(FOX_HEADS):
        vt_ref[hh] = y[:, hh * FOX_HD:(hh + 1) * FOX_HD].T.astype(BF)


def _fox_proj(x, g, w, wf, wft, bf, bft, *, tm, seg, carry):
    R = x.shape[0]
    H = FOX_HEADS
    qscale = (FOX_HD ** -0.5) * LOG2E
    base = 2 * tm * D * 4 + D * D * 2 + 2 * tm * D * 4 + tm * D * 2
    const = lambda shape: pl.BlockSpec(shape, lambda i: (0,) * len(shape))
    native = pl.BlockSpec((tm * H, FOX_HD), lambda i: (i, 0))
    qh = _proj_call(
        functools.partial(_fox_q_body, qscale=qscale), x, g, w, 0, (), (),
        jax.ShapeDtypeStruct((H, R, FOX_HD), BF), pl.BlockSpec((H, tm, FOX_HD), lambda i: (0, i, 0)), (),
        tm=tm, vmem=base + 3 * tm * D * 2, name="fox_proj_q")
    k, ka, lf, cc, cr = _proj_call(
        functools.partial(_fox_k_body, tm=tm, seg=seg, carry=carry), x, g, w, 1,
        (wf, wft, bf, bft), (const((D, H)), const((H, D)), const((1, H)), const((H, 1))),
        [
            jax.ShapeDtypeStruct((R * H, FOX_HD), F32),
            jax.ShapeDtypeStruct((H, R, 2 * FOX_HD), BF),
            jax.ShapeDtypeStruct((R, H), F32),
            jax.ShapeDtypeStruct((R, H), F32),
            jax.ShapeDtypeStruct((H, R), F32),
        ],
        [
            native,
            pl.BlockSpec((H, tm, 2 * FOX_HD), lambda i: (0, i, 0)),
            pl.BlockSpec((tm, H), lambda i: (i, 0)),
            pl.BlockSpec((tm, H), lambda i: (i, 0)),
            pl.BlockSpec((H, tm), lambda i: (0, i)),
        ],
        [pltpu.VMEM((1, H), F32), pltpu.VMEM((H, 1), F32)],
        tm=tm, vmem=base + 2 * tm * D * 4 + 5 * tm * D * 2 + 3 * tm * tm * 4 + 4 * tm * 128 * 4, name="fox_proj_k")
    v, vt = _proj_call(
        functools.partial(_fox_v_body, tm=tm), x, g, w, 2, (), (),
        [jax.ShapeDtypeStruct((R * H, FOX_HD), F32), jax.ShapeDtypeStruct((H, FOX_HD, R), BF)],
        [native, pl.BlockSpec((H, FOX_HD, tm), lambda i: (0, 0, i))], (),
        tm=tm, vmem=base + 2 * tm * D * 4 + 3 * tm * D * 2 + tm * D * 4, name="fox_proj_v")
    return qh, k, v, ka, vt, lf, cc, cr


def _softmax_step(s_biased, cq, pv, m_ref, l_ref, acc_ref):
    m_old = m_ref[...]
    m_new = jnp.maximum(m_old, jnp.max(s_biased, axis=-1, keepdims=True) + cq)
    p = jnp.exp2(s_biased - (m_new - cq))
    alpha = jnp.exp2(m_old - m_new)
    l_ref[...] = alpha * l_ref[...] + jnp.sum(p, axis=-1, keepdims=True)
    acc_ref[...] = alpha * acc_ref[...] + pv(p.astype(BF))
    m_ref[...] = m_new


def _fox_attn_body(q_ref, ka_ref, vt_ref, cr_ref, o_ref, qt_ref, m_ref, l_ref, acc_ref, sa_ref, sb_ref, *, tq, tk):
    h = pl.program_id(0)
    qi = pl.program_id(1)
    q0 = pl.multiple_of(qi * tq, tq)
    qt_ref[0:FOX_HD, :] = q_ref[0].astype(F32).T.astype(BF)
    sub = lax.broadcasted_iota(jnp.int32, (FOX_HD, tq), 0)
    qt_ref[FOX_HD:2 * FOX_HD, :] = jnp.where(sub < 3, 1.0, 0.0).astype(BF)
    cq = cr_ref[pl.ds(h, 1), pl.ds(q0, tq)] * LOG2E
    m_ref[...] = jnp.full_like(m_ref, NEG)
    l_ref[...] = jnp.zeros_like(l_ref)
    acc_ref[...] = jnp.zeros_like(acc_ref)

    def scores(t, st_ref):
        k0 = pl.multiple_of(t * tk, tk)
        st_ref[...] = jnp.dot(ka_ref[0, pl.ds(k0, tk), :], qt_ref[...], preferred_element_type=F32)

    def fold(t, st_ref, diag=None):
        k0 = pl.multiple_of(t * tk, tk)
        st = st_ref[...]
        if diag is not None:
            kj = lax.broadcasted_iota(jnp.int32, (tk, tq), 0) + diag * tk
            qcol = lax.broadcasted_iota(jnp.int32, (tk, tq), 1)
            st = jnp.where(kj <= qcol, st, NEG)
        m_old = m_ref[...]
        m_new = jnp.maximum(m_old, jnp.max(st, axis=0, keepdims=True) + cq)
        p = jnp.exp2(st - (m_new - cq))
        alpha = jnp.exp2(m_old - m_new)
        l_ref[...] = alpha * l_ref[...] + jnp.sum(p, axis=0, keepdims=True)
        acc_ref[...] = alpha * acc_ref[...] + jnp.dot(
            vt_ref[0, :, pl.ds(k0, tk)], p.astype(BF), preferred_element_type=F32)
        m_ref[...] = m_new

    scores(0, sa_ref)

    def quad(u, _):
        t = 4 * u
        scores(t + 1, sb_ref)
        fold(t, sa_ref)
        scores(t + 2, sa_ref)
        fold(t + 1, sb_ref)
        scores(t + 3, sb_ref)
        fold(t + 2, sa_ref)
        scores(t + 4, sa_ref)
        fold(t + 3, sb_ref)
        return 0

    lax.fori_loop(0, qi // 2, quad, 0)
    td = 2 * qi

    @pl.when(qi % 2 == 1)
    def _():
        scores(td - 1, sb_ref)
        fold(td - 2, sa_ref)
        scores(td, sa_ref)
        fold(td - 1, sb_ref)

    scores(td + 1, sb_ref)
    fold(td, sa_ref, diag=0)
    fold(td + 1, sb_ref, diag=1)
    o_ref[...] = (acc_ref[...] / l_ref[...]).T.astype(BF)


def _fox_attn(qh, ka, vt, cr, *, T, tq=1024):
    H = FOX_HEADS
    tk = tq // 2
    vmem = 2 * T * 2 * FOX_HD * 2 + 2 * T * FOX_HD * 2 + 2 * H * T * 4 + 8 * tk * tq * 4 + 16 * tq * FOX_HD * 4
    return pl.pallas_call(
        functools.partial(_fox_attn_body, tq=tq, tk=tk),
        grid=(H, T // tq),
        in_specs=[
            pl.BlockSpec((1, tq, FOX_HD), lambda h, i: (h, i, 0)),
            pl.BlockSpec((1, T, 2 * FOX_HD), lambda h, i: (h, 0, 0)),
            pl.BlockSpec((1, FOX_HD, T), lambda h, i: (h, 0, 0)),
            pl.BlockSpec((H, T), lambda h, i: (0, 0)),
        ],
        out_specs=pl.BlockSpec((tq, FOX_HD), lambda h, i: (i, h)),
        out_shape=jax.ShapeDtypeStruct((T, D), BF),
        scratch_shapes=[pltpu.VMEM((2 * FOX_HD, tq), BF), pltpu.VMEM((1, tq), F32), pltpu.VMEM((1, tq), F32),
                        pltpu.VMEM((FOX_HD, tq), F32), pltpu.VMEM((tk, tq), F32), pltpu.VMEM((tk, tq), F32)],
        compiler_params=_params(("arbitrary", "arbitrary"), vmem),
        name="fox_attention",
    )(qh, ka, vt, cr)


def _suffix_body(x_ref, o_ref, car_ref, *, tc):
    @pl.when(pl.program_id(1) == 0)
    def _():
        car_ref[...] = jnp.zeros_like(car_ref)

    x = x_ref[0]
    o_ref[0] = _sum_dot_right(x, _tri(tc, "gt")) + car_ref[...]
    car_ref[...] = car_ref[...] + jnp.sum(x, axis=1, keepdims=True)


def _suffix_sum(x, *, tc=512):
    B, H, S = x.shape
    nc = S // tc
    return pl.pallas_call(
        functools.partial(_suffix_body, tc=tc),
        grid=(B, nc),
        in_specs=[pl.BlockSpec((1, H, tc), lambda b, t: (b, 0, nc - 1 - t))],
        out_specs=pl.BlockSpec((1, H, tc), lambda b, t: (b, 0, nc - 1 - t)),
        out_shape=jax.ShapeDtypeStruct((B, H, S), F32),
        scratch_shapes=[pltpu.VMEM((H, 1), F32)],
        compiler_params=_params(("arbitrary", "arbitrary"), 8 * tc * tc * 4),
        name="suffix_sum",
    )(x)


def _fox_decode_body(q_ref, ka_ref, vt_ref, kc_ref, vc_ref, dsuf_ref, ec_ref, er_ref, o_ref,
                     m_ref, l_ref, acc_ref, *, nt, tq, ts):
    b = pl.program_id(0)
    t = pl.program_id(1)
    H = FOX_HEADS

    @pl.when(t == 0)
    def _():
        m_ref[...] = jnp.full_like(m_ref, NEG)
        l_ref[...] = jnp.zeros_like(l_ref)
        acc_ref[...] = jnp.zeros_like(acc_ref)

    ec = ec_ref[...] * LOG2E
    dsuf = dsuf_ref[0] * LOG2E
    s_all = [_nt_dot(q_ref[hh], kc_ref[0, pl.ds(hh, ts, stride=H), :].astype(BF)) + dsuf[hh:hh + 1, :]
             for hh in range(H)]
    for hh in range(H):
        vt = vc_ref[0, pl.ds(hh, ts, stride=H), :].astype(BF)
        _softmax_step(s_all[hh], ec[:, hh:hh + 1], lambda p, vt=vt: jnp.dot(p, vt, preferred_element_type=F32),
                      m_ref.at[hh], l_ref.at[hh], acc_ref.at[hh])

    @pl.when(t == nt - 1)
    def _():
        n_new = er_ref.shape[1]
        er = er_ref[...] * LOG2E
        r = lax.broadcasted_iota(jnp.int32, (tq, n_new), 0)
        c = lax.broadcasted_iota(jnp.int32, (tq, n_new), 1) - b * tq
        valid = jnp.logical_and(c >= 0, c <= r)
        for hh in range(H):
            s = _nt_dot(q_ref[hh], ka_ref[hh][:, 0:FOX_HD]) - er[hh:hh + 1, :]
            s = jnp.where(valid, s, NEG)
            _softmax_step(s, ec[:, hh:hh + 1], lambda p, hh=hh: _nt_dot(p, vt_ref[hh]),
                          m_ref.at[hh], l_ref.at[hh], acc_ref.at[hh])
            o_ref[:, hh * FOX_HD:(hh + 1) * FOX_HD] = (acc_ref[hh] / l_ref[hh]).astype(BF)


def _fox_decode(qh, ka, vt, kc, vc, dsuf, ec, er, *, B, tq, ts=512):
    H = FOX_HEADS
    S = kc.shape[1] // H
    R = B * tq
    nt = S // ts
    vmem = (4 * ts * D * 4 + 2 * H * R * 3 * FOX_HD * 2 + 2 * H * tq * FOX_HD * 2 + 3 * H * tq * 128 * 4
            + 4 * ts * FOX_HD * 2 + 8 * tq * max(ts, R) * 4)
    return pl.pallas_call(
        functools.partial(_fox_decode_body, nt=nt, tq=tq, ts=ts),
        grid=(B, nt),
        in_specs=[
            pl.BlockSpec((H, tq, FOX_HD), lambda b, t: (0, b, 0)),
            pl.BlockSpec((H, R, 2 * FOX_HD), lambda b, t: (0, 0, 0)),
            pl.BlockSpec((H, FOX_HD, R), lambda b, t: (0, 0, 0)),
            pl.BlockSpec((1, ts * H, FOX_HD), lambda b, t: (b, t, 0)),
            pl.BlockSpec((1, ts * H, FOX_HD), lambda b, t: (b, t, 0)),
            pl.BlockSpec((1, H, ts), lambda b, t: (b, 0, t)),
            pl.BlockSpec((tq, H), lambda b, t: (b, 0)),
            pl.BlockSpec((H, R), lambda b, t: (0, 0)),
        ],
        out_specs=pl.BlockSpec((tq, D), lambda b, t: (b, 0)),
        out_shape=jax.ShapeDtypeStruct((R, D), BF),
        scratch_shapes=[pltpu.VMEM((H, tq, 1), F32), pltpu.VMEM((H, tq, 1), F32), pltpu.VMEM((H, tq, FOX_HD), F32)],
        compiler_params=_params(("arbitrary", "arbitrary"), vmem),
        name="fox_decode_attention",
    )(qh, ka, vt, kc, vc, dsuf, ec, er)


def _mlstm_qk_body(x_ref, g_ref, w_ref, wg_ref, wgt_ref, bg_ref, bgt_ref, q_ref, k_ref, gc_ref, gr_ref, *, qscale):
    h, y = _normed_dot(x_ref, g_ref, w_ref)
    zc = jnp.dot(h, wg_ref[...], preferred_element_type=F32) + bg_ref[...]
    lane = lax.broadcasted_iota(jnp.int32, zc.shape, 1)
    gc_ref[...] = jnp.where(lane < ML_HEADS, zc, _log_sigmoid(zc))
    zr = _nt_dot(wgt_ref[...], h) + bgt_ref[...]
    sub = lax.broadcasted_iota(jnp.int32, zr.shape, 0)
    gr_ref[...] = jnp.where(sub < ML_HEADS, zr, _log_sigmoid(zr))
    q_ref[...] = (y[:, :ML_QK] * qscale).astype(BF)
    k_ref[...] = y[:, ML_QK:]


def _proj_bf16_body(x_ref, g_ref, w_ref, o_ref):
    o_ref[...] = _normed_dot(x_ref, g_ref, w_ref)[1].astype(BF)


def _proj_f32_body(x_ref, g_ref, w_ref, o_ref):
    o_ref[...] = _normed_dot(x_ref, g_ref, w_ref)[1]


def _proj_plain(x, g, w, jcol, dtype, *, tm=512, name):
    R = x.shape[0]
    body = _proj_bf16_body if dtype == BF else _proj_f32_body
    vmem = 2 * tm * D * 4 + D * D * 2 + 2 * tm * D * 4 + tm * D * 2 + 2 * tm * D * 4
    return _proj_call(body, x, g, w, jcol, (), (), jax.ShapeDtypeStruct((R, D), dtype),
                      pl.BlockSpec((tm, D), lambda i: (i, 0)), (), tm=tm, vmem=vmem, name=name)


def _mlstm_proj(x, g, w, wg, wgt, bg, bgt, *, tm=512):
    R = x.shape[0]
    G = 2 * ML_HEADS
    const = lambda shape: pl.BlockSpec(shape, lambda i: (0,) * len(shape))
    vmem = 2 * tm * D * 4 + D * D * 2 + 2 * tm * D * 4 + tm * D * 2 + 2 * tm * D * 4
    q, k, gc, gr = _proj_call(
        functools.partial(_mlstm_qk_body, qscale=ML_DK ** -0.5), x, g, w, 0,
        (wg, wgt, bg, bgt), (const((D, G)), const((G, D)), const((1, G)), const((G, 1))),
        [
            jax.ShapeDtypeStruct((R, ML_QK), BF),
            jax.ShapeDtypeStruct((R, ML_QK), F32),
            jax.ShapeDtypeStruct((R, G), F32),
            jax.ShapeDtypeStruct((G, R), F32),
        ],
        [
            pl.BlockSpec((tm, ML_QK), lambda i: (i, 0)),
            pl.BlockSpec((tm, ML_QK), lambda i: (i, 0)),
            pl.BlockSpec((tm, G), lambda i: (i, 0)),
            pl.BlockSpec((G, tm), lambda i: (0, i)),
        ],
        (), tm=tm, vmem=vmem, name="mlstm_proj_qk")
    v = _proj_plain(x, g, w, 1, BF, tm=tm, name="mlstm_proj_v")
    o = _proj_plain(x, g, w, 2, F32, tm=tm, name="mlstm_proj_o")
    return q, k, v, o, gc, gr


def _mlstm_body(q_ref, k_ref, v_ref, o_ref, gc_ref, gr_ref, gn_ref, c0_ref, n0_ref, m0_ref,
                y_ref, co_ref, no_ref, mo_ref, c_ref, n_ref, m_ref, *, L):
    t = pl.program_id(1)

    @pl.when(t == 0)
    def _():
        c_ref[...] = c0_ref[0]
        n_ref[...] = n0_ref[0]
        m_ref[...] = m0_ref[0]

    gc = gc_ref[...]
    gr = gr_ref[0]
    ra = lax.broadcasted_iota(jnp.int32, (L, L), 0)
    cb = lax.broadcasted_iota(jnp.int32, (L, L), 1)
    causal = cb <= ra
    bcs = _sum_dot_left(_tri(L, "le"), gc)
    brs = _sum_dot_right(gr, _tri(L, "ge"))
    for hh in range(ML_HEADS):
        b_c = bcs[:, ML_HEADS + hh:ML_HEADS + hh + 1]
        g_c = gc[:, hh:hh + 1] - b_c
        g_r = gr[hh:hh + 1, :] - brs[ML_HEADS + hh:ML_HEADS + hh + 1, :]
        m_h = m_ref[hh][:, 0:1]
        am = jnp.where(causal, b_c + g_r, NEG)
        mt = jnp.maximum(b_c + m_h, jnp.max(am, axis=-1, keepdims=True))
        d = jnp.exp(am - mt)
        qh = q_ref[:, hh * ML_DK:(hh + 1) * ML_DK]
        kf = k_ref[:, hh * ML_DK:(hh + 1) * ML_DK]
        vh = v_ref[:, hh * ML_DV:(hh + 1) * ML_DV]
        sc = _nt_dot(qh, kf.astype(BF)) * d
        inter = jnp.exp(b_c + m_h - mt)
        c_old = c_ref[hh]
        n_old = n_ref[hh]
        num = (jnp.dot(sc.astype(BF), vh, preferred_element_type=F32)
               + inter * jnp.dot(qh, c_old.astype(BF), preferred_element_type=F32))
        den = (jnp.sum(sc, axis=-1, keepdims=True)
               + inter * jnp.sum(qh.astype(F32) * n_old, axis=-1, keepdims=True))
        hout = num / jnp.maximum(jnp.abs(den), jnp.exp(-mt))
        m_new = mt[L - 1:L, :]
        b_last = b_c[L - 1:L, :]
        decay = jnp.exp(b_last + m_h - m_new)
        wk = jnp.exp(b_last + g_c - m_new) * kf
        c_ref[hh] = decay * c_old + lax.dot_general(
            wk.astype(BF), vh, (((0,), (0,)), ((), ())), preferred_element_type=F32)
        n_ref[hh] = decay * n_old + jnp.sum(wk, axis=0, keepdims=True)
        m_ref[hh] = jnp.broadcast_to(m_new, (1, 128))
        hn = hout * lax.rsqrt(jnp.mean(hout * hout, axis=-1, keepdims=True) + EPS)
        cols = slice(hh * ML_DV, (hh + 1) * ML_DV)
        y_ref[:, cols] = (hn * gn_ref[:, cols] * jax.nn.sigmoid(o_ref[:, cols])).astype(BF)

    co_ref[0] = c_ref[...]
    no_ref[0] = n_ref[...]
    mo_ref[0] = m_ref[...]


def _mlstm_core(q, k, v, o, gc, gr, gn, c0, n0, m0, *, B, T, L):
    nt = T // L
    G = 2 * ML_HEADS
    H = ML_HEADS
    vmem = (2 * L * (ML_QK * 6 + D * 8) + 6 * H * ML_DK * ML_DV * 4 + 16 * L * L * 4 + 12 * L * ML_DV * 4
            + 4 * 1024 * 1024)
    rows = lambda w: pl.BlockSpec((L, w), lambda b, t: (b * nt + t, 0))
    st_c = pl.BlockSpec((1, H, ML_DK, ML_DV), lambda b, t: (b, 0, 0, 0))
    st_n = pl.BlockSpec((1, H, 1, ML_DK), lambda b, t: (b, 0, 0, 0))
    st_m = pl.BlockSpec((1, H, 1, 128), lambda b, t: (b, 0, 0, 0))
    return pl.pallas_call(
        functools.partial(_mlstm_body, L=L),
        grid=(B, nt),
        in_specs=[
            rows(ML_QK), rows(ML_QK), rows(D), rows(D), rows(G),
            pl.BlockSpec((1, G, L), lambda b, t: (b, 0, t)),
            pl.BlockSpec((1, D), lambda b, t: (0, 0)),
            st_c, st_n, st_m,
        ],
        out_specs=[rows(D), st_c, st_n, st_m],
        out_shape=[
            jax.ShapeDtypeStruct((B * T, D), BF),
            jax.ShapeDtypeStruct((B, H, ML_DK, ML_DV), F32),
            jax.ShapeDtypeStruct((B, H, 1, ML_DK), F32),
            jax.ShapeDtypeStruct((B, H, 1, 128), F32),
        ],
        scratch_shapes=[
            pltpu.VMEM((H, ML_DK, ML_DV), F32),
            pltpu.VMEM((H, 1, ML_DK), F32),
            pltpu.VMEM((H, 1, 128), F32),
        ],
        compiler_params=_params(("arbitrary", "arbitrary"), vmem),
        name="mlstm_core",
    )(q, k, v, o, gc, gr, gn, c0, n0, m0)


def kernel(x_prompt, x_sample, state_pool, state_lru_conv, state_lru_h, cache_fox_k, cache_fox_v, cache_fox_logf, state_mlstm_c, state_mlstm_n, state_mlstm_m, ffn1_norm, ffn1_w_gate, ffn1_w_up, ffn1_w_down, mix_norm, ffn2_norm, ffn2_w_gate, ffn2_w_up, ffn2_w_down, pool_w, pool_scale, lru_w_in, lru_conv_w, lru_conv_b, lru_w_a, lru_b_a, lru_w_i, lru_b_i, lru_lambda, lru_w_out, fox_w_qkv, fox_w_f, fox_b_f, fox_w_o, mlstm_w_in, mlstm_b_i, mlstm_b_f, mlstm_norm, mlstm_w_out, final_norm):
    BP, TP, _ = x_prompt.shape
    BS, TS, _ = x_sample.shape
    assert BP == 1 and x_prompt.shape[2] == D and x_sample.shape[2] == D
    RS = BS * TS
    past = cache_fox_k.shape[1]
    row = lambda p: p.reshape(1, -1).astype(F32)

    xp = x_prompt.reshape(TP, D)
    xs = x_sample.reshape(RS, D)

    ffn_w = {
        1: (ffn1_norm, ffn1_w_gate, ffn1_w_up, ffn1_w_down),
        2: (ffn2_norm, ffn2_w_gate, ffn2_w_up, ffn2_w_down),
    }

    def ffn(xp, xs, which, layer, last=False):
        norm, wg, wu, wd = ffn_w[which]
        g, gf = row(norm[layer]), row(final_norm)
        xs, wgb, wub, wdb = _ffn_cast(xs, g, wg.astype(F32), wu.astype(F32), wd.astype(F32), gf,
                                      layer=layer, normalize_out=last)
        return _ffn(xp, g, wgb, wub, wdb, gf, layer=0, normalize_out=last), xs

    xp, xs = ffn(xp, xs, 1, 0)
    pw = pool_w.astype(BF)
    hist_s = jnp.pad(state_pool.astype(F32), ((0, 0), (POOL_PAD - state_pool.shape[1], 0), (0, 0)))
    xp, pool_p = _pool_mixer(xp, row(mix_norm[0]), jnp.zeros((BP, POOL_PAD, D), F32), pw, row(pool_scale),
                             B=BP, T=TP, tm=512, pos0=0)
    xs, pool_s = _pool_mixer(xs, row(mix_norm[0]), hist_s, pw, row(pool_scale), B=BS, T=TS, tm=TS, pos0=past)
    pool_p = pool_p[:, 1:]
    pool_s = pool_s[:, 1:]
    xp, xs = ffn(xp, xs, 2, 0)

    xp, xs = ffn(xp, xs, 1, 1)
    w_in = lru_w_in.astype(BF)
    lru_args = (lru_conv_w.astype(F32), row(lru_conv_b), lru_w_a.astype(BF), row(lru_b_a),
                lru_w_i.astype(BF), row(lru_b_i), row(lru_lambda))
    lru_in = lambda x, jcol: _proj_plain(x, row(mix_norm[1]), w_in, jcol, F32, name="rglru_proj")
    cst_s = jnp.pad(state_lru_conv.astype(F32), ((0, 0), (CONV_PAD - (CONV_W - 1), 0), (0, 0)))
    yp, conv_p, h_p = _lru_core(lru_in(xp, 0), lru_in(xp, 1), jnp.zeros((BP, CONV_PAD, D), F32),
                                jnp.zeros((BP, 1, D), F32), *lru_args, B=BP, T=TP, tm=256)
    ys, conv_s, h_s = _lru_core(lru_in(xs, 0), lru_in(xs, 1), cst_s, state_lru_h.astype(F32).reshape(BS, 1, D),
                                *lru_args, B=BS, T=TS, tm=TS)
    w_out = lru_w_out.astype(BF)
    xp = _mm_res(yp, w_out, xp)
    xs = _mm_res(ys, w_out, xs)
    lru_conv_p, lru_conv_s = conv_p[:, CONV_PAD - (CONV_W - 1):], conv_s[:, CONV_PAD - (CONV_W - 1):]
    lru_h_p, lru_h_s = h_p.reshape(BP, D), h_s.reshape(BS, D)
    xp, xs = ffn(xp, xs, 2, 1)

    xp, xs = ffn(xp, xs, 1, 2)
    w_qkv = fox_w_qkv.astype(BF)
    wf = fox_w_f.astype(BF)
    wft = wf.T
    bf_r = fox_b_f.astype(F32).reshape(1, FOX_HEADS)
    bf_c = fox_b_f.astype(F32).reshape(FOX_HEADS, 1)
    qh_p, k_p, v_p, ka_p, vt_p, lf_p, cc_p, cr_p = _fox_proj(
        xp, row(mix_norm[2]), w_qkv, wf, wft, bf_r, bf_c, tm=512, seg=512, carry=True)
    qh_s, k_s, v_s, ka_s, vt_s, lf_s, cc_s, cr_s = _fox_proj(
        xs, row(mix_norm[2]), w_qkv, wf, wft, bf_r, bf_c, tm=RS, seg=TS, carry=False)
    o_p = _fox_attn(qh_p, ka_p, vt_p, cr_p, T=TP)
    dsuf = _suffix_sum(jnp.swapaxes(cache_fox_logf.astype(F32), 1, 2))
    o_s = _fox_decode(qh_s, ka_s, vt_s,
                      cache_fox_k.reshape(BS, past * FOX_HEADS, FOX_HD),
                      cache_fox_v.reshape(BS, past * FOX_HEADS, FOX_HD),
                      dsuf, cc_s, cr_s, B=BS, tq=TS)
    w_o = fox_w_o.astype(BF)
    xp = _mm_res(o_p, w_o, xp)
    xs = _mm_res(o_s, w_o, xs)
    fox_k_p = k_p.reshape(BP, TP, FOX_HEADS, FOX_HD)
    fox_v_p = v_p.reshape(BP, TP, FOX_HEADS, FOX_HD)
    fox_k_s = k_s.reshape(BS, TS, FOX_HEADS, FOX_HD)
    fox_v_s = v_s.reshape(BS, TS, FOX_HEADS, FOX_HD)
    fox_lf_p = lf_p.reshape(BP, TP, FOX_HEADS)
    fox_lf_s = lf_s.reshape(BS, TS, FOX_HEADS)
    xp, xs = ffn(xp, xs, 2, 2)

    xp, xs = ffn(xp, xs, 1, 3)
    n_main = 2 * ML_QK + 2 * D
    w_main = mlstm_w_in[:, :n_main].astype(BF)
    w_gates = mlstm_w_in[:, n_main:].astype(BF)
    b_gates = jnp.concatenate([mlstm_b_i, mlstm_b_f]).astype(F32)
    ml_w = (row(mix_norm[3]), w_main, w_gates, w_gates.T, b_gates.reshape(1, -1), b_gates.reshape(-1, 1))
    q_p, kk_p, vv_p, og_p, gc_p, gr_p = _mlstm_proj(xp, *ml_w)
    q_s, kk_s, vv_s, og_s, gc_s, gr_s = _mlstm_proj(xs, *ml_w)
    G = 2 * ML_HEADS
    gn = row(mlstm_norm)
    yp, ml_c_p, ml_n_p, ml_m_p = _mlstm_core(
        q_p, kk_p, vv_p, og_p, gc_p, gr_p.reshape(1, G, TP), gn,
        jnp.zeros((BP, ML_HEADS, ML_DK, ML_DV), F32), jnp.zeros((BP, ML_HEADS, 1, ML_DK), F32),
        jnp.zeros((BP, ML_HEADS, 1, 128), F32), B=BP, T=TP, L=256)
    m0_s = jnp.broadcast_to(state_mlstm_m.astype(F32)[:, :, None, None], (BS, ML_HEADS, 1, 128))
    ys, ml_c_s, ml_n_s, ml_m_s = _mlstm_core(
        q_s, kk_s, vv_s, og_s, gc_s, jnp.swapaxes(gr_s.reshape(G, BS, TS), 0, 1), gn,
        state_mlstm_c.astype(F32), state_mlstm_n.astype(F32).reshape(BS, ML_HEADS, 1, ML_DK), m0_s,
        B=BS, T=TS, L=TS)
    w_out = mlstm_w_out.astype(BF)
    xp = _mm_res(yp, w_out, xp)
    xs = _mm_res(ys, w_out, xs)
    ml_n_p, ml_n_s = ml_n_p.reshape(BP, ML_HEADS, ML_DK), ml_n_s.reshape(BS, ML_HEADS, ML_DK)
    ml_m_p, ml_m_s = ml_m_p[:, :, 0, 0], ml_m_s[:, :, 0, 0]
    xp, xs = ffn(xp, xs, 2, 3, last=True)

    y_prompt = xp.reshape(BP, TP, D)
    y_sample = xs.reshape(BS, TS, D)
    return (y_prompt, y_sample, pool_p, pool_s, lru_conv_p, lru_conv_s, lru_h_p, lru_h_s,
            fox_k_p, fox_k_s, fox_v_p, fox_v_s, fox_lf_p, fox_lf_s,
            ml_c_p, ml_c_s, ml_n_p, ml_n_s, ml_m_p, ml_m_s)
```

```python
import functools
import math

import jax
import jax.numpy as jnp
from jax import lax
from jax.experimental import pallas as pl
from jax.experimental.pallas import tpu as pltpu

F32 = jnp.float32
BF = jnp.bfloat16

D = 2048
D_FF = 5632
EPS = 1e-6
NEG = -1e30
LOG2E = math.log2(math.e)

POOL_WINDOWS = (2, 4, 8, 16)
POOL_GW = D // len(POOL_WINDOWS)
POOL_PAD = 16
CONV_W = 4
CONV_PAD = 8
LRU_BLOCKS = 8
LRU_BW = D // LRU_BLOCKS
LRU_C = 8.0
FOX_HEADS = 16
FOX_HD = D // FOX_HEADS
ML_HEADS = 8
ML_DV = D // ML_HEADS
ML_DK = ML_DV // 2
ML_QK = ML_HEADS * ML_DK

V7X_VMEM_LIMIT = 60000 * 1024


def _params(semantics, vmem_bytes):
    return pltpu.CompilerParams(dimension_semantics=semantics,
                                vmem_limit_bytes=min(int(vmem_bytes), V7X_VMEM_LIMIT))


def _rms(xf, g):
    ms = jnp.mean(xf * xf, axis=-1, keepdims=True)
    return xf * lax.rsqrt(ms + EPS) * g


def _nt_dot(a, b):
    return lax.dot_general(a, b, (((1,), (1,)), ((), ())), preferred_element_type=F32)


def _log_sigmoid(z):
    return jnp.minimum(z, 0.0) - jnp.log1p(jnp.exp(-jnp.abs(z)))


def _split3_bf16(c):
    hi = c.astype(BF)
    r1 = c - hi.astype(F32)
    mid = r1.astype(BF)
    lo = (r1 - mid.astype(F32)).astype(BF)
    return hi, mid, lo


def _sum_dot_left(tri, x):
    return sum(jnp.dot(tri, part, preferred_element_type=F32) for part in _split3_bf16(x))


def _sum_dot_right(x, tri):
    return sum(jnp.dot(part, tri, preferred_element_type=F32) for part in _split3_bf16(x))


def _tri(n, kind, seg=None):
    a = lax.broadcasted_iota(jnp.int32, (n, n), 0)
    b = lax.broadcasted_iota(jnp.int32, (n, n), 1)
    if kind == "le":
        m = b <= a
        if seg is not None:
            m = jnp.logical_and(m, b >= jnp.bitwise_and(a, -seg))
    elif kind == "ge":
        m = b >= a
        if seg is not None:
            m = jnp.logical_and(m, a >= jnp.bitwise_and(b, -seg))
    else:
        m = a > b
    return jnp.where(m, 1.0, 0.0).astype(BF)


def _ffn_chunk(x_ref, g_ref, wg_ref, wu_ref, wd_ref, go_ref, o_ref, h_ref, *, nj, normalize_out):
    j = pl.program_id(1)

    @pl.when(j == 0)
    def _():
        xf = x_ref[...]
        h_ref[...] = _rms(xf, g_ref[...]).astype(BF)
        o_ref[...] = xf

    h = h_ref[...]
    gt = jnp.dot(h, wg_ref[...], preferred_element_type=F32)
    up = jnp.dot(h, wu_ref[...], preferred_element_type=F32)
    a = (gt * jax.nn.sigmoid(gt) * up).astype(BF)
    o_ref[...] += jnp.dot(a, wd_ref[...], preferred_element_type=F32)

    if normalize_out:
        @pl.when(j == nj - 1)
        def _():
            o_ref[...] = _rms(o_ref[...], go_ref[...])


def _ffn_body(x_ref, g_ref, wg_ref, wu_ref, wd_ref, go_ref, o_ref, h_ref, **kw):
    _ffn_chunk(x_ref, g_ref, wg_ref, wu_ref, wd_ref, go_ref, o_ref, h_ref, **kw)


def _ffn_cast_body(x_ref, g_ref, wg_ref, wu_ref, wd_ref, go_ref, o_ref, wgb_ref, wub_ref, wdb_ref, h_ref, **kw):
    wgb_ref[...] = wg_ref[...].astype(BF)
    wub_ref[...] = wu_ref[...].astype(BF)
    wdb_ref[...] = (0.5 * wd_ref[...]).astype(BF)
    _ffn_chunk(x_ref, g_ref, wgb_ref, wub_ref, wdb_ref, go_ref, o_ref, h_ref, **kw)


def _ffn_cast(x, g, wg, wu, wd, out_gain, *, layer, normalize_out, tf=256):
    tm = x.shape[0]
    nj = D_FF // tf
    vmem = 4 * tm * D * 4 + tm * D * 2 + 6 * D * tf * 4 + 9 * D * tf * 2 + 5 * tm * tf * 4
    wcol = lambda dt: (pl.BlockSpec((None, D, tf), lambda i, j: (layer, 0, j)) if dt == F32
                       else pl.BlockSpec((None, D, tf), lambda i, j: (0, 0, j)))
    return pl.pallas_call(
        functools.partial(_ffn_cast_body, nj=nj, normalize_out=normalize_out),
        grid=(1, nj),
        in_specs=[
            pl.BlockSpec((tm, D), lambda i, j: (0, 0)),
            pl.BlockSpec((1, D), lambda i, j: (0, 0)),
            wcol(F32), wcol(F32),
            pl.BlockSpec((None, tf, D), lambda i, j: (layer, j, 0)),
            pl.BlockSpec((1, D), lambda i, j: (0, 0)),
        ],
        out_specs=[
            pl.BlockSpec((tm, D), lambda i, j: (0, 0)),
            wcol(BF), wcol(BF),
            pl.BlockSpec((None, tf, D), lambda i, j: (0, j, 0)),
        ],
        out_shape=[
            jax.ShapeDtypeStruct((tm, D), F32),
            jax.ShapeDtypeStruct((1, D, D_FF), BF),
            jax.ShapeDtypeStruct((1, D, D_FF), BF),
            jax.ShapeDtypeStruct((1, D_FF, D), BF),
        ],
        scratch_shapes=[pltpu.VMEM((tm, D), BF)],
        compiler_params=_params(("arbitrary", "arbitrary"), vmem),
        name="ffn_cast",
    )(x, g, wg, wu, wd, out_gain)


def _ffn(x, g, wg, wu, wd_half, out_gain, *, layer, normalize_out, tm=1024, tf=512):
    R = x.shape[0]
    tm = min(tm, R)
    nj = D_FF // tf
    vmem = 4 * tm * D * 4 + tm * D * 2 + 6 * D * tf * 2 + 5 * tm * tf * 4
    return pl.pallas_call(
        functools.partial(_ffn_body, nj=nj, normalize_out=normalize_out),
        grid=(R // tm, nj),
        in_specs=[
            pl.BlockSpec((tm, D), lambda i, j: (i, 0)),
            pl.BlockSpec((1, D), lambda i, j: (0, 0)),
            pl.BlockSpec((None, D, tf), lambda i, j: (layer, 0, j)),
            pl.BlockSpec((None, D, tf), lambda i, j: (layer, 0, j)),
            pl.BlockSpec((None, tf, D), lambda i, j: (layer, j, 0)),
            pl.BlockSpec((1, D), lambda i, j: (0, 0)),
        ],
        out_specs=pl.BlockSpec((tm, D), lambda i, j: (i, 0)),
        out_shape=jax.ShapeDtypeStruct((R, D), F32),
        scratch_shapes=[pltpu.VMEM((tm, D), BF)],
        compiler_params=_params(("arbitrary", "arbitrary"), vmem),
        name="ffn",
    )(x, g, wg, wu, wd_half, out_gain)


def _mm_res_body(a_ref, w_ref, x_ref, o_ref):
    o_ref[...] = x_ref[...] + jnp.dot(a_ref[...], w_ref[...], preferred_element_type=F32)


def _mm_res(a, w, x, *, tm=512):
    R = x.shape[0]
    vmem = 2 * tm * D * 2 + 2 * D * D * 2 + 5 * tm * D * 4
    return pl.pallas_call(
        _mm_res_body,
        grid=(R // tm,),
        in_specs=[
            pl.BlockSpec((tm, D), lambda i: (i, 0)),
            pl.BlockSpec((D, D), lambda i: (0, 0)),
            pl.BlockSpec((tm, D), lambda i: (i, 0)),
        ],
        out_specs=pl.BlockSpec((tm, D), lambda i: (i, 0)),
        out_shape=jax.ShapeDtypeStruct((R, D), F32),
        compiler_params=_params(("arbitrary",), vmem),
        name="mm_res",
    )(a, w, x)


def _pool_body(x_ref, g_ref, hist_ref, w_ref, sc_ref, o_ref, st_ref, xe_ref, *, tm, pos0):
    t = pl.program_id(1)

    @pl.when(t == 0)
    def _():
        xe_ref[0:POOL_PAD, :] = hist_ref[0]

    xf = x_ref[...]
    h = _rms(xf, g_ref[...])
    xe_ref[POOL_PAD:POOL_PAD + tm, :] = h
    row = lax.broadcasted_iota(jnp.int32, (tm, POOL_GW), 0)
    pos1 = (row + (pos0 + 1) + t * tm).astype(F32)
    for gi, w in enumerate(POOL_WINDOWS):
        c0 = gi * POOL_GW
        hg = h[:, c0:c0 + POOL_GW]
        s = hg
        for back in range(1, w):
            s = s + xe_ref[POOL_PAD - back:POOL_PAD - back + tm, c0:c0 + POOL_GW]
        cnt = jnp.minimum(pos1, float(w))
        pooled = s / cnt - hg
        y = jnp.dot(pooled.astype(BF), w_ref[gi], preferred_element_type=F32)
        o_ref[:, c0:c0 + POOL_GW] = xf[:, c0:c0 + POOL_GW] + y * sc_ref[:, c0:c0 + POOL_GW]
    tail = xe_ref[tm:tm + POOL_PAD, :]
    st_ref[0] = tail
    xe_ref[0:POOL_PAD, :] = tail


def _pool_mixer(x, g, hist, w, scale, *, B, T, tm, pos0):
    nt = T // tm
    vmem = 4 * tm * D * 4 + (tm + POOL_PAD) * D * 4 + 2 * 4 * POOL_GW * POOL_GW * 2 + 6 * tm * D * 4
    return pl.pallas_call(
        functools.partial(_pool_body, tm=tm, pos0=pos0),
        grid=(B, nt),
        in_specs=[
            pl.BlockSpec((tm, D), lambda b, t: (b * nt + t, 0)),
            pl.BlockSpec((1, D), lambda b, t: (0, 0)),
            pl.BlockSpec((1, POOL_PAD, D), lambda b, t: (b, 0, 0)),
            pl.BlockSpec((len(POOL_WINDOWS), POOL_GW, POOL_GW), lambda b, t: (0, 0, 0)),
            pl.BlockSpec((1, D), lambda b, t: (0, 0)),
        ],
        out_specs=[
            pl.BlockSpec((tm, D), lambda b, t: (b * nt + t, 0)),
            pl.BlockSpec((1, POOL_PAD, D), lambda b, t: (b, 0, 0)),
        ],
        out_shape=[jax.ShapeDtypeStruct((B * T, D), F32), jax.ShapeDtypeStruct((B, POOL_PAD, D), F32)],
        scratch_shapes=[pltpu.VMEM((tm + POOL_PAD, D), F32)],
        compiler_params=_params(("arbitrary", "arbitrary"), vmem),
        name="pool_mixer",
    )(x, g, hist, w, scale)


def _lru_body(gate_ref, xr_ref, cst_ref, h0_ref, cw_ref, cb_ref, wa_ref, ba_ref, wi_ref, bi_ref, lam_ref,
              y_ref, cso_ref, ho_ref, xe_ref, a_ref, u_ref, hc_ref, *, tm):
    t = pl.program_id(1)

    @pl.when(t == 0)
    def _():
        xe_ref[0:CONV_PAD, :] = cst_ref[0]
        hc_ref[...] = h0_ref[0]

    xe_ref[CONV_PAD:CONV_PAD + tm, :] = xr_ref[...]
    base = CONV_PAD - (CONV_W - 1)
    xc = cb_ref[...] + xe_ref[base:base + tm, :] * cw_ref[0:1, :]
    for j in range(1, CONV_W):
        xc = xc + xe_ref[base + j:base + j + tm, :] * cw_ref[j:j + 1, :]
    tail = xe_ref[tm:tm + CONV_PAD, :]
    cso_ref[0] = tail
    xe_ref[0:CONV_PAD, :] = tail

    xcb = xc.astype(BF)
    nlam = -lam_ref[...]
    sp = jnp.maximum(nlam, 0.0) + jnp.log1p(jnp.exp(-jnp.abs(nlam)))
    for n in range(LRU_BLOCKS):
        blk = slice(n * LRU_BW, (n + 1) * LRU_BW)
        ra = jnp.dot(xcb[:, blk], wa_ref[n], preferred_element_type=F32) + ba_ref[:, blk]
        ia = jnp.dot(xcb[:, blk], wi_ref[n], preferred_element_type=F32) + bi_ref[:, blk]
        ig = 0.5 * jnp.tanh(0.5 * ia) + 0.5
        log_a = (-0.5 * LRU_C * sp[:, blk]) * (jnp.tanh(0.5 * ra) + 1.0)
        a_ref[:, blk] = jnp.exp(log_a)
        th = jnp.tanh(log_a)
        one_minus_a2 = -2.0 * th / (1.0 - th)
        u_ref[:, blk] = jnp.sqrt(one_minus_a2) * ig * xc[:, blk]

    rowi = lax.broadcasted_iota(jnp.int32, (8, D), 0)

    def group(gi, carry):
        r0 = pl.multiple_of(gi * 8, 8)
        a8 = a_ref[pl.ds(r0, 8), :]
        u8 = u_ref[pl.ds(r0, 8), :]
        for d in (1, 2, 4):
            keep = rowi >= d
            u8 = jnp.where(keep, a8 * pltpu.roll(u8, d, axis=0) + u8, u8)
            a8 = jnp.where(keep, a8 * pltpu.roll(a8, d, axis=0), a8)
        hs8 = a8 * carry + u8
        u_ref[pl.ds(r0, 8), :] = hs8
        return hs8[7:8, :]

    carry = lax.fori_loop(0, tm // 8, group, hc_ref[...])
    hc_ref[...] = carry
    ho_ref[0] = carry
    y_ref[...] = (u_ref[...] * jax.nn.gelu(gate_ref[...])).astype(BF)


def _lru_core(gate, xr, cst, h0, cw, cb, wa, ba, wi, bi, lam, *, B, T, tm):
    nt = T // tm
    vmem = 4 * tm * D * 4 + 2 * tm * D * 2 + (3 * tm + CONV_PAD) * D * 4 + 4 * 8 * LRU_BW * LRU_BW * 2 + 6 * tm * D * 4
    vec = pl.BlockSpec((1, D), lambda b, t: (0, 0))
    wblk = pl.BlockSpec((LRU_BLOCKS, LRU_BW, LRU_BW), lambda b, t: (0, 0, 0))
    return pl.pallas_call(
        functools.partial(_lru_body, tm=tm),
        grid=(B, nt),
        in_specs=[
            pl.BlockSpec((tm, D), lambda b, t: (b * nt + t, 0)),
            pl.BlockSpec((tm, D), lambda b, t: (b * nt + t, 0)),
            pl.BlockSpec((1, CONV_PAD, D), lambda b, t: (b, 0, 0)),
            pl.BlockSpec((1, 1, D), lambda b, t: (b, 0, 0)),
            pl.BlockSpec((CONV_W, D), lambda b, t: (0, 0)),
            vec, wblk, vec, wblk, vec, vec,
        ],
        out_specs=[
            pl.BlockSpec((tm, D), lambda b, t: (b * nt + t, 0)),
            pl.BlockSpec((1, CONV_PAD, D), lambda b, t: (b, 0, 0)),
            pl.BlockSpec((1, 1, D), lambda b, t: (b, 0, 0)),
        ],
        out_shape=[
            jax.ShapeDtypeStruct((B * T, D), BF),
            jax.ShapeDtypeStruct((B, CONV_PAD, D), F32),
            jax.ShapeDtypeStruct((B, 1, D), F32),
        ],
        scratch_shapes=[
            pltpu.VMEM((tm + CONV_PAD, D), F32),
            pltpu.VMEM((tm, D), F32),
            pltpu.VMEM((tm, D), F32),
            pltpu.VMEM((1, D), F32),
        ],
        compiler_params=_params(("arbitrary", "arbitrary"), vmem),
        name="rglru_core",
    )(gate, xr, cst, h0, cw, cb, wa, ba, wi, bi, lam)


def _proj_call(body, x, g, w, jcol, extra, extra_specs, out_shape, out_specs, scratch, *, tm, vmem, name):
    R = x.shape[0]
    return pl.pallas_call(
        body,
        grid=(R // tm,),
        in_specs=[
            pl.BlockSpec((tm, D), lambda i: (i, 0)),
            pl.BlockSpec((1, D), lambda i: (0, 0)),
            pl.BlockSpec((D, D), lambda i: (0, jcol), pipeline_mode=pl.Buffered(1)),
            *extra_specs,
        ],
        out_specs=out_specs,
        out_shape=out_shape,
        scratch_shapes=scratch,
        compiler_params=_params(("arbitrary",), vmem),
        name=name,
    )(x, g, w, *extra)


def _normed_dot(x_ref, g_ref, w_ref):
    h = _rms(x_ref[...], g_ref[...]).astype(BF)
    return h, jnp.dot(h, w_ref[...], preferred_element_type=F32)


def _store_heads_interleaved(ref, y, tm):
    for hh in range(FOX_HEADS):
        ref[pl.ds(hh, tm, stride=FOX_HEADS), :] = y[:, hh * FOX_HD:(hh + 1) * FOX_HD]


def _fox_q_body(x_ref, g_ref, w_ref, q_ref, *, qscale):
    _, y = _normed_dot(x_ref, g_ref, w_ref)
    yq = (y * qscale).astype(BF)
    for hh in range(FOX_HEADS):
        q_ref[hh] = yq[:, hh * FOX_HD:(hh + 1) * FOX_HD]


def _fox_k_body(x_ref, g_ref, w_ref, wf_ref, wft_ref, bf_ref, bft_ref,
                k_ref, ka_ref, lf_ref, cc_ref, cr_ref, carc_ref, carr_ref, *, tm, seg, carry):
    h, y = _normed_dot(x_ref, g_ref, w_ref)
    lf = _log_sigmoid(jnp.dot(h, wf_ref[...], preferred_element_type=F32) + bf_ref[...])
    lft = _log_sigmoid(_nt_dot(wft_ref[...], h) + bft_ref[...])
    lf_ref[...] = lf
    cc = _sum_dot_left(_tri(tm, "le", seg), lf)
    cr = _sum_dot_right(lft, _tri(tm, "ge", seg))
    if carry:
        @pl.when(pl.program_id(0) == 0)
        def _():
            carc_ref[...] = jnp.zeros_like(carc_ref)
            carr_ref[...] = jnp.zeros_like(carr_ref)

        cc = cc + carc_ref[...]
        cr = cr + carr_ref[...]
        carc_ref[...] = cc[tm - 1:tm, :]
        carr_ref[...] = cr[:, tm - 1:tm]
    cc_ref[...] = cc
    cr_ref[...] = cr

    _store_heads_interleaved(k_ref, y, tm)
    yb = y.astype(BF)
    lane = lax.broadcasted_iota(jnp.int32, (tm, FOX_HD), 1)
    nck = cc * (-LOG2E)
    for hh in range(FOX_HEADS):
        hi, mid, lo = _split3_bf16(nck[:, hh:hh + 1])
        aug = jnp.where(lane == 0, hi.astype(F32),
                        jnp.where(lane == 1, mid.astype(F32), jnp.where(lane == 2, lo.astype(F32), 0.0)))
        ka_ref[hh, :, 0:FOX_HD] = yb[:, hh * FOX_HD:(hh + 1) * FOX_HD]
        ka_ref[hh, :, FOX_HD:2 * FOX_HD] = aug.astype(BF)


def _fox_v_body(x_ref, g_ref, w_ref, v_ref, vt_ref, *, tm):
    _, y = _normed_dot(x_ref, g_ref, w_ref)
    _store_heads_interleaved(v_ref, y, tm)
    for hh in range(FOX_HEADS):
        vt_ref[hh] = y[:, hh * FOX_HD:(hh + 1) * FOX_HD].T.astype(BF)


def _fox_proj(x, g, w, wf, wft, bf, bft, *, tm, seg, carry):
    R = x.shape[0]
    H = FOX_HEADS
    qscale = (FOX_HD ** -0.5) * LOG2E
    base = 2 * tm * D * 4 + D * D * 2 + 2 * tm * D * 4 + tm * D * 2
    const = lambda shape: pl.BlockSpec(shape, lambda i: (0,) * len(shape))
    native = pl.BlockSpec((tm * H, FOX_HD), lambda i: (i, 0))
    qh = _proj_call(
        functools.partial(_fox_q_body, qscale=qscale), x, g, w, 0, (), (),
        jax.ShapeDtypeStruct((H, R, FOX_HD), BF), pl.BlockSpec((H, tm, FOX_HD), lambda i: (0, i, 0)), (),
        tm=tm, vmem=base + 3 * tm * D * 2, name="fox_proj_q")
    k, ka, lf, cc, cr = _proj_call(
        functools.partial(_fox_k_body, tm=tm, seg=seg, carry=carry), x, g, w, 1,
        (wf, wft, bf, bft), (const((D, H)), const((H, D)), const((1, H)), const((H, 1))),
        [
            jax.ShapeDtypeStruct((R * H, FOX_HD), F32),
            jax.ShapeDtypeStruct((H, R, 2 * FOX_HD), BF),
            jax.ShapeDtypeStruct((R, H), F32),
            jax.ShapeDtypeStruct((R, H), F32),
            jax.ShapeDtypeStruct((H, R), F32),
        ],
        [
            native,
            pl.BlockSpec((H, tm, 2 * FOX_HD), lambda i: (0, i, 0)),
            pl.BlockSpec((tm, H), lambda i: (i, 0)),
            pl.BlockSpec((tm, H), lambda i: (i, 0)),
            pl.BlockSpec((H, tm), lambda i: (0, i)),
        ],
        [pltpu.VMEM((1, H), F32), pltpu.VMEM((H, 1), F32)],
        tm=tm, vmem=base + 2 * tm * D * 4 + 5 * tm * D * 2 + 3 * tm * tm * 4 + 4 * tm * 128 * 4, name="fox_proj_k")
    v, vt = _proj_call(
        functools.partial(_fox_v_body, tm=tm), x, g, w, 2, (), (),
        [jax.ShapeDtypeStruct((R * H, FOX_HD), F32), jax.ShapeDtypeStruct((H, FOX_HD, R), BF)],
        [native, pl.BlockSpec((H, FOX_HD, tm), lambda i: (0, 0, i))], (),
        tm=tm, vmem=base + 2 * tm * D * 4 + 3 * tm * D * 2 + tm * D * 4, name="fox_proj_v")
    return qh, k, v, ka, vt, lf, cc, cr


def _softmax_step(s_biased, cq, pv, m_ref, l_ref, acc_ref):
    m_old = m_ref[...]
    m_new = jnp.maximum(m_old, jnp.max(s_biased, axis=-1, keepdims=True) + cq)
    p = jnp.exp2(s_biased - (m_new - cq))
    alpha = jnp.exp2(m_old - m_new)
    l_ref[...] = alpha * l_ref[...] + jnp.sum(p, axis=-1, keepdims=True)
    acc_ref[...] = alpha * acc_ref[...] + pv(p.astype(BF))
    m_ref[...] = m_new


def _fox_attn_body(q_ref, ka_ref, vt_ref, cr_ref, o_ref, qt_ref, m_ref, l_ref, acc_ref, sa_ref, sb_ref, *, tq, tk):
    h = pl.program_id(0)
    qi = pl.program_id(1)
    q0 = pl.multiple_of(qi * tq, tq)
    qt_ref[0:FOX_HD, :] = q_ref[0].astype(F32).T.astype(BF)
    sub = lax.broadcasted_iota(jnp.int32, (FOX_HD, tq), 0)
    qt_ref[FOX_HD:2 * FOX_HD, :] = jnp.where(sub < 3, 1.0, 0.0).astype(BF)
    cq = cr_ref[pl.ds(h, 1), pl.ds(q0, tq)] * LOG2E
    m_ref[...] = jnp.full_like(m_ref, NEG)
    l_ref[...] = jnp.zeros_like(l_ref)
    acc_ref[...] = jnp.zeros_like(acc_ref)

    def scores(t, st_ref):
        k0 = pl.multiple_of(t * tk, tk)
        st_ref[...] = jnp.dot(ka_ref[0, pl.ds(k0, tk), :], qt_ref[...], preferred_element_type=F32)

    def fold(t, st_ref, diag=None):
        k0 = pl.multiple_of(t * tk, tk)
        st = st_ref[...]
        if diag is not None:
            kj = lax.broadcasted_iota(jnp.int32, (tk, tq), 0) + diag * tk
            qcol = lax.broadcasted_iota(jnp.int32, (tk, tq), 1)
            st = jnp.where(kj <= qcol, st, NEG)
        m_old = m_ref[...]
        m_new = jnp.maximum(m_old, jnp.max(st, axis=0, keepdims=True) + cq)
        p = jnp.exp2(st - (m_new - cq))
        alpha = jnp.exp2(m_old - m_new)
        l_ref[...] = alpha * l_ref[...] + jnp.sum(p, axis=0, keepdims=True)
        acc_ref[...] = alpha * acc_ref[...] + jnp.dot(
            vt_ref[0, :, pl.ds(k0, tk)], p.astype(BF), preferred_element_type=F32)
        m_ref[...] = m_new

    scores(0, sa_ref)

    def quad(u, _):
        t = 4 * u
        scores(t + 1, sb_ref)
        fold(t, sa_ref)
        scores(t + 2, sa_ref)
        fold(t + 1, sb_ref)
        scores(t + 3, sb_ref)
        fold(t + 2, sa_ref)
        scores(t + 4, sa_ref)
        fold(t + 3, sb_ref)
        return 0

    lax.fori_loop(0, qi // 2, quad, 0)
    td = 2 * qi

    @pl.when(qi % 2 == 1)
    def _():
        scores(td - 1, sb_ref)
        fold(td - 2, sa_ref)
        scores(td, sa_ref)
        fold(td - 1, sb_ref)

    scores(td + 1, sb_ref)
    fold(td, sa_ref, diag=0)
    fold(td + 1, sb_ref, diag=1)
    o_ref[...] = (acc_ref[...] / l_ref[...]).T.astype(BF)


def _fox_attn(qh, ka, vt, cr, *, T, tq=1024):
    H = FOX_HEADS
    tk = tq // 2
    vmem = 2 * T * 2 * FOX_HD * 2 + 2 * T * FOX_HD * 2 + 2 * H * T * 4 + 8 * tk * tq * 4 + 16 * tq * FOX_HD * 4
    return pl.pallas_call(
        functools.partial(_fox_attn_body, tq=tq, tk=tk),
        grid=(H, T // tq),
        in_specs=[
            pl.BlockSpec((1, tq, FOX_HD), lambda h, i: (h, i, 0)),
            pl.BlockSpec((1, T, 2 * FOX_HD), lambda h, i: (h, 0, 0)),
            pl.BlockSpec((1, FOX_HD, T), lambda h, i: (h, 0, 0)),
            pl.BlockSpec((H, T), lambda h, i: (0, 0)),
        ],
        out_specs=pl.BlockSpec((tq, FOX_HD), lambda h, i: (i, h)),
        out_shape=jax.ShapeDtypeStruct((T, D), BF),
        scratch_shapes=[pltpu.VMEM((2 * FOX_HD, tq), BF), pltpu.VMEM((1, tq), F32), pltpu.VMEM((1, tq), F32),
                        pltpu.VMEM((FOX_HD, tq), F32), pltpu.VMEM((tk, tq), F32), pltpu.VMEM((tk, tq), F32)],
        compiler_params=_params(("arbitrary", "arbitrary"), vmem),
        name="fox_attention",
    )(qh, ka, vt, cr)


def _suffix_body(x_ref, o_ref, car_ref, *, tc):
    @pl.when(pl.program_id(1) == 0)
    def _():
        car_ref[...] = jnp.zeros_like(car_ref)

    x = x_ref[0]
    o_ref[0] = _sum_dot_right(x, _tri(tc, "gt")) + car_ref[...]
    car_ref[...] = car_ref[...] + jnp.sum(x, axis=1, keepdims=True)


def _suffix_sum(x, *, tc=512):
    B, H, S = x.shape
    nc = S // tc
    return pl.pallas_call(
        functools.partial(_suffix_body, tc=tc),
        grid=(B, nc),
        in_specs=[pl.BlockSpec((1, H, tc), lambda b, t: (b, 0, nc - 1 - t))],
        out_specs=pl.BlockSpec((1, H, tc), lambda b, t: (b, 0, nc - 1 - t)),
        out_shape=jax.ShapeDtypeStruct((B, H, S), F32),
        scratch_shapes=[pltpu.VMEM((H, 1), F32)],
        compiler_params=_params(("arbitrary", "arbitrary"), 8 * tc * tc * 4),
        name="suffix_sum",
    )(x)


def _fox_decode_body(q_ref, ka_ref, vt_ref, kc_ref, vc_ref, dsuf_ref, ec_ref, er_ref, o_ref,
                     m_ref, l_ref, acc_ref, *, nt, tq, ts):
    b = pl.program_id(0)
    t = pl.program_id(1)
    H = FOX_HEADS

    @pl.when(t == 0)
    def _():
        m_ref[...] = jnp.full_like(m_ref, NEG)
        l_ref[...] = jnp.zeros_like(l_ref)
        acc_ref[...] = jnp.zeros_like(acc_ref)

    ec = ec_ref[...] * LOG2E
    dsuf = dsuf_ref[0] * LOG2E
    s_all = [_nt_dot(q_ref[hh], kc_ref[0, pl.ds(hh, ts, stride=H), :].astype(BF)) + dsuf[hh:hh + 1, :]
             for hh in range(H)]
    for hh in range(H):
        vt = vc_ref[0, pl.ds(hh, ts, stride=H), :].astype(BF)
        _softmax_step(s_all[hh], ec[:, hh:hh + 1], lambda p, vt=vt: jnp.dot(p, vt, preferred_element_type=F32),
                      m_ref.at[hh], l_ref.at[hh], acc_ref.at[hh])

    @pl.when(t == nt - 1)
    def _():
        n_new = er_ref.shape[1]
        er = er_ref[...] * LOG2E
        r = lax.broadcasted_iota(jnp.int32, (tq, n_new), 0)
        c = lax.broadcasted_iota(jnp.int32, (tq, n_new), 1) - b * tq
        valid = jnp.logical_and(c >= 0, c <= r)
        for hh in range(H):
            s = _nt_dot(q_ref[hh], ka_ref[hh][:, 0:FOX_HD]) - er[hh:hh + 1, :]
            s = jnp.where(valid, s, NEG)
            _softmax_step(s, ec[:, hh:hh + 1], lambda p, hh=hh: _nt_dot(p, vt_ref[hh]),
                          m_ref.at[hh], l_ref.at[hh], acc_ref.at[hh])
            o_ref[:, hh * FOX_HD:(hh + 1) * FOX_HD] = (acc_ref[hh] / l_ref[hh]).astype(BF)


def _fox_decode(qh, ka, vt, kc, vc, dsuf, ec, er, *, B, tq, ts=512):
    H = FOX_HEADS
    S = kc.shape[1] // H
    R = B * tq
    nt = S // ts
    vmem = (4 * ts * D * 4 + 2 * H * R * 3 * FOX_HD * 2 + 2 * H * tq * FOX_HD * 2 + 3 * H * tq * 128 * 4
            + 4 * ts * FOX_HD * 2 + 8 * tq * max(ts, R) * 4)
    return pl.pallas_call(
        functools.partial(_fox_decode_body, nt=nt, tq=tq, ts=ts),
        grid=(B, nt),
        in_specs=[
            pl.BlockSpec((H, tq, FOX_HD), lambda b, t: (0, b, 0)),
            pl.BlockSpec((H, R, 2 * FOX_HD), lambda b, t: (0, 0, 0)),
            pl.BlockSpec((H, FOX_HD, R), lambda b, t: (0, 0, 0)),
            pl.BlockSpec((1, ts * H, FOX_HD), lambda b, t: (b, t, 0)),
            pl.BlockSpec((1, ts * H, FOX_HD), lambda b, t: (b, t, 0)),
            pl.BlockSpec((1, H, ts), lambda b, t: (b, 0, t)),
            pl.BlockSpec((tq, H), lambda b, t: (b, 0)),
            pl.BlockSpec((H, R), lambda b, t: (0, 0)),
        ],
        out_specs=pl.BlockSpec((tq, D), lambda b, t: (b, 0)),
        out_shape=jax.ShapeDtypeStruct((R, D), BF),
        scratch_shapes=[pltpu.VMEM((H, tq, 1), F32), pltpu.VMEM((H, tq, 1), F32), pltpu.VMEM((H, tq, FOX_HD), F32)],
        compiler_params=_params(("arbitrary", "arbitrary"), vmem),
        name="fox_decode_attention",
    )(qh, ka, vt, kc, vc, dsuf, ec, er)


def _mlstm_qk_body(x_ref, g_ref, w_ref, wg_ref, wgt_ref, bg_ref, bgt_ref, q_ref, k_ref, gc_ref, gr_ref, *, qscale):
    h, y = _normed_dot(x_ref, g_ref, w_ref)
    zc = jnp.dot(h, wg_ref[...], preferred_element_type=F32) + bg_ref[...]
    lane = lax.broadcasted_iota(jnp.int32, zc.shape, 1)
    gc_ref[...] = jnp.where(lane < ML_HEADS, zc, _log_sigmoid(zc))
    zr = _nt_dot(wgt_ref[...], h) + bgt_ref[...]
    sub = lax.broadcasted_iota(jnp.int32, zr.shape, 0)
    gr_ref[...] = jnp.where(sub < ML_HEADS, zr, _log_sigmoid(zr))
    q_ref[...] = (y[:, :ML_QK] * qscale).astype(BF)
    k_ref[...] = y[:, ML_QK:]


def _proj_bf16_body(x_ref, g_ref, w_ref, o_ref):
    o_ref[...] = _normed_dot(x_ref, g_ref, w_ref)[1].astype(BF)


def _proj_f32_body(x_ref, g_ref, w_ref, o_ref):
    o_ref[...] = _normed_dot(x_ref, g_ref, w_ref)[1]


def _proj_plain(x, g, w, jcol, dtype, *, tm=512, name):
    R = x.shape[0]
    body = _proj_bf16_body if dtype == BF else _proj_f32_body
    vmem = 2 * tm * D * 4 + D * D * 2 + 2 * tm * D * 4 + tm * D * 2 + 2 * tm * D * 4
    return _proj_call(body, x, g, w, jcol, (), (), jax.ShapeDtypeStruct((R, D), dtype),
                      pl.BlockSpec((tm, D), lambda i: (i, 0)), (), tm=tm, vmem=vmem, name=name)


def _mlstm_proj(x, g, w, wg, wgt, bg, bgt, *, tm=512):
    R = x.shape[0]
    G = 2 * ML_HEADS
    const = lambda shape: pl.BlockSpec(shape, lambda i: (0,) * len(shape))
    vmem = 2 * tm * D * 4 + D * D * 2 + 2 * tm * D * 4 + tm * D * 2 + 2 * tm * D * 4
    q, k, gc, gr = _proj_call(
        functools.partial(_mlstm_qk_body, qscale=ML_DK ** -0.5), x, g, w, 0,
        (wg, wgt, bg, bgt), (const((D, G)), const((G, D)), const((1, G)), const((G, 1))),
        [
            jax.ShapeDtypeStruct((R, ML_QK), BF),
            jax.ShapeDtypeStruct((R, ML_QK), F32),
            jax.ShapeDtypeStruct((R, G), F32),
            jax.ShapeDtypeStruct((G, R), F32),
        ],
        [
            pl.BlockSpec((tm, ML_QK), lambda i: (i, 0)),
            pl.BlockSpec((tm, ML_QK), lambda i: (i, 0)),
            pl.BlockSpec((tm, G), lambda i: (i, 0)),
            pl.BlockSpec((G, tm), lambda i: (0, i)),
        ],
        (), tm=tm, vmem=vmem, name="mlstm_proj_qk")
    v = _proj_plain(x, g, w, 1, BF, tm=tm, name="mlstm_proj_v")
    o = _proj_plain(x, g, w, 2, F32, tm=tm, name="mlstm_proj_o")
    return q, k, v, o, gc, gr


def _mlstm_body(q_ref, k_ref, v_ref, o_ref, gc_ref, gr_ref, gn_ref, c0_ref, n0_ref, m0_ref,
                y_ref, co_ref, no_ref, mo_ref, c_ref, n_ref, m_ref, *, L):
    t = pl.program_id(1)

    @pl.when(t == 0)
    def _():
        c_ref[...] = c0_ref[0]
        n_ref[...] = n0_ref[0]
        m_ref[...] = m0_ref[0]

    gc = gc_ref[...]
    gr = gr_ref[0]
    ra = lax.broadcasted_iota(jnp.int32, (L, L), 0)
    cb = lax.broadcasted_iota(jnp.int32, (L, L), 1)
    causal = cb <= ra
    bcs = _sum_dot_left(_tri(L, "le"), gc)
    brs = _sum_dot_right(gr, _tri(L, "ge"))
    for hh in range(ML_HEADS):
        b_c = bcs[:, ML_HEADS + hh:ML_HEADS + hh + 1]
        g_c = gc[:, hh:hh + 1] - b_c
        g_r = gr[hh:hh + 1, :] - brs[ML_HEADS + hh:ML_HEADS + hh + 1, :]
        m_h = m_ref[hh][:, 0:1]
        am = jnp.where(causal, b_c + g_r, NEG)
        mt = jnp.maximum(b_c + m_h, jnp.max(am, axis=-1, keepdims=True))
        d = jnp.exp(am - mt)
        qh = q_ref[:, hh * ML_DK:(hh + 1) * ML_DK]
        kf = k_ref[:, hh * ML_DK:(hh + 1) * ML_DK]
        vh = v_ref[:, hh * ML_DV:(hh + 1) * ML_DV]
        sc = _nt_dot(qh, kf.astype(BF)) * d
        inter = jnp.exp(b_c + m_h - mt)
        c_old = c_ref[hh]
        n_old = n_ref[hh]
        num = (jnp.dot(sc.astype(BF), vh, preferred_element_type=F32)
               + inter * jnp.dot(qh, c_old.astype(BF), preferred_element_type=F32))
        den = (jnp.sum(sc, axis=-1, keepdims=True)
               + inter * jnp.sum(qh.astype(F32) * n_old, axis=-1, keepdims=True))
        hout = num / jnp.maximum(jnp.abs(den), jnp.exp(-mt))
        m_new = mt[L - 1:L, :]
        b_last = b_c[L - 1:L, :]
        decay = jnp.exp(b_last + m_h - m_new)
        wk = jnp.exp(b_last + g_c - m_new) * kf
        c_ref[hh] = decay * c_old + lax.dot_general(
            wk.astype(BF), vh, (((0,), (0,)), ((), ())), preferred_element_type=F32)
        n_ref[hh] = decay * n_old + jnp.sum(wk, axis=0, keepdims=True)
        m_ref[hh] = jnp.broadcast_to(m_new, (1, 128))
        hn = hout * lax.rsqrt(jnp.mean(hout * hout, axis=-1, keepdims=True) + EPS)
        cols = slice(hh * ML_DV, (hh + 1) * ML_DV)
        y_ref[:, cols] = (hn * gn_ref[:, cols] * jax.nn.sigmoid(o_ref[:, cols])).astype(BF)

    co_ref[0] = c_ref[...]
    no_ref[0] = n_ref[...]
    mo_ref[0] = m_ref[...]


def _mlstm_core(q, k, v, o, gc, gr, gn, c0, n0, m0, *, B, T, L):
    nt = T // L
    G = 2 * ML_HEADS
    H = ML_HEADS
    vmem = (2 * L * (ML_QK * 6 + D * 8) + 6 * H * ML_DK * ML_DV * 4 + 16 * L * L * 4 + 12 * L * ML_DV * 4
            + 4 * 1024 * 1024)
    rows = lambda w: pl.BlockSpec((L, w), lambda b, t: (b * nt + t, 0))
    st_c = pl.BlockSpec((1, H, ML_DK, ML_DV), lambda b, t: (b, 0, 0, 0))
    st_n = pl.BlockSpec((1, H, 1, ML_DK), lambda b, t: (b, 0, 0, 0))
    st_m = pl.BlockSpec((1, H, 1, 128), lambda b, t: (b, 0, 0, 0))
    return pl.pallas_call(
        functools.partial(_mlstm_body, L=L),
        grid=(B, nt),
        in_specs=[
            rows(ML_QK), rows(ML_QK), rows(D), rows(D), rows(G),
            pl.BlockSpec((1, G, L), lambda b, t: (b, 0, t)),
            pl.BlockSpec((1, D), lambda b, t: (0, 0)),
            st_c, st_n, st_m,
        ],
        out_specs=[rows(D), st_c, st_n, st_m],
        out_shape=[
            jax.ShapeDtypeStruct((B * T, D), BF),
            jax.ShapeDtypeStruct((B, H, ML_DK, ML_DV), F32),
            jax.ShapeDtypeStruct((B, H, 1, ML_DK), F32),
            jax.ShapeDtypeStruct((B, H, 1, 128), F32),
        ],
        scratch_shapes=[
            pltpu.VMEM((H, ML_DK, ML_DV), F32),
            pltpu.VMEM((H, 1, ML_DK), F32),
            pltpu.VMEM((H, 1, 128), F32),
        ],
        compiler_params=_params(("arbitrary", "arbitrary"), vmem),
        name="mlstm_core",
    )(q, k, v, o, gc, gr, gn, c0, n0, m0)


def kernel(x_prompt, x_sample, state_pool, state_lru_conv, state_lru_h, cache_fox_k, cache_fox_v, cache_fox_logf, state_mlstm_c, state_mlstm_n, state_mlstm_m, ffn1_norm, ffn1_w_gate, ffn1_w_up, ffn1_w_down, mix_norm, ffn2_norm, ffn2_w_gate, ffn2_w_up, ffn2_w_down, pool_w, pool_scale, lru_w_in, lru_conv_w, lru_conv_b, lru_w_a, lru_b_a, lru_w_i, lru_b_i, lru_lambda, lru_w_out, fox_w_qkv, fox_w_f, fox_b_f, fox_w_o, mlstm_w_in, mlstm_b_i, mlstm_b_f, mlstm_norm, mlstm_w_out, final_norm):
    BP, TP, _ = x_prompt.shape
    BS, TS, _ = x_sample.shape
    assert BP == 1 and x_prompt.shape[2] == D and x_sample.shape[2] == D
    RS = BS * TS
    past = cache_fox_k.shape[1]
    row = lambda p: p.reshape(1, -1).astype(F32)

    xp = x_prompt.reshape(TP, D)
    xs = x_sample.reshape(RS, D)

    ffn_w = {
        1: (ffn1_norm, ffn1_w_gate, ffn1_w_up, ffn1_w_down),
        2: (ffn2_norm, ffn2_w_gate, ffn2_w_up, ffn2_w_down),
    }

    def ffn(xp, xs, which, layer, last=False):
        norm, wg, wu, wd = ffn_w[which]
        g, gf = row(norm[layer]), row(final_norm)
        xs, wgb, wub, wdb = _ffn_cast(xs, g, wg.astype(F32), wu.astype(F32), wd.astype(F32), gf,
                                      layer=layer, normalize_out=last)
        return _ffn(xp, g, wgb, wub, wdb, gf, layer=0, normalize_out=last), xs

    xp, xs = ffn(xp, xs, 1, 0)
    pw = pool_w.astype(BF)
    hist_s = jnp.pad(state_pool.astype(F32), ((0, 0), (POOL_PAD - state_pool.shape[1], 0), (0, 0)))
    xp, pool_p = _pool_mixer(xp, row(mix_norm[0]), jnp.zeros((BP, POOL_PAD, D), F32), pw, row(pool_scale),
                             B=BP, T=TP, tm=512, pos0=0)
    xs, pool_s = _pool_mixer(xs, row(mix_norm[0]), hist_s, pw, row(pool_scale), B=BS, T=TS, tm=TS, pos0=past)
    pool_p = pool_p[:, 1:]
    pool_s = pool_s[:, 1:]
    xp, xs = ffn(xp, xs, 2, 0)

    xp, xs = ffn(xp, xs, 1, 1)
    w_in = lru_w_in.astype(BF)
    lru_args = (lru_conv_w.astype(F32), row(lru_conv_b), lru_w_a.astype(BF), row(lru_b_a),
                lru_w_i.astype(BF), row(lru_b_i), row(lru_lambda))
    lru_in = lambda x, jcol: _proj_plain(x, row(mix_norm[1]), w_in, jcol, F32, name="rglru_proj")
    cst_s = jnp.pad(state_lru_conv.astype(F32), ((0, 0), (CONV_PAD - (CONV_W - 1), 0), (0, 0)))
    yp, conv_p, h_p = _lru_core(lru_in(xp, 0), lru_in(xp, 1), jnp.zeros((BP, CONV_PAD, D), F32),
                                jnp.zeros((BP, 1, D), F32), *lru_args, B=BP, T=TP, tm=256)
    ys, conv_s, h_s = _lru_core(lru_in(xs, 0), lru_in(xs, 1), cst_s, state_lru_h.astype(F32).reshape(BS, 1, D),
                                *lru_args, B=BS, T=TS, tm=TS)
    w_out = lru_w_out.astype(BF)
    xp = _mm_res(yp, w_out, xp)
    xs = _mm_res(ys, w_out, xs)
    lru_conv_p, lru_conv_s = conv_p[:, CONV_PAD - (CONV_W - 1):], conv_s[:, CONV_PAD - (CONV_W - 1):]
    lru_h_p, lru_h_s = h_p.reshape(BP, D), h_s.reshape(BS, D)
    xp, xs = ffn(xp, xs, 2, 1)

    xp, xs = ffn(xp, xs, 1, 2)
    w_qkv = fox_w_qkv.astype(BF)
    wf = fox_w_f.astype(BF)
    wft = wf.T
    bf_r = fox_b_f.astype(F32).reshape(1, FOX_HEADS)
    bf_c = fox_b_f.astype(F32).reshape(FOX_HEADS, 1)
    qh_p, k_p, v_p, ka_p, vt_p, lf_p, cc_p, cr_p = _fox_proj(
        xp, row(mix_norm[2]), w_qkv, wf, wft, bf_r, bf_c, tm=512, seg=512, carry=True)
    qh_s, k_s, v_s, ka_s, vt_s, lf_s, cc_s, cr_s = _fox_proj(
        xs, row(mix_norm[2]), w_qkv, wf, wft, bf_r, bf_c, tm=RS, seg=TS, carry=False)
    o_p = _fox_attn(qh_p, ka_p, vt_p, cr_p, T=TP)
    dsuf = _suffix_sum(jnp.swapaxes(cache_fox_logf.astype(F32), 1, 2))
    o_s = _fox_decode(qh_s, ka_s, vt_s,
                      cache_fox_k.reshape(BS, past * FOX_HEADS, FOX_HD),
                      cache_fox_v.reshape(BS, past * FOX_HEADS, FOX_HD),
                      dsuf, cc_s, cr_s, B=BS, tq=TS)
    w_o = fox_w_o.astype(BF)
    xp = _mm_res(o_p, w_o, xp)
    xs = _mm_res(o_s, w_o, xs)
    fox_k_p = k_p.reshape(BP, TP, FOX_HEADS, FOX_HD)
    fox_v_p = v_p.reshape(BP, TP, FOX_HEADS, FOX_HD)
    fox_k_s = k_s.reshape(BS, TS, FOX_HEADS, FOX_HD)
    fox_v_s = v_s.reshape(BS, TS, FOX_HEADS, FOX_HD)
    fox_lf_p = lf_p.reshape(BP, TP, FOX_HEADS)
    fox_lf_s = lf_s.reshape(BS, TS, FOX_HEADS)
    xp, xs = ffn(xp, xs, 2, 2)

    xp, xs = ffn(xp, xs, 1, 3)
    n_main = 2 * ML_QK + 2 * D
    w_main = mlstm_w_in[:, :n_main].astype(BF)
    w_gates = mlstm_w_in[:, n_main:].astype(BF)
    b_gates = jnp.concatenate([mlstm_b_i, mlstm_b_f]).astype(F32)
    ml_w = (row(mix_norm[3]), w_main, w_gates, w_gates.T, b_gates.reshape(1, -1), b_gates.reshape(-1, 1))
    q_p, kk_p, vv_p, og_p, gc_p, gr_p = _mlstm_proj(xp, *ml_w)
    q_s, kk_s, vv_s, og_s, gc_s, gr_s = _mlstm_proj(xs, *ml_w)
    G = 2 * ML_HEADS
    gn = row(mlstm_norm)
    yp, ml_c_p, ml_n_p, ml_m_p = _mlstm_core(
        q_p, kk_p, vv_p, og_p, gc_p, gr_p.reshape(1, G, TP), gn,
        jnp.zeros((BP, ML_HEADS, ML_DK, ML_DV), F32), jnp.zeros((BP, ML_HEADS, 1, ML_DK), F32),
        jnp.zeros((BP, ML_HEADS, 1, 128), F32), B=BP, T=TP, L=256)
    m0_s = jnp.broadcast_to(state_mlstm_m.astype(F32)[:, :, None, None], (BS, ML_HEADS, 1, 128))
    ys, ml_c_s, ml_n_s, ml_m_s = _mlstm_core(
        q_s, kk_s, vv_s, og_s, gc_s, jnp.swapaxes(gr_s.reshape(G, BS, TS), 0, 1), gn,
        state_mlstm_c.astype(F32), state_mlstm_n.astype(F32).reshape(BS, ML_HEADS, 1, ML_DK), m0_s,
        B=BS, T=TS, L=TS)
    w_out = mlstm_w_out.astype(BF)
    xp = _mm_res(yp, w_out, xp)
    xs = _mm_res(ys, w_out, xs)
    ml_n_p, ml_n_s = ml_n_p.reshape(BP, ML_HEADS, ML_DK), ml_n_s.reshape(BS, ML_HEADS, ML_DK)
    ml_m_p, ml_m_s = ml_m_p[:, :, 0, 0], ml_m_s[:, :, 0, 0]
    xp, xs = ffn(xp, xs, 2, 3, last=True)

    y_prompt = xp.reshape(BP, TP, D)
    y_sample = xs.reshape(BS, TS, D)
    return (y_prompt, y_sample, pool_p, pool_s, lru_conv_p, lru_conv_s, lru_h_p, lru_h_s,
            fox_k_p, fox_k_s, fox_v_p, fox_v_s, fox_lf_p, fox_lf_s,
            ml_c_p, ml_c_s, ml_n_p, ml_n_s, ml_m_p, ml_m_s)
```

```python
import functools
import math

import jax
import jax.numpy as jnp
from jax import lax
from jax.experimental import pallas as pl
from jax.experimental.pallas import tpu as pltpu

F32 = jnp.float32
BF = jnp.bfloat16

D = 2048
D_FF = 5632
EPS = 1e-6
NEG = -1e30
LOG2E = math.log2(math.e)

POOL_WINDOWS = (2, 4, 8, 16)
POOL_GW = D // len(POOL_WINDOWS)
POOL_PAD = 16
CONV_W = 4
CONV_PAD = 8
LRU_BLOCKS = 8
LRU_BW = D // LRU_BLOCKS
LRU_C = 8.0
FOX_HEADS = 16
FOX_HD = D // FOX_HEADS
ML_HEADS = 8
ML_DV = D // ML_HEADS
ML_DK = ML_DV // 2
ML_QK = ML_HEADS * ML_DK

V7X_VMEM_LIMIT = 60000 * 1024


def _params(semantics, vmem_bytes):
    return pltpu.CompilerParams(dimension_semantics=semantics,
                                vmem_limit_bytes=min(int(vmem_bytes), V7X_VMEM_LIMIT))


def _rms(xf, g):
    ms = jnp.mean(xf * xf, axis=-1, keepdims=True)
    return xf * lax.rsqrt(ms + EPS) * g


def _nt_dot(a, b):
    return lax.dot_general(a, b, (((1,), (1,)), ((), ())), preferred_element_type=F32)


def _log_sigmoid(z):
    return jnp.minimum(z, 0.0) - jnp.log1p(jnp.exp(-jnp.abs(z)))


def _split3_bf16(c):
    hi = c.astype(BF)
    r1 = c - hi.astype(F32)
    mid = r1.astype(BF)
    lo = (r1 - mid.astype(F32)).astype(BF)
    return hi, mid, lo


def _sum_dot_left(tri, x):
    return sum(jnp.dot(tri, part, preferred_element_type=F32) for part in _split3_bf16(x))


def _sum_dot_right(x, tri):
    return sum(jnp.dot(part, tri, preferred_element_type=F32) for part in _split3_bf16(x))


def _tri(n, kind, seg=None):
    a = lax.broadcasted_iota(jnp.int32, (n, n), 0)
    b = lax.broadcasted_iota(jnp.int32, (n, n), 1)
    if kind == "le":
        m = b <= a
        if seg is not None:
            m = jnp.logical_and(m, b >= jnp.bitwise_and(a, -seg))
    elif kind == "ge":
        m = b >= a
        if seg is not None:
            m = jnp.logical_and(m, a >= jnp.bitwise_and(b, -seg))
    else:
        m = a > b
    return jnp.where(m, 1.0, 0.0).astype(BF)


def _ffn_chunk(x_ref, g_ref, wg_ref, wu_ref, wd_ref, go_ref, o_ref, h_ref, *, nj, normalize_out):
    j = pl.program_id(1)

    @pl.when(j == 0)
    def _():
        xf = x_ref[...]
        h_ref[...] = _rms(xf, g_ref[...]).astype(BF)
        o_ref[...] = xf

    h = h_ref[...]
    gt = jnp.dot(h, wg_ref[...], preferred_element_type=F32)
    up = jnp.dot(h, wu_ref[...], preferred_element_type=F32)
    a = (gt * jax.nn.sigmoid(gt) * up).astype(BF)
    o_ref[...] += jnp.dot(a, wd_ref[...], preferred_element_type=F32)

    if normalize_out:
        @pl.when(j == nj - 1)
        def _():
            o_ref[...] = _rms(o_ref[...], go_ref[...])


def _ffn_body(x_ref, g_ref, wg_ref, wu_ref, wd_ref, go_ref, o_ref, h_ref, **kw):
    _ffn_chunk(x_ref, g_ref, wg_ref, wu_ref, wd_ref, go_ref, o_ref, h_ref, **kw)


def _ffn_cast_body(x_ref, g_ref, wg_ref, wu_ref, wd_ref, go_ref, o_ref, wgb_ref, wub_ref, wdb_ref, h_ref, **kw):
    wgb_ref[...] = wg_ref[...].astype(BF)
    wub_ref[...] = wu_ref[...].astype(BF)
    wdb_ref[...] = (0.5 * wd_ref[...]).astype(BF)
    _ffn_chunk(x_ref, g_ref, wgb_ref, wub_ref, wdb_ref, go_ref, o_ref, h_ref, **kw)


def _ffn_cast(x, g, wg, wu, wd, out_gain, *, layer, normalize_out, tf=512):
    tm = x.shape[0]
    nj = D_FF // tf
    vmem = 4 * tm * D * 4 + tm * D * 2 + 6 * D * tf * 4 + 9 * D * tf * 2 + 5 * tm * tf * 4
    wcol = lambda dt: (pl.BlockSpec((None, D, tf), lambda i, j: (layer, 0, j)) if dt == F32
                       else pl.BlockSpec((None, D, tf), lambda i, j: (0, 0, j)))
    return pl.pallas_call(
        functools.partial(_ffn_cast_body, nj=nj, normalize_out=normalize_out),
        grid=(1, nj),
        in_specs=[
            pl.BlockSpec((tm, D), lambda i, j: (0, 0)),
            pl.BlockSpec((1, D), lambda i, j: (0, 0)),
            wcol(F32), wcol(F32),
            pl.BlockSpec((None, tf, D), lambda i, j: (layer, j, 0)),
            pl.BlockSpec((1, D), lambda i, j: (0, 0)),
        ],
        out_specs=[
            pl.BlockSpec((tm, D), lambda i, j: (0, 0)),
            wcol(BF), wcol(BF),
            pl.BlockSpec((None, tf, D), lambda i, j: (0, j, 0)),
        ],
        out_shape=[
            jax.ShapeDtypeStruct((tm, D), F32),
            jax.ShapeDtypeStruct((1, D, D_FF), BF),
            jax.ShapeDtypeStruct((1, D, D_FF), BF),
            jax.ShapeDtypeStruct((1, D_FF, D), BF),
        ],
        scratch_shapes=[pltpu.VMEM((tm, D), BF)],
        compiler_params=_params(("arbitrary", "arbitrary"), vmem),
        name="ffn_cast",
    )(x, g, wg, wu, wd, out_gain)


def _ffn(x, g, wg, wu, wd_half, out_gain, *, layer, normalize_out, tm=1024, tf=512):
    R = x.shape[0]
    tm = min(tm, R)
    nj = D_FF // tf
    vmem = 4 * tm * D * 4 + tm * D * 2 + 6 * D * tf * 2 + 5 * tm * tf * 4
    return pl.pallas_call(
        functools.partial(_ffn_body, nj=nj, normalize_out=normalize_out),
        grid=(R // tm, nj),
        in_specs=[
            pl.BlockSpec((tm, D), lambda i, j: (i, 0)),
            pl.BlockSpec((1, D), lambda i, j: (0, 0)),
            pl.BlockSpec((None, D, tf), lambda i, j: (layer, 0, j)),
            pl.BlockSpec((None, D, tf), lambda i, j: (layer, 0, j)),
            pl.BlockSpec((None, tf, D), lambda i, j: (layer, j, 0)),
            pl.BlockSpec((1, D), lambda i, j: (0, 0)),
        ],
        out_specs=pl.BlockSpec((tm, D), lambda i, j: (i, 0)),
        out_shape=jax.ShapeDtypeStruct((R, D), F32),
        scratch_shapes=[pltpu.VMEM((tm, D), BF)],
        compiler_params=_params(("arbitrary", "arbitrary"), vmem),
        name="ffn",
    )(x, g, wg, wu, wd_half, out_gain)


def _mm_res_body(a_ref, w_ref, x_ref, o_ref):
    o_ref[...] = x_ref[...] + jnp.dot(a_ref[...], w_ref[...], preferred_element_type=F32)


def _mm_res(a, w, x, *, tm=1024):
    R = x.shape[0]
    tm = min(tm, R)
    vmem = 2 * tm * D * 2 + D * D * 2 + 5 * tm * D * 4
    return pl.pallas_call(
        _mm_res_body,
        grid=(R // tm,),
        in_specs=[
            pl.BlockSpec((tm, D), lambda i: (i, 0)),
            pl.BlockSpec((D, D), lambda i: (0, 0), pipeline_mode=pl.Buffered(1)),
            pl.BlockSpec((tm, D), lambda i: (i, 0)),
        ],
        out_specs=pl.BlockSpec((tm, D), lambda i: (i, 0)),
        out_shape=jax.ShapeDtypeStruct((R, D), F32),
        compiler_params=_params(("arbitrary",), vmem),
        name="mm_res",
    )(a, w, x)


def _pool_body(x_ref, g_ref, hist_ref, w_ref, sc_ref, o_ref, st_ref, xe_ref, *, tm, pos0):
    t = pl.program_id(1)

    @pl.when(t == 0)
    def _():
        xe_ref[0:POOL_PAD, :] = hist_ref[0]

    xf = x_ref[...]
    h = _rms(xf, g_ref[...])
    xe_ref[POOL_PAD:POOL_PAD + tm, :] = h
    row = lax.broadcasted_iota(jnp.int32, (tm, POOL_GW), 0)
    pos1 = (row + (pos0 + 1) + t * tm).astype(F32)
    for gi, w in enumerate(POOL_WINDOWS):
        c0 = gi * POOL_GW
        hg = h[:, c0:c0 + POOL_GW]
        s = hg
        for back in range(1, w):
            s = s + xe_ref[POOL_PAD - back:POOL_PAD - back + tm, c0:c0 + POOL_GW]
        cnt = jnp.minimum(pos1, float(w))
        pooled = s / cnt - hg
        y = jnp.dot(pooled.astype(BF), w_ref[gi], preferred_element_type=F32)
        o_ref[:, c0:c0 + POOL_GW] = xf[:, c0:c0 + POOL_GW] + y * sc_ref[:, c0:c0 + POOL_GW]
    tail = xe_ref[tm:tm + POOL_PAD, :]
    st_ref[0] = tail
    xe_ref[0:POOL_PAD, :] = tail


def _pool_mixer(x, g, hist, w, scale, *, B, T, tm, pos0):
    nt = T // tm
    vmem = 4 * tm * D * 4 + (tm + POOL_PAD) * D * 4 + 2 * 4 * POOL_GW * POOL_GW * 2 + 6 * tm * D * 4
    return pl.pallas_call(
        functools.partial(_pool_body, tm=tm, pos0=pos0),
        grid=(B, nt),
        in_specs=[
            pl.BlockSpec((tm, D), lambda b, t: (b * nt + t, 0)),
            pl.BlockSpec((1, D), lambda b, t: (0, 0)),
            pl.BlockSpec((1, POOL_PAD, D), lambda b, t: (b, 0, 0)),
            pl.BlockSpec((len(POOL_WINDOWS), POOL_GW, POOL_GW), lambda b, t: (0, 0, 0)),
            pl.BlockSpec((1, D), lambda b, t: (0, 0)),
        ],
        out_specs=[
            pl.BlockSpec((tm, D), lambda b, t: (b * nt + t, 0)),
            pl.BlockSpec((1, POOL_PAD, D), lambda b, t: (b, 0, 0)),
        ],
        out_shape=[jax.ShapeDtypeStruct((B * T, D), F32), jax.ShapeDtypeStruct((B, POOL_PAD, D), F32)],
        scratch_shapes=[pltpu.VMEM((tm + POOL_PAD, D), F32)],
        compiler_params=_params(("arbitrary", "arbitrary"), vmem),
        name="pool_mixer",
    )(x, g, hist, w, scale)


def _lru_tile(gate_ref, xr_ref, y_ref, cso_ref, ho_ref, cw_ref, cb_ref, wa_ref, ba_ref, wi_ref, bi_ref, lam_ref,
              xe_ref, a_ref, u_ref, hc_ref, *, tm):
    xe_ref[CONV_PAD:CONV_PAD + tm, :] = xr_ref[...]
    base = CONV_PAD - (CONV_W - 1)
    xc = cb_ref[...] + xe_ref[base:base + tm, :] * cw_ref[0:1, :]
    for j in range(1, CONV_W):
        xc = xc + xe_ref[base + j:base + j + tm, :] * cw_ref[j:j + 1, :]
    tail = xe_ref[tm:tm + CONV_PAD, :]
    cso_ref[0] = tail
    xe_ref[0:CONV_PAD, :] = tail

    xcb = xc.astype(BF)
    nlam = -lam_ref[...]
    sp = jnp.maximum(nlam, 0.0) + jnp.log1p(jnp.exp(-jnp.abs(nlam)))
    for n in range(LRU_BLOCKS):
        blk = slice(n * LRU_BW, (n + 1) * LRU_BW)
        ra = jnp.dot(xcb[:, blk], wa_ref[n], preferred_element_type=F32) + ba_ref[:, blk]
        ia = jnp.dot(xcb[:, blk], wi_ref[n], preferred_element_type=F32) + bi_ref[:, blk]
        ig = 0.5 * jnp.tanh(0.5 * ia) + 0.5
        log_a = (-0.5 * LRU_C * sp[:, blk]) * (jnp.tanh(0.5 * ra) + 1.0)
        a_ref[:, blk] = jnp.exp(log_a)
        th = jnp.tanh(log_a)
        one_minus_a2 = -2.0 * th / (1.0 - th)
        u_ref[:, blk] = jnp.sqrt(one_minus_a2) * ig * xc[:, blk]

    rowi = lax.broadcasted_iota(jnp.int32, (8, D), 0)

    def group(gi, carry):
        r0 = pl.multiple_of(gi * 8, 8)
        a8 = a_ref[pl.ds(r0, 8), :]
        u8 = u_ref[pl.ds(r0, 8), :]
        for d in (1, 2, 4):
            keep = rowi >= d
            u8 = jnp.where(keep, a8 * pltpu.roll(u8, d, axis=0) + u8, u8)
            a8 = jnp.where(keep, a8 * pltpu.roll(a8, d, axis=0), a8)
        hs8 = a8 * carry + u8
        u_ref[pl.ds(r0, 8), :] = hs8
        return hs8[7:8, :]

    carry = lax.fori_loop(0, tm // 8, group, hc_ref[...])
    hc_ref[...] = carry
    ho_ref[0] = carry
    y_ref[...] = (u_ref[...] * jax.nn.gelu(gate_ref[...])).astype(BF)


def _lru_body(gate_ref, xr_ref, cst_ref, h0_ref, cw_ref, cb_ref, wa_ref, ba_ref, wi_ref, bi_ref, lam_ref,
              y_ref, cso_ref, ho_ref, xe_ref, a_ref, u_ref, hc_ref, *, tm):
    @pl.when(pl.program_id(1) == 0)
    def _():
        xe_ref[0:CONV_PAD, :] = cst_ref[0]
        hc_ref[...] = h0_ref[0]

    _lru_tile(gate_ref, xr_ref, y_ref, cso_ref, ho_ref, cw_ref, cb_ref, wa_ref, ba_ref, wi_ref, bi_ref, lam_ref,
              xe_ref, a_ref, u_ref, hc_ref, tm=tm)


def _lru_core(gate, xr, cst, h0, cw, cb, wa, ba, wi, bi, lam, *, B, T, tm):
    nt = T // tm
    vmem = 4 * tm * D * 4 + 2 * tm * D * 2 + (3 * tm + CONV_PAD) * D * 4 + 4 * 8 * LRU_BW * LRU_BW * 2 + 6 * tm * D * 4
    vec = pl.BlockSpec((1, D), lambda b, t: (0, 0))
    wblk = pl.BlockSpec((LRU_BLOCKS, LRU_BW, LRU_BW), lambda b, t: (0, 0, 0))
    return pl.pallas_call(
        functools.partial(_lru_body, tm=tm),
        grid=(B, nt),
        in_specs=[
            pl.BlockSpec((tm, D), lambda b, t: (b * nt + t, 0)),
            pl.BlockSpec((tm, D), lambda b, t: (b * nt + t, 0)),
            pl.BlockSpec((1, CONV_PAD, D), lambda b, t: (b, 0, 0)),
            pl.BlockSpec((1, 1, D), lambda b, t: (b, 0, 0)),
            pl.BlockSpec((CONV_W, D), lambda b, t: (0, 0)),
            vec, wblk, vec, wblk, vec, vec,
        ],
        out_specs=[
            pl.BlockSpec((tm, D), lambda b, t: (b * nt + t, 0)),
            pl.BlockSpec((1, CONV_PAD, D), lambda b, t: (b, 0, 0)),
            pl.BlockSpec((1, 1, D), lambda b, t: (b, 0, 0)),
        ],
        out_shape=[
            jax.ShapeDtypeStruct((B * T, D), BF),
            jax.ShapeDtypeStruct((B, CONV_PAD, D), F32),
            jax.ShapeDtypeStruct((B, 1, D), F32),
        ],
        scratch_shapes=[
            pltpu.VMEM((tm + CONV_PAD, D), F32),
            pltpu.VMEM((tm, D), F32),
            pltpu.VMEM((tm, D), F32),
            pltpu.VMEM((1, D), F32),
        ],
        compiler_params=_params(("arbitrary", "arbitrary"), vmem),
        name="rglru_core",
    )(gate, xr, cst, h0, cw, cb, wa, ba, wi, bi, lam)


def _proj_call(body, x, g, w, jcol, extra, extra_specs, out_shape, out_specs, scratch, *, tm, vmem, name):
    R = x.shape[0]
    return pl.pallas_call(
        body,
        grid=(R // tm,),
        in_specs=[
            pl.BlockSpec((tm, D), lambda i: (i, 0)),
            pl.BlockSpec((1, D), lambda i: (0, 0)),
            pl.BlockSpec((D, D), lambda i: (0, jcol), pipeline_mode=pl.Buffered(1)),
            *extra_specs,
        ],
        out_specs=out_specs,
        out_shape=out_shape,
        scratch_shapes=scratch,
        compiler_params=_params(("arbitrary",), vmem),
        name=name,
    )(x, g, w, *extra)


def _normed_dot(x_ref, g_ref, w_ref):
    h = _rms(x_ref[...], g_ref[...]).astype(BF)
    return h, jnp.dot(h, w_ref[...], preferred_element_type=F32)


def _store_heads_interleaved(ref, y, tm):
    for hh in range(FOX_HEADS):
        ref[pl.ds(hh, tm, stride=FOX_HEADS), :] = y[:, hh * FOX_HD:(hh + 1) * FOX_HD]


def _fox_q_body(x_ref, g_ref, w_ref, q_ref, *, qscale):
    _, y = _normed_dot(x_ref, g_ref, w_ref)
    yq = (y * qscale).astype(BF)
    for hh in range(FOX_HEADS):
        q_ref[hh] = yq[:, hh * FOX_HD:(hh + 1) * FOX_HD]


def _fox_k_body(x_ref, g_ref, w_ref, wf_ref, wft_ref, bf_ref, bft_ref,
                k_ref, ka_ref, lf_ref, cc_ref, cr_ref, carc_ref, carr_ref, *, tm, seg, carry):
    h, y = _normed_dot(x_ref, g_ref, w_ref)
    lf = _log_sigmoid(jnp.dot(h, wf_ref[...], preferred_element_type=F32) + bf_ref[...])
    lft = _log_sigmoid(_nt_dot(wft_ref[...], h) + bft_ref[...])
    lf_ref[...] = lf
    cc = _sum_dot_left(_tri(tm, "le", seg), lf)
    cr = _sum_dot_right(lft, _tri(tm, "ge", seg))
    if carry:
        @pl.when(pl.program_id(0) == 0)
        def _():
            carc_ref[...] = jnp.zeros_like(carc_ref)
            carr_ref[...] = jnp.zeros_like(carr_ref)

        cc = cc + carc_ref[...]
        cr = cr + carr_ref[...]
        carc_ref[...] = cc[tm - 1:tm, :]
        carr_ref[...] = cr[:, tm - 1:tm]
    cc_ref[...] = cc
    cr_ref[...] = cr

    _store_heads_interleaved(k_ref, y, tm)
    yb = y.astype(BF)
    lane = lax.broadcasted_iota(jnp.int32, (tm, FOX_HD), 1)
    nck = cc * (-LOG2E)
    for hh in range(FOX_HEADS):
        hi, mid, lo = _split3_bf16(nck[:, hh:hh + 1])
        aug = jnp.where(lane == 0, hi.astype(F32),
                        jnp.where(lane == 1, mid.astype(F32), jnp.where(lane == 2, lo.astype(F32), 0.0)))
        ka_ref[hh, :, 0:FOX_HD] = yb[:, hh * FOX_HD:(hh + 1) * FOX_HD]
        ka_ref[hh, :, FOX_HD:2 * FOX_HD] = aug.astype(BF)


def _fox_v_body(x_ref, g_ref, w_ref, v_ref, vt_ref, *, tm):
    _, y = _normed_dot(x_ref, g_ref, w_ref)
    _store_heads_interleaved(v_ref, y, tm)
    for hh in range(FOX_HEADS):
        vt_ref[hh] = y[:, hh * FOX_HD:(hh + 1) * FOX_HD].T.astype(BF)


def _fox_proj(x, g, w, wf, wft, bf, bft, *, tm, seg, carry):
    R = x.shape[0]
    H = FOX_HEADS
    qscale = (FOX_HD ** -0.5) * LOG2E
    base = 2 * tm * D * 4 + D * D * 2 + 2 * tm * D * 4 + tm * D * 2
    const = lambda shape: pl.BlockSpec(shape, lambda i: (0,) * len(shape))
    native = pl.BlockSpec((tm * H, FOX_HD), lambda i: (i, 0))
    qh = _proj_call(
        functools.partial(_fox_q_body, qscale=qscale), x, g, w, 0, (), (),
        jax.ShapeDtypeStruct((H, R, FOX_HD), BF), pl.BlockSpec((H, tm, FOX_HD), lambda i: (0, i, 0)), (),
        tm=tm, vmem=base + 3 * tm * D * 2, name="fox_proj_q")
    k, ka, lf, cc, cr = _proj_call(
        functools.partial(_fox_k_body, tm=tm, seg=seg, carry=carry), x, g, w, 1,
        (wf, wft, bf, bft), (const((D, H)), const((H, D)), const((1, H)), const((H, 1))),
        [
            jax.ShapeDtypeStruct((R * H, FOX_HD), F32),
            jax.ShapeDtypeStruct((H, R, 2 * FOX_HD), BF),
            jax.ShapeDtypeStruct((R, H), F32),
            jax.ShapeDtypeStruct((R, H), F32),
            jax.ShapeDtypeStruct((H, R), F32),
        ],
        [
            native,
            pl.BlockSpec((H, tm, 2 * FOX_HD), lambda i: (0, i, 0)),
            pl.BlockSpec((tm, H), lambda i: (i, 0)),
            pl.BlockSpec((tm, H), lambda i: (i, 0)),
            pl.BlockSpec((H, tm), lambda i: (0, i)),
        ],
        [pltpu.VMEM((1, H), F32), pltpu.VMEM((H, 1), F32)],
        tm=tm, vmem=base + 2 * tm * D * 4 + 5 * tm * D * 2 + 3 * tm * tm * 4 + 4 * tm * 128 * 4, name="fox_proj_k")
    v, vt = _proj_call(
        functools.partial(_fox_v_body, tm=tm), x, g, w, 2, (), (),
        [jax.ShapeDtypeStruct((R * H, FOX_HD), F32), jax.ShapeDtypeStruct((H, FOX_HD, R), BF)],
        [native, pl.BlockSpec((H, FOX_HD, tm), lambda i: (0, 0, i))], (),
        tm=tm, vmem=base + 2 * tm * D * 4 + 3 * tm * D * 2 + tm * D * 4, name="fox_proj_v")
    return qh, k, v, ka, vt, lf, cc, cr


def _softmax_step(s_biased, cq, pv, m_ref, l_ref, acc_ref):
    m_old = m_ref[...]
    m_new = jnp.maximum(m_old, jnp.max(s_biased, axis=-1, keepdims=True) + cq)
    p = jnp.exp2(s_biased - (m_new - cq))
    alpha = jnp.exp2(m_old - m_new)
    l_ref[...] = alpha * l_ref[...] + jnp.sum(p, axis=-1, keepdims=True)
    acc_ref[...] = alpha * acc_ref[...] + pv(p.astype(BF))
    m_ref[...] = m_new


def _fox_attn_body(q_ref, ka_ref, vt_ref, cr_ref, o_ref, qt_ref, m_ref, l_ref, acc_ref, sa_ref, sb_ref, *, tq, tk):
    h = pl.program_id(0)
    qi = pl.program_id(1)
    q0 = pl.multiple_of(qi * tq, tq)
    qt_ref[0:FOX_HD, :] = q_ref[0].astype(F32).T.astype(BF)
    sub = lax.broadcasted_iota(jnp.int32, (FOX_HD, tq), 0)
    qt_ref[FOX_HD:2 * FOX_HD, :] = jnp.where(sub < 3, 1.0, 0.0).astype(BF)
    cq = cr_ref[pl.ds(h, 1), pl.ds(q0, tq)] * LOG2E
    m_ref[...] = jnp.full_like(m_ref, NEG)
    l_ref[...] = jnp.zeros_like(l_ref)
    acc_ref[...] = jnp.zeros_like(acc_ref)

    def scores(t, st_ref):
        k0 = pl.multiple_of(t * tk, tk)
        st_ref[...] = jnp.dot(ka_ref[0, pl.ds(k0, tk), :], qt_ref[...], preferred_element_type=F32)

    def fold(t, st_ref, diag=None):
        k0 = pl.multiple_of(t * tk, tk)
        st = st_ref[...]
        if diag is not None:
            kj = lax.broadcasted_iota(jnp.int32, (tk, tq), 0) + diag * tk
            qcol = lax.broadcasted_iota(jnp.int32, (tk, tq), 1)
            st = jnp.where(kj <= qcol, st, NEG)
        m_old = m_ref[...]
        m_new = jnp.maximum(m_old, jnp.max(st, axis=0, keepdims=True) + cq)
        p = jnp.exp2(st - (m_new - cq))
        alpha = jnp.exp2(m_old - m_new)
        l_ref[...] = alpha * l_ref[...] + jnp.sum(p, axis=0, keepdims=True)
        acc_ref[...] = alpha * acc_ref[...] + jnp.dot(
            vt_ref[0, :, pl.ds(k0, tk)], p.astype(BF), preferred_element_type=F32)
        m_ref[...] = m_new

    scores(0, sa_ref)

    def quad(u, _):
        t = 4 * u
        scores(t + 1, sb_ref)
        fold(t, sa_ref)
        scores(t + 2, sa_ref)
        fold(t + 1, sb_ref)
        scores(t + 3, sb_ref)
        fold(t + 2, sa_ref)
        scores(t + 4, sa_ref)
        fold(t + 3, sb_ref)
        return 0

    lax.fori_loop(0, qi // 2, quad, 0)
    td = 2 * qi

    @pl.when(qi % 2 == 1)
    def _():
        scores(td - 1, sb_ref)
        fold(td - 2, sa_ref)
        scores(td, sa_ref)
        fold(td - 1, sb_ref)

    scores(td + 1, sb_ref)
    fold(td, sa_ref, diag=0)
    fold(td + 1, sb_ref, diag=1)
    o_ref[...] = (acc_ref[...] / l_ref[...]).T.astype(BF)


def _fox_attn(qh, ka, vt, cr, *, T, tq=1024):
    H = FOX_HEADS
    tk = tq // 2
    vmem = 2 * T * 2 * FOX_HD * 2 + 2 * T * FOX_HD * 2 + 2 * H * T * 4 + 8 * tk * tq * 4 + 16 * tq * FOX_HD * 4
    return pl.pallas_call(
        functools.partial(_fox_attn_body, tq=tq, tk=tk),
        grid=(H, T // tq),
        in_specs=[
            pl.BlockSpec((1, tq, FOX_HD), lambda h, i: (h, i, 0)),
            pl.BlockSpec((1, T, 2 * FOX_HD), lambda h, i: (h, 0, 0)),
            pl.BlockSpec((1, FOX_HD, T), lambda h, i: (h, 0, 0)),
            pl.BlockSpec((H, T), lambda h, i: (0, 0)),
        ],
        out_specs=pl.BlockSpec((tq, FOX_HD), lambda h, i: (i, h)),
        out_shape=jax.ShapeDtypeStruct((T, D), BF),
        scratch_shapes=[pltpu.VMEM((2 * FOX_HD, tq), BF), pltpu.VMEM((1, tq), F32), pltpu.VMEM((1, tq), F32),
                        pltpu.VMEM((FOX_HD, tq), F32), pltpu.VMEM((tk, tq), F32), pltpu.VMEM((tk, tq), F32)],
        compiler_params=_params(("arbitrary", "arbitrary"), vmem),
        name="fox_attention",
    )(qh, ka, vt, cr)


def _suffix_body(x_ref, o_ref, car_ref, *, tc):
    @pl.when(pl.program_id(1) == 0)
    def _():
        car_ref[...] = jnp.zeros_like(car_ref)

    x = x_ref[0]
    o_ref[0] = _sum_dot_right(x, _tri(tc, "gt")) + car_ref[...]
    car_ref[...] = car_ref[...] + jnp.sum(x, axis=1, keepdims=True)


def _suffix_sum(x, *, tc=512):
    B, H, S = x.shape
    nc = S // tc
    return pl.pallas_call(
        functools.partial(_suffix_body, tc=tc),
        grid=(B, nc),
        in_specs=[pl.BlockSpec((1, H, tc), lambda b, t: (b, 0, nc - 1 - t))],
        out_specs=pl.BlockSpec((1, H, tc), lambda b, t: (b, 0, nc - 1 - t)),
        out_shape=jax.ShapeDtypeStruct((B, H, S), F32),
        scratch_shapes=[pltpu.VMEM((H, 1), F32)],
        compiler_params=_params(("arbitrary", "arbitrary"), 8 * tc * tc * 4),
        name="suffix_sum",
    )(x)


def _fox_decode_body(q_ref, ka_ref, vt_ref, kc_ref, vc_ref, dsuf_ref, ec_ref, er_ref, o_ref,
                     m_ref, l_ref, acc_ref, *, nt, tq, ts):
    b = pl.program_id(0)
    t = pl.program_id(1)
    H = FOX_HEADS

    @pl.when(t == 0)
    def _():
        m_ref[...] = jnp.full_like(m_ref, NEG)
        l_ref[...] = jnp.zeros_like(l_ref)
        acc_ref[...] = jnp.zeros_like(acc_ref)

    ec = ec_ref[...] * LOG2E
    dsuf = dsuf_ref[0] * LOG2E
    s_all = [_nt_dot(q_ref[hh], kc_ref[0, pl.ds(hh, ts, stride=H), :].astype(BF)) + dsuf[hh:hh + 1, :]
             for hh in range(H)]
    for hh in range(H):
        vt = vc_ref[0, pl.ds(hh, ts, stride=H), :].astype(BF)
        _softmax_step(s_all[hh], ec[:, hh:hh + 1], lambda p, vt=vt: jnp.dot(p, vt, preferred_element_type=F32),
                      m_ref.at[hh], l_ref.at[hh], acc_ref.at[hh])

    @pl.when(t == nt - 1)
    def _():
        n_new = er_ref.shape[1]
        er = er_ref[...] * LOG2E
        r = lax.broadcasted_iota(jnp.int32, (tq, n_new), 0)
        c = lax.broadcasted_iota(jnp.int32, (tq, n_new), 1) - b * tq
        valid = jnp.logical_and(c >= 0, c <= r)
        for hh in range(H):
            s = _nt_dot(q_ref[hh], ka_ref[hh][:, 0:FOX_HD]) - er[hh:hh + 1, :]
            s = jnp.where(valid, s, NEG)
            _softmax_step(s, ec[:, hh:hh + 1], lambda p, hh=hh: _nt_dot(p, vt_ref[hh]),
                          m_ref.at[hh], l_ref.at[hh], acc_ref.at[hh])
            o_ref[:, hh * FOX_HD:(hh + 1) * FOX_HD] = (acc_ref[hh] / l_ref[hh]).astype(BF)


def _fox_decode(qh, ka, vt, kc, vc, dsuf, ec, er, *, B, tq, ts=512):
    H = FOX_HEADS
    S = kc.shape[1] // H
    R = B * tq
    nt = S // ts
    vmem = (4 * ts * D * 4 + 2 * H * R * 3 * FOX_HD * 2 + 2 * H * tq * FOX_HD * 2 + 3 * H * tq * 128 * 4
            + 4 * ts * FOX_HD * 2 + 8 * tq * max(ts, R) * 4)
    return pl.pallas_call(
        functools.partial(_fox_decode_body, nt=nt, tq=tq, ts=ts),
        grid=(B, nt),
        in_specs=[
            pl.BlockSpec((H, tq, FOX_HD), lambda b, t: (0, b, 0)),
            pl.BlockSpec((H, R, 2 * FOX_HD), lambda b, t: (0, 0, 0)),
            pl.BlockSpec((H, FOX_HD, R), lambda b, t: (0, 0, 0)),
            pl.BlockSpec((1, ts * H, FOX_HD), lambda b, t: (b, t, 0)),
            pl.BlockSpec((1, ts * H, FOX_HD), lambda b, t: (b, t, 0)),
            pl.BlockSpec((1, H, ts), lambda b, t: (b, 0, t)),
            pl.BlockSpec((tq, H), lambda b, t: (b, 0)),
            pl.BlockSpec((H, R), lambda b, t: (0, 0)),
        ],
        out_specs=pl.BlockSpec((tq, D), lambda b, t: (b, 0)),
        out_shape=jax.ShapeDtypeStruct((R, D), BF),
        scratch_shapes=[pltpu.VMEM((H, tq, 1), F32), pltpu.VMEM((H, tq, 1), F32), pltpu.VMEM((H, tq, FOX_HD), F32)],
        compiler_params=_params(("arbitrary", "arbitrary"), vmem),
        name="fox_decode_attention",
    )(qh, ka, vt, kc, vc, dsuf, ec, er)


def _mlstm_qk_body(x_ref, g_ref, w_ref, wg_ref, wgt_ref, bg_ref, bgt_ref, q_ref, k_ref, gc_ref, gr_ref, *, qscale):
    h, y = _normed_dot(x_ref, g_ref, w_ref)
    zc = jnp.dot(h, wg_ref[...], preferred_element_type=F32) + bg_ref[...]
    lane = lax.broadcasted_iota(jnp.int32, zc.shape, 1)
    gc_ref[...] = jnp.where(lane < ML_HEADS, zc, _log_sigmoid(zc))
    zr = _nt_dot(wgt_ref[...], h) + bgt_ref[...]
    sub = lax.broadcasted_iota(jnp.int32, zr.shape, 0)
    gr_ref[...] = jnp.where(sub < ML_HEADS, zr, _log_sigmoid(zr))
    q_ref[...] = (y[:, :ML_QK] * qscale).astype(BF)
    k_ref[...] = y[:, ML_QK:]


def _proj_bf16_body(x_ref, g_ref, w_ref, o_ref):
    o_ref[...] = _normed_dot(x_ref, g_ref, w_ref)[1].astype(BF)


def _proj_f32_body(x_ref, g_ref, w_ref, o_ref):
    o_ref[...] = _normed_dot(x_ref, g_ref, w_ref)[1]


def _proj_plain(x, g, w, jcol, dtype, *, tm=1024, name):
    R = x.shape[0]
    tm = min(tm, R)
    body = _proj_bf16_body if dtype == BF else _proj_f32_body
    vmem = 2 * tm * D * 4 + D * D * 2 + 2 * tm * D * 4 + tm * D * 2 + 2 * tm * D * 4
    return _proj_call(body, x, g, w, jcol, (), (), jax.ShapeDtypeStruct((R, D), dtype),
                      pl.BlockSpec((tm, D), lambda i: (i, 0)), (), tm=tm, vmem=vmem, name=name)


def _mlstm_proj(x, g, w, wg, wgt, bg, bgt, *, tm=512):
    R = x.shape[0]
    G = 2 * ML_HEADS
    const = lambda shape: pl.BlockSpec(shape, lambda i: (0,) * len(shape))
    vmem = 2 * tm * D * 4 + D * D * 2 + 2 * tm * D * 4 + tm * D * 2 + 2 * tm * D * 4
    q, k, gc, gr = _proj_call(
        functools.partial(_mlstm_qk_body, qscale=ML_DK ** -0.5), x, g, w, 0,
        (wg, wgt, bg, bgt), (const((D, G)), const((G, D)), const((1, G)), const((G, 1))),
        [
            jax.ShapeDtypeStruct((R, ML_QK), BF),
            jax.ShapeDtypeStruct((R, ML_QK), F32),
            jax.ShapeDtypeStruct((R, G), F32),
            jax.ShapeDtypeStruct((G, R), F32),
        ],
        [
            pl.BlockSpec((tm, ML_QK), lambda i: (i, 0)),
            pl.BlockSpec((tm, ML_QK), lambda i: (i, 0)),
            pl.BlockSpec((tm, G), lambda i: (i, 0)),
            pl.BlockSpec((G, tm), lambda i: (0, i)),
        ],
        (), tm=tm, vmem=vmem, name="mlstm_proj_qk")
    v = _proj_plain(x, g, w, 1, BF, tm=tm, name="mlstm_proj_v")
    o = _proj_plain(x, g, w, 2, F32, tm=tm, name="mlstm_proj_o")
    return q, k, v, o, gc, gr


def _mlstm_body(q_ref, k_ref, v_ref, o_ref, gc_ref, gr_ref, gn_ref, c0_ref, n0_ref, m0_ref,
                y_ref, co_ref, no_ref, mo_ref, c_ref, n_ref, m_ref, *, L):
    t = pl.program_id(1)

    @pl.when(t == 0)
    def _():
        c_ref[...] = c0_ref[0]
        n_ref[...] = n0_ref[0]
        m_ref[...] = m0_ref[0]

    gc = gc_ref[...]
    gr = gr_ref[0]
    ra = lax.broadcasted_iota(jnp.int32, (L, L), 0)
    cb = lax.broadcasted_iota(jnp.int32, (L, L), 1)
    causal = cb <= ra
    bcs = _sum_dot_left(_tri(L, "le"), gc)
    brs = _sum_dot_right(gr, _tri(L, "ge"))
    for hh in range(ML_HEADS):
        b_c = bcs[:, ML_HEADS + hh:ML_HEADS + hh + 1]
        g_c = gc[:, hh:hh + 1] - b_c
        g_r = gr[hh:hh + 1, :] - brs[ML_HEADS + hh:ML_HEADS + hh + 1, :]
        m_h = m_ref[hh][:, 0:1]
        am = jnp.where(causal, b_c + g_r, NEG)
        mt = jnp.maximum(b_c + m_h, jnp.max(am, axis=-1, keepdims=True))
        d = jnp.exp(am - mt)
        qh = q_ref[:, hh * ML_DK:(hh + 1) * ML_DK]
        kf = k_ref[:, hh * ML_DK:(hh + 1) * ML_DK]
        vh = v_ref[:, hh * ML_DV:(hh + 1) * ML_DV]
        sc = _nt_dot(qh, kf.astype(BF)) * d
        inter = jnp.exp(b_c + m_h - mt)
        c_old = c_ref[hh]
        n_old = n_ref[hh]
        num = (jnp.dot(sc.astype(BF), vh, preferred_element_type=F32)
               + inter * jnp.dot(qh, c_old.astype(BF), preferred_element_type=F32))
        den = (jnp.sum(sc, axis=-1, keepdims=True)
               + inter * jnp.sum(qh.astype(F32) * n_old, axis=-1, keepdims=True))
        hout = num / jnp.maximum(jnp.abs(den), jnp.exp(-mt))
        m_new = mt[L - 1:L, :]
        b_last = b_c[L - 1:L, :]
        decay = jnp.exp(b_last + m_h - m_new)
        wk = jnp.exp(b_last + g_c - m_new) * kf
        c_ref[hh] = decay * c_old + lax.dot_general(
            wk.astype(BF), vh, (((0,), (0,)), ((), ())), preferred_element_type=F32)
        n_ref[hh] = decay * n_old + jnp.sum(wk, axis=0, keepdims=True)
        m_ref[hh] = jnp.broadcast_to(m_new, (1, 128))
        hn = hout * lax.rsqrt(jnp.mean(hout * hout, axis=-1, keepdims=True) + EPS)
        cols = slice(hh * ML_DV, (hh + 1) * ML_DV)
        y_ref[:, cols] = (hn * gn_ref[:, cols] * jax.nn.sigmoid(o_ref[:, cols])).astype(BF)

    co_ref[0] = c_ref[...]
    no_ref[0] = n_ref[...]
    mo_ref[0] = m_ref[...]


def _mlstm_core(q, k, v, o, gc, gr, gn, c0, n0, m0, *, B, T, L):
    nt = T // L
    G = 2 * ML_HEADS
    H = ML_HEADS
    vmem = (2 * L * (ML_QK * 6 + D * 8) + 6 * H * ML_DK * ML_DV * 4 + 16 * L * L * 4 + 12 * L * ML_DV * 4
            + 4 * 1024 * 1024)
    rows = lambda w: pl.BlockSpec((L, w), lambda b, t: (b * nt + t, 0))
    st_c = pl.BlockSpec((1, H, ML_DK, ML_DV), lambda b, t: (b, 0, 0, 0))
    st_n = pl.BlockSpec((1, H, 1, ML_DK), lambda b, t: (b, 0, 0, 0))
    st_m = pl.BlockSpec((1, H, 1, 128), lambda b, t: (b, 0, 0, 0))
    return pl.pallas_call(
        functools.partial(_mlstm_body, L=L),
        grid=(B, nt),
        in_specs=[
            rows(ML_QK), rows(ML_QK), rows(D), rows(D), rows(G),
            pl.BlockSpec((1, G, L), lambda b, t: (b, 0, t)),
            pl.BlockSpec((1, D), lambda b, t: (0, 0)),
            st_c, st_n, st_m,
        ],
        out_specs=[rows(D), st_c, st_n, st_m],
        out_shape=[
            jax.ShapeDtypeStruct((B * T, D), BF),
            jax.ShapeDtypeStruct((B, H, ML_DK, ML_DV), F32),
            jax.ShapeDtypeStruct((B, H, 1, ML_DK), F32),
            jax.ShapeDtypeStruct((B, H, 1, 128), F32),
        ],
        scratch_shapes=[
            pltpu.VMEM((H, ML_DK, ML_DV), F32),
            pltpu.VMEM((H, 1, ML_DK), F32),
            pltpu.VMEM((H, 1, 128), F32),
        ],
        compiler_params=_params(("arbitrary", "arbitrary"), vmem),
        name="mlstm_core",
    )(q, k, v, o, gc, gr, gn, c0, n0, m0)


def kernel(x_prompt, x_sample, state_pool, state_lru_conv, state_lru_h, cache_fox_k, cache_fox_v, cache_fox_logf, state_mlstm_c, state_mlstm_n, state_mlstm_m, ffn1_norm, ffn1_w_gate, ffn1_w_up, ffn1_w_down, mix_norm, ffn2_norm, ffn2_w_gate, ffn2_w_up, ffn2_w_down, pool_w, pool_scale, lru_w_in, lru_conv_w, lru_conv_b, lru_w_a, lru_b_a, lru_w_i, lru_b_i, lru_lambda, lru_w_out, fox_w_qkv, fox_w_f, fox_b_f, fox_w_o, mlstm_w_in, mlstm_b_i, mlstm_b_f, mlstm_norm, mlstm_w_out, final_norm):
    BP, TP, _ = x_prompt.shape
    BS, TS, _ = x_sample.shape
    assert BP == 1 and x_prompt.shape[2] == D and x_sample.shape[2] == D
    RS = BS * TS
    past = cache_fox_k.shape[1]
    row = lambda p: p.reshape(1, -1).astype(F32)

    xp = x_prompt.reshape(TP, D)
    xs = x_sample.reshape(RS, D)

    ffn_w = {
        1: (ffn1_norm, ffn1_w_gate, ffn1_w_up, ffn1_w_down),
        2: (ffn2_norm, ffn2_w_gate, ffn2_w_up, ffn2_w_down),
    }

    def ffn(xp, xs, which, layer, last=False):
        norm, wg, wu, wd = ffn_w[which]
        g, gf = row(norm[layer]), row(final_norm)
        xs, wgb, wub, wdb = _ffn_cast(xs, g, wg.astype(F32), wu.astype(F32), wd.astype(F32), gf,
                                      layer=layer, normalize_out=last)
        return _ffn(xp, g, wgb, wub, wdb, gf, layer=0, normalize_out=last), xs

    xp, xs = ffn(xp, xs, 1, 0)
    pw = pool_w.astype(BF)
    hist_s = jnp.pad(state_pool.astype(F32), ((0, 0), (POOL_PAD - state_pool.shape[1], 0), (0, 0)))
    xp, pool_p = _pool_mixer(xp, row(mix_norm[0]), jnp.zeros((BP, POOL_PAD, D), F32), pw, row(pool_scale),
                             B=BP, T=TP, tm=512, pos0=0)
    xs, pool_s = _pool_mixer(xs, row(mix_norm[0]), hist_s, pw, row(pool_scale), B=BS, T=TS, tm=TS, pos0=past)
    pool_p = pool_p[:, 1:]
    pool_s = pool_s[:, 1:]
    xp, xs = ffn(xp, xs, 2, 0)

    xp, xs = ffn(xp, xs, 1, 1)
    w_in = lru_w_in.astype(BF)
    lru_args = (lru_conv_w.astype(F32), row(lru_conv_b), lru_w_a.astype(BF), row(lru_b_a),
                lru_w_i.astype(BF), row(lru_b_i), row(lru_lambda))
    lru_in = lambda x, jcol: _proj_plain(x, row(mix_norm[1]), w_in, jcol, F32, name="rglru_proj")
    cst_s = jnp.pad(state_lru_conv.astype(F32), ((0, 0), (CONV_PAD - (CONV_W - 1), 0), (0, 0)))
    yp, conv_p, h_p = _lru_core(lru_in(xp, 0), lru_in(xp, 1), jnp.zeros((BP, CONV_PAD, D), F32),
                                jnp.zeros((BP, 1, D), F32), *lru_args, B=BP, T=TP, tm=256)
    ys, conv_s, h_s = _lru_core(lru_in(xs, 0), lru_in(xs, 1), cst_s, state_lru_h.astype(F32).reshape(BS, 1, D),
                                *lru_args, B=BS, T=TS, tm=TS)
    w_out = lru_w_out.astype(BF)
    xp = _mm_res(yp, w_out, xp)
    xs = _mm_res(ys, w_out, xs)
    lru_conv_p, lru_conv_s = conv_p[:, CONV_PAD - (CONV_W - 1):], conv_s[:, CONV_PAD - (CONV_W - 1):]
    lru_h_p, lru_h_s = h_p.reshape(BP, D), h_s.reshape(BS, D)
    xp, xs = ffn(xp, xs, 2, 1)

    xp, xs = ffn(xp, xs, 1, 2)
    w_qkv = fox_w_qkv.astype(BF)
    wf = fox_w_f.astype(BF)
    wft = wf.T
    bf_r = fox_b_f.astype(F32).reshape(1, FOX_HEADS)
    bf_c = fox_b_f.astype(F32).reshape(FOX_HEADS, 1)
    qh_p, k_p, v_p, ka_p, vt_p, lf_p, cc_p, cr_p = _fox_proj(
        xp, row(mix_norm[2]), w_qkv, wf, wft, bf_r, bf_c, tm=512, seg=512, carry=True)
    qh_s, k_s, v_s, ka_s, vt_s, lf_s, cc_s, cr_s = _fox_proj(
        xs, row(mix_norm[2]), w_qkv, wf, wft, bf_r, bf_c, tm=RS, seg=TS, carry=False)
    o_p = _fox_attn(qh_p, ka_p, vt_p, cr_p, T=TP)
    dsuf = _suffix_sum(jnp.swapaxes(cache_fox_logf.astype(F32), 1, 2))
    o_s = _fox_decode(qh_s, ka_s, vt_s,
                      cache_fox_k.reshape(BS, past * FOX_HEADS, FOX_HD),
                      cache_fox_v.reshape(BS, past * FOX_HEADS, FOX_HD),
                      dsuf, cc_s, cr_s, B=BS, tq=TS)
    w_o = fox_w_o.astype(BF)
    xp = _mm_res(o_p, w_o, xp)
    xs = _mm_res(o_s, w_o, xs)
    fox_k_p = k_p.reshape(BP, TP, FOX_HEADS, FOX_HD)
    fox_v_p = v_p.reshape(BP, TP, FOX_HEADS, FOX_HD)
    fox_k_s = k_s.reshape(BS, TS, FOX_HEADS, FOX_HD)
    fox_v_s = v_s.reshape(BS, TS, FOX_HEADS, FOX_HD)
    fox_lf_p = lf_p.reshape(BP, TP, FOX_HEADS)
    fox_lf_s = lf_s.reshape(BS, TS, FOX_HEADS)
    xp, xs = ffn(xp, xs, 2, 2)

    xp, xs = ffn(xp, xs, 1, 3)
    n_main = 2 * ML_QK + 2 * D
    w_main = mlstm_w_in[:, :n_main].astype(BF)
    w_gates = mlstm_w_in[:, n_main:].astype(BF)
    b_gates = jnp.concatenate([mlstm_b_i, mlstm_b_f]).astype(F32)
    ml_w = (row(mix_norm[3]), w_main, w_gates, w_gates.T, b_gates.reshape(1, -1), b_gates.reshape(-1, 1))
    q_p, kk_p, vv_p, og_p, gc_p, gr_p = _mlstm_proj(xp, *ml_w)
    q_s, kk_s, vv_s, og_s, gc_s, gr_s = _mlstm_proj(xs, *ml_w)
    G = 2 * ML_HEADS
    gn = row(mlstm_norm)
    yp, ml_c_p, ml_n_p, ml_m_p = _mlstm_core(
        q_p, kk_p, vv_p, og_p, gc_p, gr_p.reshape(1, G, TP), gn,
        jnp.zeros((BP, ML_HEADS, ML_DK, ML_DV), F32), jnp.zeros((BP, ML_HEADS, 1, ML_DK), F32),
        jnp.zeros((BP, ML_HEADS, 1, 128), F32), B=BP, T=TP, L=256)
    m0_s = jnp.broadcast_to(state_mlstm_m.astype(F32)[:, :, None, None], (BS, ML_HEADS, 1, 128))
    ys, ml_c_s, ml_n_s, ml_m_s = _mlstm_core(
        q_s, kk_s, vv_s, og_s, gc_s, jnp.swapaxes(gr_s.reshape(G, BS, TS), 0, 1), gn,
        state_mlstm_c.astype(F32), state_mlstm_n.astype(F32).reshape(BS, ML_HEADS, 1, ML_DK), m0_s,
        B=BS, T=TS, L=TS)
    w_out = mlstm_w_out.astype(BF)
    xp = _mm_res(yp, w_out, xp)
    xs = _mm_res(ys, w_out, xs)
    ml_n_p, ml_n_s = ml_n_p.reshape(BP, ML_HEADS, ML_DK), ml_n_s.reshape(BS, ML_HEADS, ML_DK)
    ml_m_p, ml_m_s = ml_m_p[:, :, 0, 0], ml_m_s[:, :, 0, 0]
    xp, xs = ffn(xp, xs, 2, 3, last=True)

    y_prompt = xp.reshape(BP, TP, D)
    y_sample = xs.reshape(BS, TS, D)
    return (y_prompt, y_sample, pool_p, pool_s, lru_conv_p, lru_conv_s, lru_h_p, lru_h_s,
            fox_k_p, fox_k_s, fox_v_p, fox_v_s, fox_lf_p, fox_lf_s,
            ml_c_p, ml_c_s, ml_n_p, ml_n_s, ml_m_p, ml_m_s)
```

```python
import functools
import math

import jax
import jax.numpy as jnp
from jax import lax
from jax.experimental import pallas as pl
from jax.experimental.pallas import tpu as pltpu

F32 = jnp.float32
BF = jnp.bfloat16

D = 2048
D_FF = 5632
EPS = 1e-6
NEG = -1e30
LOG2E = math.log2(math.e)

POOL_WINDOWS = (2, 4, 8, 16)
POOL_GW = D // len(POOL_WINDOWS)
POOL_PAD = 16
POOL_LEAD = 8
CONV_W = 4
CONV_PAD = 8
LRU_BLOCKS = 8
LRU_BW = D // LRU_BLOCKS
LRU_C = 8.0
FOX_HEADS = 16
FOX_HD = D // FOX_HEADS
ML_HEADS = 8
ML_DV = D // ML_HEADS
ML_DK = ML_DV // 2
ML_QK = ML_HEADS * ML_DK

V7X_VMEM_LIMIT = 60000 * 1024


def _params(semantics, vmem_bytes):
    return pltpu.CompilerParams(dimension_semantics=semantics,
                                vmem_limit_bytes=min(int(vmem_bytes), V7X_VMEM_LIMIT))


def _rms(xf, g):
    ms = jnp.mean(xf * xf, axis=-1, keepdims=True)
    return xf * lax.rsqrt(ms + EPS) * g


def _nt_dot(a, b):
    return lax.dot_general(a, b, (((1,), (1,)), ((), ())), preferred_element_type=F32)


def _log_sigmoid(z):
    return jnp.minimum(z, 0.0) - jnp.log1p(jnp.exp(-jnp.abs(z)))


def _split3_bf16(c):
    hi = c.astype(BF)
    r1 = c - hi.astype(F32)
    mid = r1.astype(BF)
    lo = (r1 - mid.astype(F32)).astype(BF)
    return hi, mid, lo


def _sum_dot_left(tri, x):
    return sum(jnp.dot(tri, part, preferred_element_type=F32) for part in _split3_bf16(x))


def _sum_dot_right(x, tri):
    return sum(jnp.dot(part, tri, preferred_element_type=F32) for part in _split3_bf16(x))


def _tri(n, kind, seg=None):
    a = lax.broadcasted_iota(jnp.int32, (n, n), 0)
    b = lax.broadcasted_iota(jnp.int32, (n, n), 1)
    if kind == "le":
        m = b <= a
        if seg is not None:
            m = jnp.logical_and(m, b >= jnp.bitwise_and(a, -seg))
    elif kind == "ge":
        m = b >= a
        if seg is not None:
            m = jnp.logical_and(m, a >= jnp.bitwise_and(b, -seg))
    else:
        m = a > b
    return jnp.where(m, 1.0, 0.0).astype(BF)


def _ffn_chunk(x_ref, g_ref, wg_ref, wu_ref, wd_ref, go_ref, o_ref, h_ref, *, nj, normalize_out):
    j = pl.program_id(1)

    @pl.when(j == 0)
    def _():
        xf = x_ref[...]
        h_ref[...] = _rms(xf, g_ref[...]).astype(BF)
        o_ref[...] = xf

    h = h_ref[...]
    gt = jnp.dot(h, wg_ref[...], preferred_element_type=F32)
    up = jnp.dot(h, wu_ref[...], preferred_element_type=F32)
    a = (gt * jax.nn.sigmoid(gt) * up).astype(BF)
    o_ref[...] += jnp.dot(a, wd_ref[...], preferred_element_type=F32)

    if normalize_out:
        @pl.when(j == nj - 1)
        def _():
            o_ref[...] = _rms(o_ref[...], go_ref[...])


def _ffn_body(x_ref, g_ref, wg_ref, wu_ref, wd_ref, go_ref, o_ref, h_ref, **kw):
    _ffn_chunk(x_ref, g_ref, wg_ref, wu_ref, wd_ref, go_ref, o_ref, h_ref, **kw)


def _ffn_cast_body(x_ref, g_ref, wg_ref, wu_ref, wd_ref, go_ref, o_ref, wgb_ref, wub_ref, wdb_ref, h_ref, **kw):
    wgb_ref[...] = wg_ref[...].astype(BF)
    wub_ref[...] = wu_ref[...].astype(BF)
    wdb_ref[...] = (0.5 * wd_ref[...]).astype(BF)
    _ffn_chunk(x_ref, g_ref, wgb_ref, wub_ref, wdb_ref, go_ref, o_ref, h_ref, **kw)


def _ffn_cast(x, g, wg, wu, wd, out_gain, *, layer, normalize_out, tf=256):
    tm = x.shape[0]
    nj = D_FF // tf
    vmem = 4 * tm * D * 4 + tm * D * 2 + 6 * D * tf * 4 + 9 * D * tf * 2 + 5 * tm * tf * 4
    wcol = lambda dt: (pl.BlockSpec((None, D, tf), lambda i, j: (layer, 0, j)) if dt == F32
                       else pl.BlockSpec((None, D, tf), lambda i, j: (0, 0, j)))
    return pl.pallas_call(
        functools.partial(_ffn_cast_body, nj=nj, normalize_out=normalize_out),
        grid=(1, nj),
        in_specs=[
            pl.BlockSpec((tm, D), lambda i, j: (0, 0)),
            pl.BlockSpec((1, D), lambda i, j: (0, 0)),
            wcol(F32), wcol(F32),
            pl.BlockSpec((None, tf, D), lambda i, j: (layer, j, 0)),
            pl.BlockSpec((1, D), lambda i, j: (0, 0)),
        ],
        out_specs=[
            pl.BlockSpec((tm, D), lambda i, j: (0, 0)),
            wcol(BF), wcol(BF),
            pl.BlockSpec((None, tf, D), lambda i, j: (0, j, 0)),
        ],
        out_shape=[
            jax.ShapeDtypeStruct((tm, D), F32),
            jax.ShapeDtypeStruct((1, D, D_FF), BF),
            jax.ShapeDtypeStruct((1, D, D_FF), BF),
            jax.ShapeDtypeStruct((1, D_FF, D), BF),
        ],
        scratch_shapes=[pltpu.VMEM((tm, D), BF)],
        compiler_params=_params(("arbitrary", "arbitrary"), vmem),
        name="ffn_cast",
    )(x, g, wg, wu, wd, out_gain)


def _ffn(x, g, wg, wu, wd_half, out_gain, *, layer, normalize_out, tm=1024, tf=512):
    R = x.shape[0]
    tm = min(tm, R)
    nj = D_FF // tf
    vmem = 4 * tm * D * 4 + tm * D * 2 + 6 * D * tf * 2 + 5 * tm * tf * 4
    return pl.pallas_call(
        functools.partial(_ffn_body, nj=nj, normalize_out=normalize_out),
        grid=(R // tm, nj),
        in_specs=[
            pl.BlockSpec((tm, D), lambda i, j: (i, 0)),
            pl.BlockSpec((1, D), lambda i, j: (0, 0)),
            pl.BlockSpec((None, D, tf), lambda i, j: (layer, 0, j)),
            pl.BlockSpec((None, D, tf), lambda i, j: (layer, 0, j)),
            pl.BlockSpec((None, tf, D), lambda i, j: (layer, j, 0)),
            pl.BlockSpec((1, D), lambda i, j: (0, 0)),
        ],
        out_specs=pl.BlockSpec((tm, D), lambda i, j: (i, 0)),
        out_shape=jax.ShapeDtypeStruct((R, D), F32),
        scratch_shapes=[pltpu.VMEM((tm, D), BF)],
        compiler_params=_params(("arbitrary", "arbitrary"), vmem),
        name="ffn",
    )(x, g, wg, wu, wd_half, out_gain)


def _mm_res_body(a_ref, w_ref, x_ref, o_ref):
    o_ref[...] = x_ref[...] + jnp.dot(a_ref[...], w_ref[...], preferred_element_type=F32)


def _mm_res(a, w, x, *, tm=512):
    R = x.shape[0]
    vmem = 2 * tm * D * 2 + 2 * D * D * 2 + 5 * tm * D * 4
    return pl.pallas_call(
        _mm_res_body,
        grid=(R // tm,),
        in_specs=[
            pl.BlockSpec((tm, D), lambda i: (i, 0)),
            pl.BlockSpec((D, D), lambda i: (0, 0)),
            pl.BlockSpec((tm, D), lambda i: (i, 0)),
        ],
        out_specs=pl.BlockSpec((tm, D), lambda i: (i, 0)),
        out_shape=jax.ShapeDtypeStruct((R, D), F32),
        compiler_params=_params(("arbitrary",), vmem),
        name="mm_res",
    )(a, w, x)


def _pool_body(x_ref, g_ref, hist_ref, w_ref, sc_ref, o_ref, st_ref, xe_ref, se_ref, *, tm, pos0):
    t = pl.program_id(1)
    n = POOL_PAD + tm
    lo = POOL_LEAD

    @pl.when(t == 0)
    def _():
        xe_ref[0:lo, :] = jnp.zeros((lo, D), F32)
        se_ref[0:lo, :] = jnp.zeros((lo, POOL_GW), F32)
        xe_ref[lo:lo + POOL_PAD, :] = hist_ref[0]

    xf = x_ref[...]
    h = _rms(xf, g_ref[...])
    xe_ref[lo + POOL_PAD:lo + n, :] = h
    row = lax.broadcasted_iota(jnp.int32, (tm, POOL_GW), 0)
    pos1 = (row + (pos0 + 1) + t * tm).astype(F32)
    for gi, w in enumerate(POOL_WINDOWS):
        c0 = gi * POOL_GW
        src, cols = xe_ref, slice(c0, c0 + POOL_GW)
        d = 1
        while d < w:
            se_ref[lo:lo + n, :] = src[lo:lo + n, cols] + src[lo - d:lo - d + n, cols]
            src, cols = se_ref, slice(0, POOL_GW)
            d *= 2
        s = se_ref[lo + POOL_PAD:lo + n, :]
        hg = h[:, c0:c0 + POOL_GW]
        cnt = jnp.minimum(pos1, float(w))
        pooled = s / cnt - hg
        y = jnp.dot(pooled.astype(BF), w_ref[gi], preferred_element_type=F32)
        o_ref[:, c0:c0 + POOL_GW] = xf[:, c0:c0 + POOL_GW] + y * sc_ref[:, c0:c0 + POOL_GW]
    tail = xe_ref[lo + tm:lo + n, :]
    st_ref[0] = tail
    xe_ref[lo:lo + POOL_PAD, :] = tail


def _pool_mixer(x, g, hist, w, scale, *, B, T, tm, pos0):
    nt = T // tm
    vmem = 4 * tm * D * 4 + (tm + POOL_PAD) * D * 4 + 2 * 4 * POOL_GW * POOL_GW * 2 + 6 * tm * D * 4
    return pl.pallas_call(
        functools.partial(_pool_body, tm=tm, pos0=pos0),
        grid=(B, nt),
        in_specs=[
            pl.BlockSpec((tm, D), lambda b, t: (b * nt + t, 0)),
            pl.BlockSpec((1, D), lambda b, t: (0, 0)),
            pl.BlockSpec((1, POOL_PAD, D), lambda b, t: (b, 0, 0)),
            pl.BlockSpec((len(POOL_WINDOWS), POOL_GW, POOL_GW), lambda b, t: (0, 0, 0)),
            pl.BlockSpec((1, D), lambda b, t: (0, 0)),
        ],
        out_specs=[
            pl.BlockSpec((tm, D), lambda b, t: (b * nt + t, 0)),
            pl.BlockSpec((1, POOL_PAD, D), lambda b, t: (b, 0, 0)),
        ],
        out_shape=[jax.ShapeDtypeStruct((B * T, D), F32), jax.ShapeDtypeStruct((B, POOL_PAD, D), F32)],
        scratch_shapes=[pltpu.VMEM((POOL_LEAD + POOL_PAD + tm, D), F32),
                        pltpu.VMEM((POOL_LEAD + POOL_PAD + tm, POOL_GW), F32)],
        compiler_params=_params(("arbitrary", "arbitrary"), vmem),
        name="pool_mixer",
    )(x, g, hist, w, scale)


def _lru_body(gate_ref, xr_ref, cst_ref, h0_ref, cw_ref, cb_ref, wa_ref, ba_ref, wi_ref, bi_ref, lam_ref,
              y_ref, cso_ref, ho_ref, xe_ref, a_ref, u_ref, hc_ref, *, tm):
    t = pl.program_id(1)

    @pl.when(t == 0)
    def _():
        xe_ref[0:CONV_PAD, :] = cst_ref[0]
        hc_ref[...] = h0_ref[0]

    xe_ref[CONV_PAD:CONV_PAD + tm, :] = xr_ref[...]
    base = CONV_PAD - (CONV_W - 1)
    xc = cb_ref[...] + xe_ref[base:base + tm, :] * cw_ref[0:1, :]
    for j in range(1, CONV_W):
        xc = xc + xe_ref[base + j:base + j + tm, :] * cw_ref[j:j + 1, :]
    tail = xe_ref[tm:tm + CONV_PAD, :]
    cso_ref[0] = tail
    xe_ref[0:CONV_PAD, :] = tail

    xcb = xc.astype(BF)
    nlam = -lam_ref[...]
    sp = jnp.maximum(nlam, 0.0) + jnp.log1p(jnp.exp(-jnp.abs(nlam)))
    for n in range(LRU_BLOCKS):
        blk = slice(n * LRU_BW, (n + 1) * LRU_BW)
        ra = jnp.dot(xcb[:, blk], wa_ref[n], preferred_element_type=F32) + ba_ref[:, blk]
        ia = jnp.dot(xcb[:, blk], wi_ref[n], preferred_element_type=F32) + bi_ref[:, blk]
        ig = 0.5 * jnp.tanh(0.5 * ia) + 0.5
        log_a = (-0.5 * LRU_C * sp[:, blk]) * (jnp.tanh(0.5 * ra) + 1.0)
        a_ref[:, blk] = jnp.exp(log_a)
        th = jnp.tanh(log_a)
        one_minus_a2 = -2.0 * th / (1.0 - th)
        u_ref[:, blk] = jnp.sqrt(one_minus_a2) * ig * xc[:, blk]

    rowi = lax.broadcasted_iota(jnp.int32, (8, D), 0)

    def group(gi, carry):
        r0 = pl.multiple_of(gi * 8, 8)
        a8 = a_ref[pl.ds(r0, 8), :]
        u8 = u_ref[pl.ds(r0, 8), :]
        for d in (1, 2, 4):
            keep = rowi >= d
            u8 = jnp.where(keep, a8 * pltpu.roll(u8, d, axis=0) + u8, u8)
            a8 = jnp.where(keep, a8 * pltpu.roll(a8, d, axis=0), a8)
        hs8 = a8 * carry + u8
        u_ref[pl.ds(r0, 8), :] = hs8
        return hs8[7:8, :]

    carry = lax.fori_loop(0, tm // 8, group, hc_ref[...])
    hc_ref[...] = carry
    ho_ref[0] = carry
    y_ref[...] = (u_ref[...] * jax.nn.gelu(gate_ref[...])).astype(BF)


def _lru_core(gate, xr, cst, h0, cw, cb, wa, ba, wi, bi, lam, *, B, T, tm):
    nt = T // tm
    vmem = 4 * tm * D * 4 + 2 * tm * D * 2 + (3 * tm + CONV_PAD) * D * 4 + 4 * 8 * LRU_BW * LRU_BW * 2 + 6 * tm * D * 4
    vec = pl.BlockSpec((1, D), lambda b, t: (0, 0))
    wblk = pl.BlockSpec((LRU_BLOCKS, LRU_BW, LRU_BW), lambda b, t: (0, 0, 0))
    return pl.pallas_call(
        functools.partial(_lru_body, tm=tm),
        grid=(B, nt),
        in_specs=[
            pl.BlockSpec((tm, D), lambda b, t: (b * nt + t, 0)),
            pl.BlockSpec((tm, D), lambda b, t: (b * nt + t, 0)),
            pl.BlockSpec((1, CONV_PAD, D), lambda b, t: (b, 0, 0)),
            pl.BlockSpec((1, 1, D), lambda b, t: (b, 0, 0)),
            pl.BlockSpec((CONV_W, D), lambda b, t: (0, 0)),
            vec, wblk, vec, wblk, vec, vec,
        ],
        out_specs=[
            pl.BlockSpec((tm, D), lambda b, t: (b * nt + t, 0)),
            pl.BlockSpec((1, CONV_PAD, D), lambda b, t: (b, 0, 0)),
            pl.BlockSpec((1, 1, D), lambda b, t: (b, 0, 0)),
        ],
        out_shape=[
            jax.ShapeDtypeStruct((B * T, D), BF),
            jax.ShapeDtypeStruct((B, CONV_PAD, D), F32),
            jax.ShapeDtypeStruct((B, 1, D), F32),
        ],
        scratch_shapes=[
            pltpu.VMEM((tm + CONV_PAD, D), F32),
            pltpu.VMEM((tm, D), F32),
            pltpu.VMEM((tm, D), F32),
            pltpu.VMEM((1, D), F32),
        ],
        compiler_params=_params(("arbitrary", "arbitrary"), vmem),
        name="rglru_core",
    )(gate, xr, cst, h0, cw, cb, wa, ba, wi, bi, lam)


def _proj_call(body, x, g, w, jcol, extra, extra_specs, out_shape, out_specs, scratch, *, tm, vmem, name):
    R = x.shape[0]
    return pl.pallas_call(
        body,
        grid=(R // tm,),
        in_specs=[
            pl.BlockSpec((tm, D), lambda i: (i, 0)),
            pl.BlockSpec((1, D), lambda i: (0, 0)),
            pl.BlockSpec((D, D), lambda i: (0, jcol), pipeline_mode=pl.Buffered(1)),
            *extra_specs,
        ],
        out_specs=out_specs,
        out_shape=out_shape,
        scratch_shapes=scratch,
        compiler_params=_params(("arbitrary",), vmem),
        name=name,
    )(x, g, w, *extra)


def _normed_dot(x_ref, g_ref, w_ref):
    h = _rms(x_ref[...], g_ref[...]).astype(BF)
    return h, jnp.dot(h, w_ref[...], preferred_element_type=F32)


def _store_heads_interleaved(ref, y, tm):
    ref[...] = pltpu.einshape("m(hd)->(mh)d", y, h=FOX_HEADS)


def _fox_q_body(x_ref, g_ref, w_ref, q_ref, *, qscale):
    _, y = _normed_dot(x_ref, g_ref, w_ref)
    yq = (y * qscale).astype(BF)
    for hh in range(FOX_HEADS):
        q_ref[hh] = yq[:, hh * FOX_HD:(hh + 1) * FOX_HD]


def _fox_k_body(x_ref, g_ref, w_ref, wf_ref, wft_ref, bf_ref, bft_ref,
                k_ref, ka_ref, lf_ref, cc_ref, cr_ref, carc_ref, carr_ref, *, tm, seg, carry):
    h, y = _normed_dot(x_ref, g_ref, w_ref)
    lf = _log_sigmoid(jnp.dot(h, wf_ref[...], preferred_element_type=F32) + bf_ref[...])
    lft = _log_sigmoid(_nt_dot(wft_ref[...], h) + bft_ref[...])
    lf_ref[...] = lf
    cc = _sum_dot_left(_tri(tm, "le", seg), lf)
    cr = _sum_dot_right(lft, _tri(tm, "ge", seg))
    if carry:
        @pl.when(pl.program_id(0) == 0)
        def _():
            carc_ref[...] = jnp.zeros_like(carc_ref)
            carr_ref[...] = jnp.zeros_like(carr_ref)

        cc = cc + carc_ref[...]
        cr = cr + carr_ref[...]
        carc_ref[...] = cc[tm - 1:tm, :]
        carr_ref[...] = cr[:, tm - 1:tm]
    cc_ref[...] = cc
    cr_ref[...] = cr

    _store_heads_interleaved(k_ref, y, tm)
    yb = y.astype(BF)
    pieces = jnp.concatenate(_split3_bf16(cc * (-LOG2E)), axis=1)
    r = lax.broadcasted_iota(jnp.int32, (3 * FOX_HEADS, D), 0)
    c = lax.broadcasted_iota(jnp.int32, (3 * FOX_HEADS, D), 1)
    sel = jnp.logical_and(c // FOX_HD == r % FOX_HEADS, c % FOX_HD == r // FOX_HEADS)
    aug = jnp.dot(pieces, jnp.where(sel, 1.0, 0.0).astype(BF), preferred_element_type=F32).astype(BF)
    for hh in range(FOX_HEADS):
        ka_ref[hh, :, 0:FOX_HD] = yb[:, hh * FOX_HD:(hh + 1) * FOX_HD]
        ka_ref[hh, :, FOX_HD:2 * FOX_HD] = aug[:, hh * FOX_HD:(hh + 1) * FOX_HD]


def _fox_v_body(x_ref, g_ref, w_ref, v_ref, vt_ref, *, tm):
    _, y = _normed_dot(x_ref, g_ref, w_ref)
    _store_heads_interleaved(v_ref, y, tm)
    for hh in range(FOX_HEADS):
        vt_ref[hh] = y[:, hh * FOX_HD:(hh + 1) * FOX_HD].T.astype(BF)


def _fox_proj(x, g, w, wf, wft, bf, bft, *, tm, seg, carry):
    R = x.shape[0]
    H = FOX_HEADS
    qscale = (FOX_HD ** -0.5) * LOG2E
    base = 2 * tm * D * 4 + D * D * 2 + 2 * tm * D * 4 + tm * D * 2
    const = lambda shape: pl.BlockSpec(shape, lambda i: (0,) * len(shape))
    native = pl.BlockSpec((tm * H, FOX_HD), lambda i: (i, 0))
    qh = _proj_call(
        functools.partial(_fox_q_body, qscale=qscale), x, g, w, 0, (), (),
        jax.ShapeDtypeStruct((H, R, FOX_HD), BF), pl.BlockSpec((H, tm, FOX_HD), lambda i: (0, i, 0)), (),
        tm=tm, vmem=base + 3 * tm * D * 2, name="fox_proj_q")
    k, ka, lf, cc, cr = _proj_call(
        functools.partial(_fox_k_body, tm=tm, seg=seg, carry=carry), x, g, w, 1,
        (wf, wft, bf, bft), (const((D, H)), const((H, D)), const((1, H)), const((H, 1))),
        [
            jax.ShapeDtypeStruct((R * H, FOX_HD), F32),
            jax.ShapeDtypeStruct((H, R, 2 * FOX_HD), BF),
            jax.ShapeDtypeStruct((R, H), F32),
            jax.ShapeDtypeStruct((R, H), F32),
            jax.ShapeDtypeStruct((H, R), F32),
        ],
        [
            native,
            pl.BlockSpec((H, tm, 2 * FOX_HD), lambda i: (0, i, 0)),
            pl.BlockSpec((tm, H), lambda i: (i, 0)),
            pl.BlockSpec((tm, H), lambda i: (i, 0)),
            pl.BlockSpec((H, tm), lambda i: (0, i)),
        ],
        [pltpu.VMEM((1, H), F32), pltpu.VMEM((H, 1), F32)],
        tm=tm, vmem=base + 2 * tm * D * 4 + 5 * tm * D * 2 + 3 * tm * tm * 4 + 4 * tm * 128 * 4, name="fox_proj_k")
    v, vt = _proj_call(
        functools.partial(_fox_v_body, tm=tm), x, g, w, 2, (), (),
        [jax.ShapeDtypeStruct((R * H, FOX_HD), F32), jax.ShapeDtypeStruct((H, FOX_HD, R), BF)],
        [native, pl.BlockSpec((H, FOX_HD, tm), lambda i: (0, 0, i))], (),
        tm=tm, vmem=base + 2 * tm * D * 4 + 3 * tm * D * 2 + tm * D * 4, name="fox_proj_v")
    return qh, k, v, ka, vt, lf, cc, cr


def _softmax_step(s_biased, cq, pv, m_ref, l_ref, acc_ref):
    m_old = m_ref[...]
    m_new = jnp.maximum(m_old, jnp.max(s_biased, axis=-1, keepdims=True) + cq)
    p = jnp.exp2(s_biased - (m_new - cq))
    alpha = jnp.exp2(m_old - m_new)
    l_ref[...] = alpha * l_ref[...] + jnp.sum(p, axis=-1, keepdims=True)
    acc_ref[...] = alpha * acc_ref[...] + pv(p.astype(BF))
    m_ref[...] = m_new


def _fox_attn_body(q_ref, ka_ref, vt_ref, cr_ref, o_ref, qt_ref, m_ref, l_ref, acc_ref, sa_ref, sb_ref, *, tq, tk):
    h = pl.program_id(0)
    qi = pl.program_id(1)
    q0 = pl.multiple_of(qi * tq, tq)
    qt_ref[0:FOX_HD, :] = q_ref[0].astype(F32).T.astype(BF)
    sub = lax.broadcasted_iota(jnp.int32, (FOX_HD, tq), 0)
    qt_ref[FOX_HD:2 * FOX_HD, :] = jnp.where(sub < 3, 1.0, 0.0).astype(BF)
    cq = cr_ref[pl.ds(h, 1), pl.ds(q0, tq)] * LOG2E
    m_ref[...] = jnp.full_like(m_ref, NEG)
    l_ref[...] = jnp.zeros_like(l_ref)
    acc_ref[...] = jnp.zeros_like(acc_ref)

    def scores(t, st_ref):
        k0 = pl.multiple_of(t * tk, tk)
        st_ref[...] = jnp.dot(ka_ref[0, pl.ds(k0, tk), :], qt_ref[...], preferred_element_type=F32)

    def fold(t, st_ref, diag=None):
        k0 = pl.multiple_of(t * tk, tk)
        st = st_ref[...]
        if diag is not None:
            kj = lax.broadcasted_iota(jnp.int32, (tk, tq), 0) + diag * tk
            qcol = lax.broadcasted_iota(jnp.int32, (tk, tq), 1)
            st = jnp.where(kj <= qcol, st, NEG)
        m_old = m_ref[...]
        m_new = jnp.maximum(m_old, jnp.max(st, axis=0, keepdims=True) + cq)
        p = jnp.exp2(st - (m_new - cq))
        alpha = jnp.exp2(m_old - m_new)
        l_ref[...] = alpha * l_ref[...] + jnp.sum(p, axis=0, keepdims=True)
        acc_ref[...] = alpha * acc_ref[...] + jnp.dot(
            vt_ref[0, :, pl.ds(k0, tk)], p.astype(BF), preferred_element_type=F32)
        m_ref[...] = m_new

    scores(0, sa_ref)

    def quad(u, _):
        t = 4 * u
        scores(t + 1, sb_ref)
        fold(t, sa_ref)
        scores(t + 2, sa_ref)
        fold(t + 1, sb_ref)
        scores(t + 3, sb_ref)
        fold(t + 2, sa_ref)
        scores(t + 4, sa_ref)
        fold(t + 3, sb_ref)
        return 0

    lax.fori_loop(0, qi // 2, quad, 0)
    td = 2 * qi

    @pl.when(qi % 2 == 1)
    def _():
        scores(td - 1, sb_ref)
        fold(td - 2, sa_ref)
        scores(td, sa_ref)
        fold(td - 1, sb_ref)

    scores(td + 1, sb_ref)
    fold(td, sa_ref, diag=0)
    fold(td + 1, sb_ref, diag=1)
    o_ref[...] = (acc_ref[...] / l_ref[...]).T.astype(BF)


def _fox_attn(qh, ka, vt, cr, *, T, tq=1024):
    H = FOX_HEADS
    tk = tq // 2
    vmem = 2 * T * 2 * FOX_HD * 2 + 2 * T * FOX_HD * 2 + 2 * H * T * 4 + 8 * tk * tq * 4 + 16 * tq * FOX_HD * 4
    return pl.pallas_call(
        functools.partial(_fox_attn_body, tq=tq, tk=tk),
        grid=(H, T // tq),
        in_specs=[
            pl.BlockSpec((1, tq, FOX_HD), lambda h, i: (h, i, 0)),
            pl.BlockSpec((1, T, 2 * FOX_HD), lambda h, i: (h, 0, 0)),
            pl.BlockSpec((1, FOX_HD, T), lambda h, i: (h, 0, 0)),
            pl.BlockSpec((H, T), lambda h, i: (0, 0)),
        ],
        out_specs=pl.BlockSpec((tq, FOX_HD), lambda h, i: (i, h)),
        out_shape=jax.ShapeDtypeStruct((T, D), BF),
        scratch_shapes=[pltpu.VMEM((2 * FOX_HD, tq), BF), pltpu.VMEM((1, tq), F32), pltpu.VMEM((1, tq), F32),
                        pltpu.VMEM((FOX_HD, tq), F32), pltpu.VMEM((tk, tq), F32), pltpu.VMEM((tk, tq), F32)],
        compiler_params=_params(("arbitrary", "arbitrary"), vmem),
        name="fox_attention",
    )(qh, ka, vt, cr)


def _suffix_body(x_ref, o_ref, car_ref, *, tc):
    @pl.when(pl.program_id(1) == 0)
    def _():
        car_ref[...] = jnp.zeros_like(car_ref)

    x = x_ref[0]
    o_ref[0] = _sum_dot_right(x, _tri(tc, "gt")) + car_ref[...]
    car_ref[...] = car_ref[...] + jnp.sum(x, axis=1, keepdims=True)


def _suffix_sum(x, *, tc=512):
    B, H, S = x.shape
    nc = S // tc
    return pl.pallas_call(
        functools.partial(_suffix_body, tc=tc),
        grid=(B, nc),
        in_specs=[pl.BlockSpec((1, H, tc), lambda b, t: (b, 0, nc - 1 - t))],
        out_specs=pl.BlockSpec((1, H, tc), lambda b, t: (b, 0, nc - 1 - t)),
        out_shape=jax.ShapeDtypeStruct((B, H, S), F32),
        scratch_shapes=[pltpu.VMEM((H, 1), F32)],
        compiler_params=_params(("arbitrary", "arbitrary"), 8 * tc * tc * 4),
        name="suffix_sum",
    )(x)


def _fox_decode_body(q_ref, ka_ref, vt_ref, kc_ref, vc_ref, dsuf_ref, ec_ref, er_ref, o_ref,
                     m_ref, l_ref, acc_ref, *, nt, tq, ts):
    b = pl.program_id(0)
    t = pl.program_id(1)
    H = FOX_HEADS

    @pl.when(t == 0)
    def _():
        m_ref[...] = jnp.full_like(m_ref, NEG)
        l_ref[...] = jnp.zeros_like(l_ref)
        acc_ref[...] = jnp.zeros_like(acc_ref)

    ec = ec_ref[...] * LOG2E
    dsuf = dsuf_ref[0] * LOG2E
    s_all = [_nt_dot(q_ref[hh], kc_ref[0, pl.ds(hh, ts, stride=H), :].astype(BF)) + dsuf[hh:hh + 1, :]
             for hh in range(H)]
    for hh in range(H):
        vt = vc_ref[0, pl.ds(hh, ts, stride=H), :].astype(BF)
        _softmax_step(s_all[hh], ec[:, hh:hh + 1], lambda p, vt=vt: jnp.dot(p, vt, preferred_element_type=F32),
                      m_ref.at[hh], l_ref.at[hh], acc_ref.at[hh])

    @pl.when(t == nt - 1)
    def _():
        n_new = er_ref.shape[1]
        er = er_ref[...] * LOG2E
        r = lax.broadcasted_iota(jnp.int32, (tq, n_new), 0)
        c = lax.broadcasted_iota(jnp.int32, (tq, n_new), 1) - b * tq
        valid = jnp.logical_and(c >= 0, c <= r)
        for hh in range(H):
            s = _nt_dot(q_ref[hh], ka_ref[hh][:, 0:FOX_HD]) - er[hh:hh + 1, :]
            s = jnp.where(valid, s, NEG)
            _softmax_step(s, ec[:, hh:hh + 1], lambda p, hh=hh: _nt_dot(p, vt_ref[hh]),
                          m_ref.at[hh], l_ref.at[hh], acc_ref.at[hh])
            o_ref[:, hh * FOX_HD:(hh + 1) * FOX_HD] = (acc_ref[hh] / l_ref[hh]).astype(BF)


def _fox_decode(qh, ka, vt, kc, vc, dsuf, ec, er, *, B, tq, ts=512):
    H = FOX_HEADS
    S = kc.shape[1] // H
    R = B * tq
    nt = S // ts
    vmem = (4 * ts * D * 4 + 2 * H * R * 3 * FOX_HD * 2 + 2 * H * tq * FOX_HD * 2 + 3 * H * tq * 128 * 4
            + 4 * ts * FOX_HD * 2 + 8 * tq * max(ts, R) * 4)
    return pl.pallas_call(
        functools.partial(_fox_decode_body, nt=nt, tq=tq, ts=ts),
        grid=(B, nt),
        in_specs=[
            pl.BlockSpec((H, tq, FOX_HD), lambda b, t: (0, b, 0)),
            pl.BlockSpec((H, R, 2 * FOX_HD), lambda b, t: (0, 0, 0)),
            pl.BlockSpec((H, FOX_HD, R), lambda b, t: (0, 0, 0)),
            pl.BlockSpec((1, ts * H, FOX_HD), lambda b, t: (b, t, 0)),
            pl.BlockSpec((1, ts * H, FOX_HD), lambda b, t: (b, t, 0)),
            pl.BlockSpec((1, H, ts), lambda b, t: (b, 0, t)),
            pl.BlockSpec((tq, H), lambda b, t: (b, 0)),
            pl.BlockSpec((H, R), lambda b, t: (0, 0)),
        ],
        out_specs=pl.BlockSpec((tq, D), lambda b, t: (b, 0)),
        out_shape=jax.ShapeDtypeStruct((R, D), BF),
        scratch_shapes=[pltpu.VMEM((H, tq, 1), F32), pltpu.VMEM((H, tq, 1), F32), pltpu.VMEM((H, tq, FOX_HD), F32)],
        compiler_params=_params(("arbitrary", "arbitrary"), vmem),
        name="fox_decode_attention",
    )(qh, ka, vt, kc, vc, dsuf, ec, er)


def _mlstm_qk_body(x_ref, g_ref, w_ref, wg_ref, wgt_ref, bg_ref, bgt_ref, q_ref, k_ref, gc_ref, gr_ref, *, qscale):
    h, y = _normed_dot(x_ref, g_ref, w_ref)
    zc = jnp.dot(h, wg_ref[...], preferred_element_type=F32) + bg_ref[...]
    lane = lax.broadcasted_iota(jnp.int32, zc.shape, 1)
    gc_ref[...] = jnp.where(lane < ML_HEADS, zc, _log_sigmoid(zc))
    zr = _nt_dot(wgt_ref[...], h) + bgt_ref[...]
    sub = lax.broadcasted_iota(jnp.int32, zr.shape, 0)
    gr_ref[...] = jnp.where(sub < ML_HEADS, zr, _log_sigmoid(zr))
    q_ref[...] = (y[:, :ML_QK] * qscale).astype(BF)
    k_ref[...] = y[:, ML_QK:]


def _proj_bf16_body(x_ref, g_ref, w_ref, o_ref):
    o_ref[...] = _normed_dot(x_ref, g_ref, w_ref)[1].astype(BF)


def _proj_f32_body(x_ref, g_ref, w_ref, o_ref):
    o_ref[...] = _normed_dot(x_ref, g_ref, w_ref)[1]


def _proj_plain(x, g, w, jcol, dtype, *, tm=512, name):
    R = x.shape[0]
    body = _proj_bf16_body if dtype == BF else _proj_f32_body
    vmem = 2 * tm * D * 4 + D * D * 2 + 2 * tm * D * 4 + tm * D * 2 + 2 * tm * D * 4
    return _proj_call(body, x, g, w, jcol, (), (), jax.ShapeDtypeStruct((R, D), dtype),
                      pl.BlockSpec((tm, D), lambda i: (i, 0)), (), tm=tm, vmem=vmem, name=name)


def _mlstm_proj(x, g, w, wg, wgt, bg, bgt, *, tm=512):
    R = x.shape[0]
    G = 2 * ML_HEADS
    const = lambda shape: pl.BlockSpec(shape, lambda i: (0,) * len(shape))
    vmem = 2 * tm * D * 4 + D * D * 2 + 2 * tm * D * 4 + tm * D * 2 + 2 * tm * D * 4
    q, k, gc, gr = _proj_call(
        functools.partial(_mlstm_qk_body, qscale=ML_DK ** -0.5), x, g, w, 0,
        (wg, wgt, bg, bgt), (const((D, G)), const((G, D)), const((1, G)), const((G, 1))),
        [
            jax.ShapeDtypeStruct((R, ML_QK), BF),
            jax.ShapeDtypeStruct((R, ML_QK), F32),
            jax.ShapeDtypeStruct((R, G), F32),
            jax.ShapeDtypeStruct((G, R), F32),
        ],
        [
            pl.BlockSpec((tm, ML_QK), lambda i: (i, 0)),
            pl.BlockSpec((tm, ML_QK), lambda i: (i, 0)),
            pl.BlockSpec((tm, G), lambda i: (i, 0)),
            pl.BlockSpec((G, tm), lambda i: (0, i)),
        ],
        (), tm=tm, vmem=vmem, name="mlstm_proj_qk")
    v = _proj_plain(x, g, w, 1, BF, tm=tm, name="mlstm_proj_v")
    o = _proj_plain(x, g, w, 2, F32, tm=tm, name="mlstm_proj_o")
    return q, k, v, o, gc, gr


def _mlstm_body(q_ref, k_ref, v_ref, o_ref, gc_ref, gr_ref, gn_ref, c0_ref, n0_ref, m0_ref,
                y_ref, co_ref, no_ref, mo_ref, c_ref, n_ref, m_ref, *, L):
    t = pl.program_id(1)

    @pl.when(t == 0)
    def _():
        c_ref[...] = c0_ref[0]
        n_ref[...] = n0_ref[0]
        m_ref[...] = m0_ref[0]

    gc = gc_ref[...]
    gr = gr_ref[0]
    ra = lax.broadcasted_iota(jnp.int32, (L, L), 0)
    cb = lax.broadcasted_iota(jnp.int32, (L, L), 1)
    causal = cb <= ra
    bcs = _sum_dot_left(_tri(L, "le"), gc)
    brs = _sum_dot_right(gr, _tri(L, "ge"))
    for hh in range(ML_HEADS):
        b_c = bcs[:, ML_HEADS + hh:ML_HEADS + hh + 1]
        g_c = gc[:, hh:hh + 1] - b_c
        g_r = gr[hh:hh + 1, :] - brs[ML_HEADS + hh:ML_HEADS + hh + 1, :]
        m_h = m_ref[hh][:, 0:1]
        am = jnp.where(causal, b_c + g_r, NEG)
        mt = jnp.maximum(b_c + m_h, jnp.max(am, axis=-1, keepdims=True))
        d = jnp.exp(am - mt)
        qh = q_ref[:, hh * ML_DK:(hh + 1) * ML_DK]
        kf = k_ref[:, hh * ML_DK:(hh + 1) * ML_DK]
        vh = v_ref[:, hh * ML_DV:(hh + 1) * ML_DV]
        sc = _nt_dot(qh, kf.astype(BF)) * d
        inter = jnp.exp(b_c + m_h - mt)
        c_old = c_ref[hh]
        n_old = n_ref[hh]
        num = (jnp.dot(sc.astype(BF), vh, preferred_element_type=F32)
               + inter * jnp.dot(qh, c_old.astype(BF), preferred_element_type=F32))
        den = (jnp.sum(sc, axis=-1, keepdims=True)
               + inter * jnp.sum(qh.astype(F32) * n_old, axis=-1, keepdims=True))
        hout = num / jnp.maximum(jnp.abs(den), jnp.exp(-mt))
        m_new = mt[L - 1:L, :]
        b_last = b_c[L - 1:L, :]
        decay = jnp.exp(b_last + m_h - m_new)
        wk = jnp.exp(b_last + g_c - m_new) * kf
        c_ref[hh] = decay * c_old + lax.dot_general(
            wk.astype(BF), vh, (((0,), (0,)), ((), ())), preferred_element_type=F32)
        n_ref[hh] = decay * n_old + jnp.sum(wk, axis=0, keepdims=True)
        m_ref[hh] = jnp.broadcast_to(m_new, (1, 128))
        hn = hout * lax.rsqrt(jnp.mean(hout * hout, axis=-1, keepdims=True) + EPS)
        cols = slice(hh * ML_DV, (hh + 1) * ML_DV)
        y_ref[:, cols] = (hn * gn_ref[:, cols] * jax.nn.sigmoid(o_ref[:, cols])).astype(BF)

    co_ref[0] = c_ref[...]
    no_ref[0] = n_ref[...]
    mo_ref[0] = m_ref[...]


def _mlstm_core(q, k, v, o, gc, gr, gn, c0, n0, m0, *, B, T, L):
    nt = T // L
    G = 2 * ML_HEADS
    H = ML_HEADS
    vmem = (2 * L * (ML_QK * 6 + D * 8) + 6 * H * ML_DK * ML_DV * 4 + 16 * L * L * 4 + 12 * L * ML_DV * 4
            + 4 * 1024 * 1024)
    rows = lambda w: pl.BlockSpec((L, w), lambda b, t: (b * nt + t, 0))
    st_c = pl.BlockSpec((1, H, ML_DK, ML_DV), lambda b, t: (b, 0, 0, 0))
    st_n = pl.BlockSpec((1, H, 1, ML_DK), lambda b, t: (b, 0, 0, 0))
    st_m = pl.BlockSpec((1, H, 1, 128), lambda b, t: (b, 0, 0, 0))
    return pl.pallas_call(
        functools.partial(_mlstm_body, L=L),
        grid=(B, nt),
        in_specs=[
            rows(ML_QK), rows(ML_QK), rows(D), rows(D), rows(G),
            pl.BlockSpec((1, G, L), lambda b, t: (b, 0, t)),
            pl.BlockSpec((1, D), lambda b, t: (0, 0)),
            st_c, st_n, st_m,
        ],
        out_specs=[rows(D), st_c, st_n, st_m],
        out_shape=[
            jax.ShapeDtypeStruct((B * T, D), BF),
            jax.ShapeDtypeStruct((B, H, ML_DK, ML_DV), F32),
            jax.ShapeDtypeStruct((B, H, 1, ML_DK), F32),
            jax.ShapeDtypeStruct((B, H, 1, 128), F32),
        ],
        scratch_shapes=[
            pltpu.VMEM((H, ML_DK, ML_DV), F32),
            pltpu.VMEM((H, 1, ML_DK), F32),
            pltpu.VMEM((H, 1, 128), F32),
        ],
        compiler_params=_params(("arbitrary", "arbitrary"), vmem),
        name="mlstm_core",
    )(q, k, v, o, gc, gr, gn, c0, n0, m0)


def kernel(x_prompt, x_sample, state_pool, state_lru_conv, state_lru_h, cache_fox_k, cache_fox_v, cache_fox_logf, state_mlstm_c, state_mlstm_n, state_mlstm_m, ffn1_norm, ffn1_w_gate, ffn1_w_up, ffn1_w_down, mix_norm, ffn2_norm, ffn2_w_gate, ffn2_w_up, ffn2_w_down, pool_w, pool_scale, lru_w_in, lru_conv_w, lru_conv_b, lru_w_a, lru_b_a, lru_w_i, lru_b_i, lru_lambda, lru_w_out, fox_w_qkv, fox_w_f, fox_b_f, fox_w_o, mlstm_w_in, mlstm_b_i, mlstm_b_f, mlstm_norm, mlstm_w_out, final_norm):
    BP, TP, _ = x_prompt.shape
    BS, TS, _ = x_sample.shape
    assert BP == 1 and x_prompt.shape[2] == D and x_sample.shape[2] == D
    RS = BS * TS
    past = cache_fox_k.shape[1]
    row = lambda p: p.reshape(1, -1).astype(F32)

    xp = x_prompt.reshape(TP, D)
    xs = x_sample.reshape(RS, D)

    ffn_w = {
        1: (ffn1_norm, ffn1_w_gate, ffn1_w_up, ffn1_w_down),
        2: (ffn2_norm, ffn2_w_gate, ffn2_w_up, ffn2_w_down),
    }

    def ffn(xp, xs, which, layer, last=False):
        norm, wg, wu, wd = ffn_w[which]
        g, gf = row(norm[layer]), row(final_norm)
        xs, wgb, wub, wdb = _ffn_cast(xs, g, wg.astype(F32), wu.astype(F32), wd.astype(F32), gf,
                                      layer=layer, normalize_out=last)
        return _ffn(xp, g, wgb, wub, wdb, gf, layer=0, normalize_out=last), xs

    xp, xs = ffn(xp, xs, 1, 0)
    pw = pool_w.astype(BF)
    hist_s = jnp.pad(state_pool.astype(F32), ((0, 0), (POOL_PAD - state_pool.shape[1], 0), (0, 0)))
    xp, pool_p = _pool_mixer(xp, row(mix_norm[0]), jnp.zeros((BP, POOL_PAD, D), F32), pw, row(pool_scale),
                             B=BP, T=TP, tm=512, pos0=0)
    xs, pool_s = _pool_mixer(xs, row(mix_norm[0]), hist_s, pw, row(pool_scale), B=BS, T=TS, tm=TS, pos0=past)
    pool_p = pool_p[:, 1:]
    pool_s = pool_s[:, 1:]
    xp, xs = ffn(xp, xs, 2, 0)

    xp, xs = ffn(xp, xs, 1, 1)
    w_in = lru_w_in.astype(BF)
    lru_args = (lru_conv_w.astype(F32), row(lru_conv_b), lru_w_a.astype(BF), row(lru_b_a),
                lru_w_i.astype(BF), row(lru_b_i), row(lru_lambda))
    lru_in = lambda x, jcol: _proj_plain(x, row(mix_norm[1]), w_in, jcol, F32, name="rglru_proj")
    cst_s = jnp.pad(state_lru_conv.astype(F32), ((0, 0), (CONV_PAD - (CONV_W - 1), 0), (0, 0)))
    yp, conv_p, h_p = _lru_core(lru_in(xp, 0), lru_in(xp, 1), jnp.zeros((BP, CONV_PAD, D), F32),
                                jnp.zeros((BP, 1, D), F32), *lru_args, B=BP, T=TP, tm=256)
    ys, conv_s, h_s = _lru_core(lru_in(xs, 0), lru_in(xs, 1), cst_s, state_lru_h.astype(F32).reshape(BS, 1, D),
                                *lru_args, B=BS, T=TS, tm=TS)
    w_out = lru_w_out.astype(BF)
    xp = _mm_res(yp, w_out, xp)
    xs = _mm_res(ys, w_out, xs)
    lru_conv_p, lru_conv_s = conv_p[:, CONV_PAD - (CONV_W - 1):], conv_s[:, CONV_PAD - (CONV_W - 1):]
    lru_h_p, lru_h_s = h_p.reshape(BP, D), h_s.reshape(BS, D)
    xp, xs = ffn(xp, xs, 2, 1)

    xp, xs = ffn(xp, xs, 1, 2)
    w_qkv = fox_w_qkv.astype(BF)
    wf = fox_w_f.astype(BF)
    wft = wf.T
    bf_r = fox_b_f.astype(F32).reshape(1, FOX_HEADS)
    bf_c = fox_b_f.astype(F32).reshape(FOX_HEADS, 1)
    qh_p, k_p, v_p, ka_p, vt_p, lf_p, cc_p, cr_p = _fox_proj(
        xp, row(mix_norm[2]), w_qkv, wf, wft, bf_r, bf_c, tm=512, seg=512, carry=True)
    qh_s, k_s, v_s, ka_s, vt_s, lf_s, cc_s, cr_s = _fox_proj(
        xs, row(mix_norm[2]), w_qkv, wf, wft, bf_r, bf_c, tm=RS, seg=TS, carry=False)
    o_p = _fox_attn(qh_p, ka_p, vt_p, cr_p, T=TP)
    dsuf = _suffix_sum(jnp.swapaxes(cache_fox_logf.astype(F32), 1, 2))
    o_s = _fox_decode(qh_s, ka_s, vt_s,
                      cache_fox_k.reshape(BS, past * FOX_HEADS, FOX_HD),
                      cache_fox_v.reshape(BS, past * FOX_HEADS, FOX_HD),
                      dsuf, cc_s, cr_s, B=BS, tq=TS)
    w_o = fox_w_o.astype(BF)
    xp = _mm_res(o_p, w_o, xp)
    xs = _mm_res(o_s, w_o, xs)
    fox_k_p = k_p.reshape(BP, TP, FOX_HEADS, FOX_HD)
    fox_v_p = v_p.reshape(BP, TP, FOX_HEADS, FOX_HD)
    fox_k_s = k_s.reshape(BS, TS, FOX_HEADS, FOX_HD)
    fox_v_s = v_s.reshape(BS, TS, FOX_HEADS, FOX_HD)
    fox_lf_p = lf_p.reshape(BP, TP, FOX_HEADS)
    fox_lf_s = lf_s.reshape(BS, TS, FOX_HEADS)
    xp, xs = ffn(xp, xs, 2, 2)

    xp, xs = ffn(xp, xs, 1, 3)
    n_main = 2 * ML_QK + 2 * D
    w_main = mlstm_w_in[:, :n_main].astype(BF)
    w_gates = mlstm_w_in[:, n_main:].astype(BF)
    b_gates = jnp.concatenate([mlstm_b_i, mlstm_b_f]).astype(F32)
    ml_w = (row(mix_norm[3]), w_main, w_gates, w_gates.T, b_gates.reshape(1, -1), b_gates.reshape(-1, 1))
    q_p, kk_p, vv_p, og_p, gc_p, gr_p = _mlstm_proj(xp, *ml_w)
    q_s, kk_s, vv_s, og_s, gc_s, gr_s = _mlstm_proj(xs, *ml_w)
    G = 2 * ML_HEADS
    gn = row(mlstm_norm)
    yp, ml_c_p, ml_n_p, ml_m_p = _mlstm_core(
        q_p, kk_p, vv_p, og_p, gc_p, gr_p.reshape(1, G, TP), gn,
        jnp.zeros((BP, ML_HEADS, ML_DK, ML_DV), F32), jnp.zeros((BP, ML_HEADS, 1, ML_DK), F32),
        jnp.zeros((BP, ML_HEADS, 1, 128), F32), B=BP, T=TP, L=256)
    m0_s = jnp.broadcast_to(state_mlstm_m.astype(F32)[:, :, None, None], (BS, ML_HEADS, 1, 128))
    ys, ml_c_s, ml_n_s, ml_m_s = _mlstm_core(
        q_s, kk_s, vv_s, og_s, gc_s, jnp.swapaxes(gr_s.reshape(G, BS, TS), 0, 1), gn,
        state_mlstm_c.astype(F32), state_mlstm_n.astype(F32).reshape(BS, ML_HEADS, 1, ML_DK), m0_s,
        B=BS, T=TS, L=TS)
    w_out = mlstm_w_out.astype(BF)
    xp = _mm_res(yp, w_out, xp)
    xs = _mm_res(ys, w_out, xs)
    ml_n_p, ml_n_s = ml_n_p.reshape(BP, ML_HEADS, ML_DK), ml_n_s.reshape(BS, ML_HEADS, ML_DK)
    ml_m_p, ml_m_s = ml_m_p[:, :, 0, 0], ml_m_s[:, :, 0, 0]
    xp, xs = ffn(xp, xs, 2, 3, last=True)

    y_prompt = xp.reshape(BP, TP, D)
    y_sample = xs.reshape(BS, TS, D)
    return (y_prompt, y_sample, pool_p, pool_s, lru_conv_p, lru_conv_s, lru_h_p, lru_h_s,
            fox_k_p, fox_k_s, fox_v_p, fox_v_s, fox_lf_p, fox_lf_s,
            ml_c_p, ml_c_s, ml_n_p, ml_n_s, ml_m_p, ml_m_s)
```

```python
import functools
import math

import jax
import jax.numpy as jnp
from jax import lax
from jax.experimental import pallas as pl
from jax.experimental.pallas import tpu as pltpu

F32 = jnp.float32
BF = jnp.bfloat16

D = 2048
D_FF = 5632
EPS = 1e-6
NEG = -1e30
LOG2E = math.log2(math.e)

POOL_WINDOWS = (2, 4, 8, 16)
POOL_GW = D // len(POOL_WINDOWS)
POOL_PAD = 16
POOL_LEAD = 8
CONV_W = 4
CONV_PAD = 8
LRU_BLOCKS = 8
LRU_BW = D // LRU_BLOCKS
LRU_C = 8.0
FOX_HEADS = 16
FOX_HD = D // FOX_HEADS
ML_HEADS = 8
ML_DV = D // ML_HEADS
ML_DK = ML_DV // 2
ML_QK = ML_HEADS * ML_DK

V7X_VMEM_LIMIT = 60000 * 1024


def _params(semantics, vmem_bytes):
    return pltpu.CompilerParams(dimension_semantics=semantics,
                                vmem_limit_bytes=min(int(vmem_bytes), V7X_VMEM_LIMIT))


def _rms(xf, g):
    ms = jnp.mean(xf * xf, axis=-1, keepdims=True)
    return xf * lax.rsqrt(ms + EPS) * g


def _nt_dot(a, b):
    return lax.dot_general(a, b, (((1,), (1,)), ((), ())), preferred_element_type=F32)


def _log_sigmoid(z):
    return jnp.minimum(z, 0.0) - jnp.log1p(jnp.exp(-jnp.abs(z)))


def _split3_bf16(c):
    hi = c.astype(BF)
    r1 = c - hi.astype(F32)
    mid = r1.astype(BF)
    lo = (r1 - mid.astype(F32)).astype(BF)
    return hi, mid, lo


def _sum_dot_left(tri, x):
    return sum(jnp.dot(tri, part, preferred_element_type=F32) for part in _split3_bf16(x))


def _sum_dot_right(x, tri):
    return sum(jnp.dot(part, tri, preferred_element_type=F32) for part in _split3_bf16(x))


def _tri(n, kind, seg=None):
    a = lax.broadcasted_iota(jnp.int32, (n, n), 0)
    b = lax.broadcasted_iota(jnp.int32, (n, n), 1)
    if kind == "le":
        m = b <= a
        if seg is not None:
            m = jnp.logical_and(m, b >= jnp.bitwise_and(a, -seg))
    elif kind == "ge":
        m = b >= a
        if seg is not None:
            m = jnp.logical_and(m, a >= jnp.bitwise_and(b, -seg))
    else:
        m = a > b
    return jnp.where(m, 1.0, 0.0).astype(BF)


def _ffn_chunk(x_ref, g_ref, wg_ref, wu_ref, wd_ref, go_ref, o_ref, h_ref, *, nj, normalize_out):
    j = pl.program_id(1)

    @pl.when(j == 0)
    def _():
        xf = x_ref[...]
        h_ref[...] = _rms(xf, g_ref[...]).astype(BF)
        o_ref[...] = xf

    h = h_ref[...]
    gt = jnp.dot(h, wg_ref[...], preferred_element_type=F32)
    up = jnp.dot(h, wu_ref[...], preferred_element_type=F32)
    a = (gt * jax.nn.sigmoid(gt) * up).astype(BF)
    o_ref[...] += jnp.dot(a, wd_ref[...], preferred_element_type=F32)

    if normalize_out:
        @pl.when(j == nj - 1)
        def _():
            o_ref[...] = _rms(o_ref[...], go_ref[...])


def _ffn_body(x_ref, g_ref, wg_ref, wu_ref, wd_ref, go_ref, o_ref, h_ref, **kw):
    _ffn_chunk(x_ref, g_ref, wg_ref, wu_ref, wd_ref, go_ref, o_ref, h_ref, **kw)


def _ffn_cast_body(x_ref, g_ref, wg_ref, wu_ref, wd_ref, go_ref, o_ref, wgb_ref, wub_ref, wdb_ref, h_ref, **kw):
    wgb_ref[...] = wg_ref[...].astype(BF)
    wub_ref[...] = wu_ref[...].astype(BF)
    wdb_ref[...] = (0.5 * wd_ref[...]).astype(BF)
    _ffn_chunk(x_ref, g_ref, wgb_ref, wub_ref, wdb_ref, go_ref, o_ref, h_ref, **kw)


def _ffn_cast(x, g, wg, wu, wd, out_gain, *, layer, normalize_out, tf=256):
    tm = x.shape[0]
    nj = D_FF // tf
    vmem = 4 * tm * D * 4 + tm * D * 2 + 6 * D * tf * 4 + 9 * D * tf * 2 + 5 * tm * tf * 4
    wcol = lambda dt: (pl.BlockSpec((None, D, tf), lambda i, j: (layer, 0, j)) if dt == F32
                       else pl.BlockSpec((None, D, tf), lambda i, j: (0, 0, j)))
    return pl.pallas_call(
        functools.partial(_ffn_cast_body, nj=nj, normalize_out=normalize_out),
        grid=(1, nj),
        in_specs=[
            pl.BlockSpec((tm, D), lambda i, j: (0, 0)),
            pl.BlockSpec((1, D), lambda i, j: (0, 0)),
            wcol(F32), wcol(F32),
            pl.BlockSpec((None, tf, D), lambda i, j: (layer, j, 0)),
            pl.BlockSpec((1, D), lambda i, j: (0, 0)),
        ],
        out_specs=[
            pl.BlockSpec((tm, D), lambda i, j: (0, 0)),
            wcol(BF), wcol(BF),
            pl.BlockSpec((None, tf, D), lambda i, j: (0, j, 0)),
        ],
        out_shape=[
            jax.ShapeDtypeStruct((tm, D), F32),
            jax.ShapeDtypeStruct((1, D, D_FF), BF),
            jax.ShapeDtypeStruct((1, D, D_FF), BF),
            jax.ShapeDtypeStruct((1, D_FF, D), BF),
        ],
        scratch_shapes=[pltpu.VMEM((tm, D), BF)],
        compiler_params=_params(("arbitrary", "arbitrary"), vmem),
        name="ffn_cast",
    )(x, g, wg, wu, wd, out_gain)


def _ffn(x, g, wg, wu, wd_half, out_gain, *, layer, normalize_out, tm=1024, tf=512):
    R = x.shape[0]
    tm = min(tm, R)
    nj = D_FF // tf
    vmem = 4 * tm * D * 4 + tm * D * 2 + 6 * D * tf * 2 + 5 * tm * tf * 4
    return pl.pallas_call(
        functools.partial(_ffn_body, nj=nj, normalize_out=normalize_out),
        grid=(R // tm, nj),
        in_specs=[
            pl.BlockSpec((tm, D), lambda i, j: (i, 0)),
            pl.BlockSpec((1, D), lambda i, j: (0, 0)),
            pl.BlockSpec((None, D, tf), lambda i, j: (layer, 0, j)),
            pl.BlockSpec((None, D, tf), lambda i, j: (layer, 0, j)),
            pl.BlockSpec((None, tf, D), lambda i, j: (layer, j, 0)),
            pl.BlockSpec((1, D), lambda i, j: (0, 0)),
        ],
        out_specs=pl.BlockSpec((tm, D), lambda i, j: (i, 0)),
        out_shape=jax.ShapeDtypeStruct((R, D), F32),
        scratch_shapes=[pltpu.VMEM((tm, D), BF)],
        compiler_params=_params(("arbitrary", "arbitrary"), vmem),
        name="ffn",
    )(x, g, wg, wu, wd_half, out_gain)


def _mm_res_body(a_ref, w_ref, x_ref, o_ref):
    o_ref[...] = x_ref[...] + jnp.dot(a_ref[...], w_ref[...], preferred_element_type=F32)


def _mm_res(a, w, x, *, tm=512):
    R = x.shape[0]
    vmem = 2 * tm * D * 2 + 2 * D * D * 2 + 5 * tm * D * 4
    return pl.pallas_call(
        _mm_res_body,
        grid=(R // tm,),
        in_specs=[
            pl.BlockSpec((tm, D), lambda i: (i, 0)),
            pl.BlockSpec((D, D), lambda i: (0, 0)),
            pl.BlockSpec((tm, D), lambda i: (i, 0)),
        ],
        out_specs=pl.BlockSpec((tm, D), lambda i: (i, 0)),
        out_shape=jax.ShapeDtypeStruct((R, D), F32),
        compiler_params=_params(("arbitrary",), vmem),
        name="mm_res",
    )(a, w, x)


def _pool_body(x_ref, g_ref, hist_ref, w_ref, sc_ref, o_ref, st_ref, xe_ref, se_ref, *, tm, pos0):
    t = pl.program_id(1)
    n = POOL_PAD + tm
    lo = POOL_LEAD

    @pl.when(t == 0)
    def _():
        xe_ref[0:lo, :] = jnp.zeros((lo, D), F32)
        se_ref[0:lo, :] = jnp.zeros((lo, POOL_GW), F32)
        xe_ref[lo:lo + POOL_PAD, :] = hist_ref[0]

    xf = x_ref[...]
    h = _rms(xf, g_ref[...])
    xe_ref[lo + POOL_PAD:lo + n, :] = h
    row = lax.broadcasted_iota(jnp.int32, (tm, POOL_GW), 0)
    pos1 = (row + (pos0 + 1) + t * tm).astype(F32)
    for gi, w in enumerate(POOL_WINDOWS):
        c0 = gi * POOL_GW
        src, cols = xe_ref, slice(c0, c0 + POOL_GW)
        d = 1
        while d < w:
            se_ref[lo:lo + n, :] = src[lo:lo + n, cols] + src[lo - d:lo - d + n, cols]
            src, cols = se_ref, slice(0, POOL_GW)
            d *= 2
        s = se_ref[lo + POOL_PAD:lo + n, :]
        hg = h[:, c0:c0 + POOL_GW]
        cnt = jnp.minimum(pos1, float(w))
        pooled = s / cnt - hg
        y = jnp.dot(pooled.astype(BF), w_ref[gi], preferred_element_type=F32)
        o_ref[:, c0:c0 + POOL_GW] = xf[:, c0:c0 + POOL_GW] + y * sc_ref[:, c0:c0 + POOL_GW]
    tail = xe_ref[lo + tm:lo + n, :]
    st_ref[0] = tail
    xe_ref[lo:lo + POOL_PAD, :] = tail


def _pool_mixer(x, g, hist, w, scale, *, B, T, tm, pos0):
    nt = T // tm
    vmem = 4 * tm * D * 4 + (tm + POOL_PAD) * D * 4 + 2 * 4 * POOL_GW * POOL_GW * 2 + 6 * tm * D * 4
    return pl.pallas_call(
        functools.partial(_pool_body, tm=tm, pos0=pos0),
        grid=(B, nt),
        in_specs=[
            pl.BlockSpec((tm, D), lambda b, t: (b * nt + t, 0)),
            pl.BlockSpec((1, D), lambda b, t: (0, 0)),
            pl.BlockSpec((1, POOL_PAD, D), lambda b, t: (b, 0, 0)),
            pl.BlockSpec((len(POOL_WINDOWS), POOL_GW, POOL_GW), lambda b, t: (0, 0, 0)),
            pl.BlockSpec((1, D), lambda b, t: (0, 0)),
        ],
        out_specs=[
            pl.BlockSpec((tm, D), lambda b, t: (b * nt + t, 0)),
            pl.BlockSpec((1, POOL_PAD, D), lambda b, t: (b, 0, 0)),
        ],
        out_shape=[jax.ShapeDtypeStruct((B * T, D), F32), jax.ShapeDtypeStruct((B, POOL_PAD, D), F32)],
        scratch_shapes=[pltpu.VMEM((POOL_LEAD + POOL_PAD + tm, D), F32),
                        pltpu.VMEM((POOL_LEAD + POOL_PAD + tm, POOL_GW), F32)],
        compiler_params=_params(("arbitrary", "arbitrary"), vmem),
        name="pool_mixer",
    )(x, g, hist, w, scale)


def _lru_body(gate_ref, xr_ref, cst_ref, h0_ref, cw_ref, cb_ref, wa_ref, ba_ref, wi_ref, bi_ref, lam_ref,
              y_ref, cso_ref, ho_ref, xe_ref, a_ref, u_ref, hc_ref, *, tm):
    t = pl.program_id(1)

    @pl.when(t == 0)
    def _():
        xe_ref[0:CONV_PAD, :] = cst_ref[0]
        hc_ref[...] = h0_ref[0]

    xe_ref[CONV_PAD:CONV_PAD + tm, :] = xr_ref[...]
    base = CONV_PAD - (CONV_W - 1)
    xc = cb_ref[...] + xe_ref[base:base + tm, :] * cw_ref[0:1, :]
    for j in range(1, CONV_W):
        xc = xc + xe_ref[base + j:base + j + tm, :] * cw_ref[j:j + 1, :]
    tail = xe_ref[tm:tm + CONV_PAD, :]
    cso_ref[0] = tail
    xe_ref[0:CONV_PAD, :] = tail

    xcb = xc.astype(BF)
    nlam = -lam_ref[...]
    sp = jnp.maximum(nlam, 0.0) + jnp.log1p(jnp.exp(-jnp.abs(nlam)))
    for n in range(LRU_BLOCKS):
        blk = slice(n * LRU_BW, (n + 1) * LRU_BW)
        ra = jnp.dot(xcb[:, blk], wa_ref[n], preferred_element_type=F32) + ba_ref[:, blk]
        ia = jnp.dot(xcb[:, blk], wi_ref[n], preferred_element_type=F32) + bi_ref[:, blk]
        ig = 0.5 * jnp.tanh(0.5 * ia) + 0.5
        log_a = (-0.5 * LRU_C * sp[:, blk]) * (jnp.tanh(0.5 * ra) + 1.0)
        a_ref[:, blk] = jnp.exp(log_a)
        th = jnp.tanh(log_a)
        one_minus_a2 = -2.0 * th / (1.0 - th)
        u_ref[:, blk] = jnp.sqrt(one_minus_a2) * ig * xc[:, blk]

    rowi = lax.broadcasted_iota(jnp.int32, (8, D), 0)

    def group(gi, carry):
        r0 = pl.multiple_of(gi * 8, 8)
        a8 = a_ref[pl.ds(r0, 8), :]
        u8 = u_ref[pl.ds(r0, 8), :]
        for d in (1, 2, 4):
            keep = rowi >= d
            u8 = jnp.where(keep, a8 * pltpu.roll(u8, d, axis=0) + u8, u8)
            a8 = jnp.where(keep, a8 * pltpu.roll(a8, d, axis=0), a8)
        hs8 = a8 * carry + u8
        u_ref[pl.ds(r0, 8), :] = hs8
        return hs8[7:8, :]

    carry = lax.fori_loop(0, tm // 8, group, hc_ref[...])
    hc_ref[...] = carry
    ho_ref[0] = carry
    y_ref[...] = (u_ref[...] * jax.nn.gelu(gate_ref[...])).astype(BF)


def _lru_core(gate, xr, cst, h0, cw, cb, wa, ba, wi, bi, lam, *, B, T, tm):
    nt = T // tm
    vmem = 4 * tm * D * 4 + 2 * tm * D * 2 + (3 * tm + CONV_PAD) * D * 4 + 4 * 8 * LRU_BW * LRU_BW * 2 + 6 * tm * D * 4
    vec = pl.BlockSpec((1, D), lambda b, t: (0, 0))
    wblk = pl.BlockSpec((LRU_BLOCKS, LRU_BW, LRU_BW), lambda b, t: (0, 0, 0))
    return pl.pallas_call(
        functools.partial(_lru_body, tm=tm),
        grid=(B, nt),
        in_specs=[
            pl.BlockSpec((tm, D), lambda b, t: (b * nt + t, 0)),
            pl.BlockSpec((tm, D), lambda b, t: (b * nt + t, 0)),
            pl.BlockSpec((1, CONV_PAD, D), lambda b, t: (b, 0, 0)),
            pl.BlockSpec((1, 1, D), lambda b, t: (b, 0, 0)),
            pl.BlockSpec((CONV_W, D), lambda b, t: (0, 0)),
            vec, wblk, vec, wblk, vec, vec,
        ],
        out_specs=[
            pl.BlockSpec((tm, D), lambda b, t: (b * nt + t, 0)),
            pl.BlockSpec((1, CONV_PAD, D), lambda b, t: (b, 0, 0)),
            pl.BlockSpec((1, 1, D), lambda b, t: (b, 0, 0)),
        ],
        out_shape=[
            jax.ShapeDtypeStruct((B * T, D), BF),
            jax.ShapeDtypeStruct((B, CONV_PAD, D), F32),
            jax.ShapeDtypeStruct((B, 1, D), F32),
        ],
        scratch_shapes=[
            pltpu.VMEM((tm + CONV_PAD, D), F32),
            pltpu.VMEM((tm, D), F32),
            pltpu.VMEM((tm, D), F32),
            pltpu.VMEM((1, D), F32),
        ],
        compiler_params=_params(("arbitrary", "arbitrary"), vmem),
        name="rglru_core",
    )(gate, xr, cst, h0, cw, cb, wa, ba, wi, bi, lam)


def _proj_call(body, x, g, w, jcol, extra, extra_specs, out_shape, out_specs, scratch, *, tm, vmem, name):
    R = x.shape[0]
    return pl.pallas_call(
        body,
        grid=(R // tm,),
        in_specs=[
            pl.BlockSpec((tm, D), lambda i: (i, 0)),
            pl.BlockSpec((1, D), lambda i: (0, 0)),
            pl.BlockSpec((D, D), lambda i: (0, jcol), pipeline_mode=pl.Buffered(1)),
            *extra_specs,
        ],
        out_specs=out_specs,
        out_shape=out_shape,
        scratch_shapes=scratch,
        compiler_params=_params(("arbitrary",), vmem),
        name=name,
    )(x, g, w, *extra)


def _normed_dot(x_ref, g_ref, w_ref):
    h = _rms(x_ref[...], g_ref[...]).astype(BF)
    return h, jnp.dot(h, w_ref[...], preferred_element_type=F32)


def _store_heads_interleaved(ref, y, tm):
    for hh in range(FOX_HEADS):
        ref[pl.ds(hh, tm, stride=FOX_HEADS), :] = y[:, hh * FOX_HD:(hh + 1) * FOX_HD]


def _fox_q_body(x_ref, g_ref, w_ref, q_ref, *, qscale):
    _, y = _normed_dot(x_ref, g_ref, w_ref)
    yq = (y * qscale).astype(BF)
    for hh in range(FOX_HEADS):
        q_ref[hh] = yq[:, hh * FOX_HD:(hh + 1) * FOX_HD]


def _fox_k_body(x_ref, g_ref, w_ref, wf_ref, wft_ref, bf_ref, bft_ref,
                k_ref, ka_ref, lf_ref, cc_ref, cr_ref, carc_ref, carr_ref, *, tm, seg, carry):
    h, y = _normed_dot(x_ref, g_ref, w_ref)
    lf = _log_sigmoid(jnp.dot(h, wf_ref[...], preferred_element_type=F32) + bf_ref[...])
    lft = _log_sigmoid(_nt_dot(wft_ref[...], h) + bft_ref[...])
    lf_ref[...] = lf
    cc = _sum_dot_left(_tri(tm, "le", seg), lf)
    cr = _sum_dot_right(lft, _tri(tm, "ge", seg))
    if carry:
        @pl.when(pl.program_id(0) == 0)
        def _():
            carc_ref[...] = jnp.zeros_like(carc_ref)
            carr_ref[...] = jnp.zeros_like(carr_ref)

        cc = cc + carc_ref[...]
        cr = cr + carr_ref[...]
        carc_ref[...] = cc[tm - 1:tm, :]
        carr_ref[...] = cr[:, tm - 1:tm]
    cc_ref[...] = cc
    cr_ref[...] = cr

    _store_heads_interleaved(k_ref, y, tm)
    yb = y.astype(BF)
    pieces = jnp.concatenate(_split3_bf16(cc * (-LOG2E)), axis=1)
    r = lax.broadcasted_iota(jnp.int32, (3 * FOX_HEADS, D), 0)
    c = lax.broadcasted_iota(jnp.int32, (3 * FOX_HEADS, D), 1)
    sel = jnp.logical_and(c // FOX_HD == r % FOX_HEADS, c % FOX_HD == r // FOX_HEADS)
    aug = jnp.dot(pieces, jnp.where(sel, 1.0, 0.0).astype(BF), preferred_element_type=F32).astype(BF)
    for hh in range(FOX_HEADS):
        ka_ref[hh, :, 0:FOX_HD] = yb[:, hh * FOX_HD:(hh + 1) * FOX_HD]
        ka_ref[hh, :, FOX_HD:2 * FOX_HD] = aug[:, hh * FOX_HD:(hh + 1) * FOX_HD]


def _fox_v_body(x_ref, g_ref, w_ref, v_ref, vt_ref, *, tm):
    _, y = _normed_dot(x_ref, g_ref, w_ref)
    _store_heads_interleaved(v_ref, y, tm)
    for hh in range(FOX_HEADS):
        vt_ref[hh] = y[:, hh * FOX_HD:(hh + 1) * FOX_HD].T.astype(BF)


def _fox_proj(x, g, w, wf, wft, bf, bft, *, tm, seg, carry):
    R = x.shape[0]
    H = FOX_HEADS
    qscale = (FOX_HD ** -0.5) * LOG2E
    base = 2 * tm * D * 4 + D * D * 2 + 2 * tm * D * 4 + tm * D * 2
    const = lambda shape: pl.BlockSpec(shape, lambda i: (0,) * len(shape))
    native = pl.BlockSpec((tm * H, FOX_HD), lambda i: (i, 0))
    qh = _proj_call(
        functools.partial(_fox_q_body, qscale=qscale), x, g, w, 0, (), (),
        jax.ShapeDtypeStruct((H, R, FOX_HD), BF), pl.BlockSpec((H, tm, FOX_HD), lambda i: (0, i, 0)), (),
        tm=tm, vmem=base + 3 * tm * D * 2, name="fox_proj_q")
    k, ka, lf, cc, cr = _proj_call(
        functools.partial(_fox_k_body, tm=tm, seg=seg, carry=carry), x, g, w, 1,
        (wf, wft, bf, bft), (const((D, H)), const((H, D)), const((1, H)), const((H, 1))),
        [
            jax.ShapeDtypeStruct((R * H, FOX_HD), F32),
            jax.ShapeDtypeStruct((H, R, 2 * FOX_HD), BF),
            jax.ShapeDtypeStruct((R, H), F32),
            jax.ShapeDtypeStruct((R, H), F32),
            jax.ShapeDtypeStruct((H, R), F32),
        ],
        [
            native,
            pl.BlockSpec((H, tm, 2 * FOX_HD), lambda i: (0, i, 0)),
            pl.BlockSpec((tm, H), lambda i: (i, 0)),
            pl.BlockSpec((tm, H), lambda i: (i, 0)),
            pl.BlockSpec((H, tm), lambda i: (0, i)),
        ],
        [pltpu.VMEM((1, H), F32), pltpu.VMEM((H, 1), F32)],
        tm=tm, vmem=base + 2 * tm * D * 4 + 5 * tm * D * 2 + 3 * tm * tm * 4 + 4 * tm * 128 * 4, name="fox_proj_k")
    v, vt = _proj_call(
        functools.partial(_fox_v_body, tm=tm), x, g, w, 2, (), (),
        [jax.ShapeDtypeStruct((R * H, FOX_HD), F32), jax.ShapeDtypeStruct((H, FOX_HD, R), BF)],
        [native, pl.BlockSpec((H, FOX_HD, tm), lambda i: (0, 0, i))], (),
        tm=tm, vmem=base + 2 * tm * D * 4 + 3 * tm * D * 2 + tm * D * 4, name="fox_proj_v")
    return qh, k, v, ka, vt, lf, cc, cr


def _softmax_step(s_biased, cq, pv, m_ref, l_ref, acc_ref):
    m_old = m_ref[...]
    m_new = jnp.maximum(m_old, jnp.max(s_biased, axis=-1, keepdims=True) + cq)
    p = jnp.exp2(s_biased - (m_new - cq))
    alpha = jnp.exp2(m_old - m_new)
    l_ref[...] = alpha * l_ref[...] + jnp.sum(p, axis=-1, keepdims=True)
    acc_ref[...] = alpha * acc_ref[...] + pv(p.astype(BF))
    m_ref[...] = m_new


def _fox_attn_body(q_ref, ka_ref, vt_ref, cr_ref, o_ref, qt_ref, m_ref, l_ref, acc_ref, sa_ref, sb_ref, *, tq, tk):
    h = pl.program_id(0)
    qi = pl.program_id(1)
    q0 = pl.multiple_of(qi * tq, tq)
    qt_ref[0:FOX_HD, :] = q_ref[0].astype(F32).T.astype(BF)
    sub = lax.broadcasted_iota(jnp.int32, (FOX_HD, tq), 0)
    qt_ref[FOX_HD:2 * FOX_HD, :] = jnp.where(sub < 3, 1.0, 0.0).astype(BF)
    cq = cr_ref[pl.ds(h, 1), pl.ds(q0, tq)] * LOG2E
    m_ref[...] = jnp.full_like(m_ref, NEG)
    l_ref[...] = jnp.zeros_like(l_ref)
    acc_ref[...] = jnp.zeros_like(acc_ref)

    def scores(t, st_ref):
        k0 = pl.multiple_of(t * tk, tk)
        st_ref[...] = jnp.dot(ka_ref[0, pl.ds(k0, tk), :], qt_ref[...], preferred_element_type=F32)

    def fold(t, st_ref, diag=None):
        k0 = pl.multiple_of(t * tk, tk)
        st = st_ref[...]
        if diag is not None:
            kj = lax.broadcasted_iota(jnp.int32, (tk, tq), 0) + diag * tk
            qcol = lax.broadcasted_iota(jnp.int32, (tk, tq), 1)
            st = jnp.where(kj <= qcol, st, NEG)
        m_old = m_ref[...]
        m_new = jnp.maximum(m_old, jnp.max(st, axis=0, keepdims=True) + cq)
        p = jnp.exp2(st - (m_new - cq))
        alpha = jnp.exp2(m_old - m_new)
        l_ref[...] = alpha * l_ref[...] + jnp.sum(p, axis=0, keepdims=True)
        acc_ref[...] = alpha * acc_ref[...] + jnp.dot(
            vt_ref[0, :, pl.ds(k0, tk)], p.astype(BF), preferred_element_type=F32)
        m_ref[...] = m_new

    scores(0, sa_ref)

    def quad(u, _):
        t = 4 * u
        scores(t + 1, sb_ref)
        fold(t, sa_ref)
        scores(t + 2, sa_ref)
        fold(t + 1, sb_ref)
        scores(t + 3, sb_ref)
        fold(t + 2, sa_ref)
        scores(t + 4, sa_ref)
        fold(t + 3, sb_ref)
        return 0

    lax.fori_loop(0, qi // 2, quad, 0)
    td = 2 * qi

    @pl.when(qi % 2 == 1)
    def _():
        scores(td - 1, sb_ref)
        fold(td - 2, sa_ref)
        scores(td, sa_ref)
        fold(td - 1, sb_ref)

    scores(td + 1, sb_ref)
    fold(td, sa_ref, diag=0)
    fold(td + 1, sb_ref, diag=1)
    o_ref[...] = (acc_ref[...] / l_ref[...]).T.astype(BF)


def _fox_attn(qh, ka, vt, cr, *, T, tq=1024):
    H = FOX_HEADS
    tk = tq // 2
    vmem = 2 * T * 2 * FOX_HD * 2 + 2 * T * FOX_HD * 2 + 2 * H * T * 4 + 8 * tk * tq * 4 + 16 * tq * FOX_HD * 4
    return pl.pallas_call(
        functools.partial(_fox_attn_body, tq=tq, tk=tk),
        grid=(H, T // tq),
        in_specs=[
            pl.BlockSpec((1, tq, FOX_HD), lambda h, i: (h, i, 0)),
            pl.BlockSpec((1, T, 2 * FOX_HD), lambda h, i: (h, 0, 0)),
            pl.BlockSpec((1, FOX_HD, T), lambda h, i: (h, 0, 0)),
            pl.BlockSpec((H, T), lambda h, i: (0, 0)),
        ],
        out_specs=pl.BlockSpec((tq, FOX_HD), lambda h, i: (i, h)),
        out_shape=jax.ShapeDtypeStruct((T, D), BF),
        scratch_shapes=[pltpu.VMEM((2 * FOX_HD, tq), BF), pltpu.VMEM((1, tq), F32), pltpu.VMEM((1, tq), F32),
                        pltpu.VMEM((FOX_HD, tq), F32), pltpu.VMEM((tk, tq), F32), pltpu.VMEM((tk, tq), F32)],
        compiler_params=_params(("arbitrary", "arbitrary"), vmem),
        name="fox_attention",
    )(qh, ka, vt, cr)


def _suffix_body(x_ref, o_ref, car_ref, *, tc):
    @pl.when(pl.program_id(1) == 0)
    def _():
        car_ref[...] = jnp.zeros_like(car_ref)

    x = x_ref[0]
    o_ref[0] = _sum_dot_right(x, _tri(tc, "gt")) + car_ref[...]
    car_ref[...] = car_ref[...] + jnp.sum(x, axis=1, keepdims=True)


def _suffix_sum(x, *, tc=512):
    B, H, S = x.shape
    nc = S // tc
    return pl.pallas_call(
        functools.partial(_suffix_body, tc=tc),
        grid=(B, nc),
        in_specs=[pl.BlockSpec((1, H, tc), lambda b, t: (b, 0, nc - 1 - t))],
        out_specs=pl.BlockSpec((1, H, tc), lambda b, t: (b, 0, nc - 1 - t)),
        out_shape=jax.ShapeDtypeStruct((B, H, S), F32),
        scratch_shapes=[pltpu.VMEM((H, 1), F32)],
        compiler_params=_params(("arbitrary", "arbitrary"), 8 * tc * tc * 4),
        name="suffix_sum",
    )(x)


def _fox_decode_body(q_ref, ka_ref, vt_ref, kc_ref, vc_ref, dsuf_ref, ec_ref, er_ref, o_ref,
                     m_ref, l_ref, acc_ref, *, nt, tq, ts):
    b = pl.program_id(0)
    t = pl.program_id(1)
    H = FOX_HEADS

    @pl.when(t == 0)
    def _():
        m_ref[...] = jnp.full_like(m_ref, NEG)
        l_ref[...] = jnp.zeros_like(l_ref)
        acc_ref[...] = jnp.zeros_like(acc_ref)

    ec = ec_ref[...] * LOG2E
    dsuf = dsuf_ref[0] * LOG2E
    s_all = [_nt_dot(q_ref[hh], kc_ref[0, pl.ds(hh, ts, stride=H), :].astype(BF)) + dsuf[hh:hh + 1, :]
             for hh in range(H)]
    for hh in range(H):
        vt = vc_ref[0, pl.ds(hh, ts, stride=H), :].astype(BF)
        _softmax_step(s_all[hh], ec[:, hh:hh + 1], lambda p, vt=vt: jnp.dot(p, vt, preferred_element_type=F32),
                      m_ref.at[hh], l_ref.at[hh], acc_ref.at[hh])

    @pl.when(t == nt - 1)
    def _():
        n_new = er_ref.shape[1]
        er = er_ref[...] * LOG2E
        r = lax.broadcasted_iota(jnp.int32, (tq, n_new), 0)
        c = lax.broadcasted_iota(jnp.int32, (tq, n_new), 1) - b * tq
        valid = jnp.logical_and(c >= 0, c <= r)
        for hh in range(H):
            s = _nt_dot(q_ref[hh], ka_ref[hh][:, 0:FOX_HD]) - er[hh:hh + 1, :]
            s = jnp.where(valid, s, NEG)
            _softmax_step(s, ec[:, hh:hh + 1], lambda p, hh=hh: _nt_dot(p, vt_ref[hh]),
                          m_ref.at[hh], l_ref.at[hh], acc_ref.at[hh])
            o_ref[:, hh * FOX_HD:(hh + 1) * FOX_HD] = (acc_ref[hh] / l_ref[hh]).astype(BF)


def _fox_decode(qh, ka, vt, kc, vc, dsuf, ec, er, *, B, tq, ts=512):
    H = FOX_HEADS
    S = kc.shape[1] // H
    R = B * tq
    nt = S // ts
    vmem = (4 * ts * D * 4 + 2 * H * R * 3 * FOX_HD * 2 + 2 * H * tq * FOX_HD * 2 + 3 * H * tq * 128 * 4
            + 4 * ts * FOX_HD * 2 + 8 * tq * max(ts, R) * 4)
    return pl.pallas_call(
        functools.partial(_fox_decode_body, nt=nt, tq=tq, ts=ts),
        grid=(B, nt),
        in_specs=[
            pl.BlockSpec((H, tq, FOX_HD), lambda b, t: (0, b, 0)),
            pl.BlockSpec((H, R, 2 * FOX_HD), lambda b, t: (0, 0, 0)),
            pl.BlockSpec((H, FOX_HD, R), lambda b, t: (0, 0, 0)),
            pl.BlockSpec((1, ts * H, FOX_HD), lambda b, t: (b, t, 0)),
            pl.BlockSpec((1, ts * H, FOX_HD), lambda b, t: (b, t, 0)),
            pl.BlockSpec((1, H, ts), lambda b, t: (b, 0, t)),
            pl.BlockSpec((tq, H), lambda b, t: (b, 0)),
            pl.BlockSpec((H, R), lambda b, t: (0, 0)),
        ],
        out_specs=pl.BlockSpec((tq, D), lambda b, t: (b, 0)),
        out_shape=jax.ShapeDtypeStruct((R, D), BF),
        scratch_shapes=[pltpu.VMEM((H, tq, 1), F32), pltpu.VMEM((H, tq, 1), F32), pltpu.VMEM((H, tq, FOX_HD), F32)],
        compiler_params=_params(("arbitrary", "arbitrary"), vmem),
        name="fox_decode_attention",
    )(qh, ka, vt, kc, vc, dsuf, ec, er)


def _mlstm_qk_body(x_ref, g_ref, w_ref, wg_ref, wgt_ref, bg_ref, bgt_ref, q_ref, k_ref, gc_ref, gr_ref, *, qscale):
    h, y = _normed_dot(x_ref, g_ref, w_ref)
    zc = jnp.dot(h, wg_ref[...], preferred_element_type=F32) + bg_ref[...]
    lane = lax.broadcasted_iota(jnp.int32, zc.shape, 1)
    gc_ref[...] = jnp.where(lane < ML_HEADS, zc, _log_sigmoid(zc))
    zr = _nt_dot(wgt_ref[...], h) + bgt_ref[...]
    sub = lax.broadcasted_iota(jnp.int32, zr.shape, 0)
    gr_ref[...] = jnp.where(sub < ML_HEADS, zr, _log_sigmoid(zr))
    q_ref[...] = (y[:, :ML_QK] * qscale).astype(BF)
    k_ref[...] = y[:, ML_QK:]


def _proj_bf16_body(x_ref, g_ref, w_ref, o_ref):
    o_ref[...] = _normed_dot(x_ref, g_ref, w_ref)[1].astype(BF)


def _proj_f32_body(x_ref, g_ref, w_ref, o_ref):
    o_ref[...] = _normed_dot(x_ref, g_ref, w_ref)[1]


def _proj_plain(x, g, w, jcol, dtype, *, tm=512, name):
    R = x.shape[0]
    body = _proj_bf16_body if dtype == BF else _proj_f32_body
    vmem = 2 * tm * D * 4 + D * D * 2 + 2 * tm * D * 4 + tm * D * 2 + 2 * tm * D * 4
    return _proj_call(body, x, g, w, jcol, (), (), jax.ShapeDtypeStruct((R, D), dtype),
                      pl.BlockSpec((tm, D), lambda i: (i, 0)), (), tm=tm, vmem=vmem, name=name)


def _mlstm_proj(x, g, w, wg, wgt, bg, bgt, *, tm=512):
    R = x.shape[0]
    G = 2 * ML_HEADS
    const = lambda shape: pl.BlockSpec(shape, lambda i: (0,) * len(shape))
    vmem = 2 * tm * D * 4 + D * D * 2 + 2 * tm * D * 4 + tm * D * 2 + 2 * tm * D * 4
    q, k, gc, gr = _proj_call(
        functools.partial(_mlstm_qk_body, qscale=ML_DK ** -0.5), x, g, w, 0,
        (wg, wgt, bg, bgt), (const((D, G)), const((G, D)), const((1, G)), const((G, 1))),
        [
            jax.ShapeDtypeStruct((R, ML_QK), BF),
            jax.ShapeDtypeStruct((R, ML_QK), F32),
            jax.ShapeDtypeStruct((R, G), F32),
            jax.ShapeDtypeStruct((G, R), F32),
        ],
        [
            pl.BlockSpec((tm, ML_QK), lambda i: (i, 0)),
            pl.BlockSpec((tm, ML_QK), lambda i: (i, 0)),
            pl.BlockSpec((tm, G), lambda i: (i, 0)),
            pl.BlockSpec((G, tm), lambda i: (0, i)),
        ],
        (), tm=tm, vmem=vmem, name="mlstm_proj_qk")
    v = _proj_plain(x, g, w, 1, BF, tm=tm, name="mlstm_proj_v")
    o = _proj_plain(x, g, w, 2, F32, tm=tm, name="mlstm_proj_o")
    return q, k, v, o, gc, gr


def _mlstm_body(q_ref, k_ref, v_ref, o_ref, gc_ref, gr_ref, gn_ref, c0_ref, n0_ref, m0_ref,
                y_ref, co_ref, no_ref, mo_ref, c_ref, n_ref, m_ref, *, L):
    t = pl.program_id(1)

    @pl.when(t == 0)
    def _():
        c_ref[...] = c0_ref[0]
        n_ref[...] = n0_ref[0]
        m_ref[...] = m0_ref[0]

    gc = gc_ref[...]
    gr = gr_ref[0]
    ra = lax.broadcasted_iota(jnp.int32, (L, L), 0)
    cb = lax.broadcasted_iota(jnp.int32, (L, L), 1)
    causal = cb <= ra
    bcs = _sum_dot_left(_tri(L, "le"), gc)
    brs = _sum_dot_right(gr, _tri(L, "ge"))
    for hh in range(ML_HEADS):
        b_c = bcs[:, ML_HEADS + hh:ML_HEADS + hh + 1]
        g_c = gc[:, hh:hh + 1] - b_c
        g_r = gr[hh:hh + 1, :] - brs[ML_HEADS + hh:ML_HEADS + hh + 1, :]
        m_h = m_ref[hh][:, 0:1]
        am = jnp.where(causal, b_c + g_r, NEG)
        mt = jnp.maximum(b_c + m_h, jnp.max(am, axis=-1, keepdims=True))
        d = jnp.exp(am - mt)
        qh = q_ref[:, hh * ML_DK:(hh + 1) * ML_DK]
        kf = k_ref[:, hh * ML_DK:(hh + 1) * ML_DK]
        vh = v_ref[:, hh * ML_DV:(hh + 1) * ML_DV]
        sc = _nt_dot(qh, kf.astype(BF)) * d
        inter = jnp.exp(b_c + m_h - mt)
        c_old = c_ref[hh]
        n_old = n_ref[hh]
        num = (jnp.dot(sc.astype(BF), vh, preferred_element_type=F32)
               + inter * jnp.dot(qh, c_old.astype(BF), preferred_element_type=F32))
        den = (jnp.sum(sc, axis=-1, keepdims=True)
               + inter * jnp.sum(qh.astype(F32) * n_old, axis=-1, keepdims=True))
        hout = num / jnp.maximum(jnp.abs(den), jnp.exp(-mt))
        m_new = mt[L - 1:L, :]
        b_last = b_c[L - 1:L, :]
        decay = jnp.exp(b_last + m_h - m_new)
        wk = jnp.exp(b_last + g_c - m_new) * kf
        c_ref[hh] = decay * c_old + lax.dot_general(
            wk.astype(BF), vh, (((0,), (0,)), ((), ())), preferred_element_type=F32)
        n_ref[hh] = decay * n_old + jnp.sum(wk, axis=0, keepdims=True)
        m_ref[hh] = jnp.broadcast_to(m_new, (1, 128))
        hn = hout * lax.rsqrt(jnp.mean(hout * hout, axis=-1, keepdims=True) + EPS)
        cols = slice(hh * ML_DV, (hh + 1) * ML_DV)
        y_ref[:, cols] = (hn * gn_ref[:, cols] * jax.nn.sigmoid(o_ref[:, cols])).astype(BF)

    co_ref[0] = c_ref[...]
    no_ref[0] = n_ref[...]
    mo_ref[0] = m_ref[...]


def _mlstm_core(q, k, v, o, gc, gr, gn, c0, n0, m0, *, B, T, L):
    nt = T // L
    G = 2 * ML_HEADS
    H = ML_HEADS
    vmem = (2 * L * (ML_QK * 6 + D * 8) + 6 * H * ML_DK * ML_DV * 4 + 16 * L * L * 4 + 12 * L * ML_DV * 4
            + 4 * 1024 * 1024)
    rows = lambda w: pl.BlockSpec((L, w), lambda b, t: (b * nt + t, 0))
    st_c = pl.BlockSpec((1, H, ML_DK, ML_DV), lambda b, t: (b, 0, 0, 0))
    st_n = pl.BlockSpec((1, H, 1, ML_DK), lambda b, t: (b, 0, 0, 0))
    st_m = pl.BlockSpec((1, H, 1, 128), lambda b, t: (b, 0, 0, 0))
    return pl.pallas_call(
        functools.partial(_mlstm_body, L=L),
        grid=(B, nt),
        in_specs=[
            rows(ML_QK), rows(ML_QK), rows(D), rows(D), rows(G),
            pl.BlockSpec((1, G, L), lambda b, t: (b, 0, t)),
            pl.BlockSpec((1, D), lambda b, t: (0, 0)),
            st_c, st_n, st_m,
        ],
        out_specs=[rows(D), st_c, st_n, st_m],
        out_shape=[
            jax.ShapeDtypeStruct((B * T, D), BF),
            jax.ShapeDtypeStruct((B, H, ML_DK, ML_DV), F32),
            jax.ShapeDtypeStruct((B, H, 1, ML_DK), F32),
            jax.ShapeDtypeStruct((B, H, 1, 128), F32),
        ],
        scratch_shapes=[
            pltpu.VMEM((H, ML_DK, ML_DV), F32),
            pltpu.VMEM((H, 1, ML_DK), F32),
            pltpu.VMEM((H, 1, 128), F32),
        ],
        compiler_params=_params(("arbitrary", "arbitrary"), vmem),
        name="mlstm_core",
    )(q, k, v, o, gc, gr, gn, c0, n0, m0)


def kernel(x_prompt, x_sample, state_pool, state_lru_conv, state_lru_h, cache_fox_k, cache_fox_v, cache_fox_logf, state_mlstm_c, state_mlstm_n, state_mlstm_m, ffn1_norm, ffn1_w_gate, ffn1_w_up, ffn1_w_down, mix_norm, ffn2_norm, ffn2_w_gate, ffn2_w_up, ffn2_w_down, pool_w, pool_scale, lru_w_in, lru_conv_w, lru_conv_b, lru_w_a, lru_b_a, lru_w_i, lru_b_i, lru_lambda, lru_w_out, fox_w_qkv, fox_w_f, fox_b_f, fox_w_o, mlstm_w_in, mlstm_b_i, mlstm_b_f, mlstm_norm, mlstm_w_out, final_norm):
    BP, TP, _ = x_prompt.shape
    BS, TS, _ = x_sample.shape
    assert BP == 1 and x_prompt.shape[2] == D and x_sample.shape[2] == D
    RS = BS * TS
    past = cache_fox_k.shape[1]
    row = lambda p: p.reshape(1, -1).astype(F32)

    xp = x_prompt.reshape(TP, D)
    xs = x_sample.reshape(RS, D)

    ffn_w = {
        1: (ffn1_norm, ffn1_w_gate, ffn1_w_up, ffn1_w_down),
        2: (ffn2_norm, ffn2_w_gate, ffn2_w_up, ffn2_w_down),
    }

    def ffn(xp, xs, which, layer, last=False):
        norm, wg, wu, wd = ffn_w[which]
        g, gf = row(norm[layer]), row(final_norm)
        xs, wgb, wub, wdb = _ffn_cast(xs, g, wg.astype(F32), wu.astype(F32), wd.astype(F32), gf,
                                      layer=layer, normalize_out=last)
        return _ffn(xp, g, wgb, wub, wdb, gf, layer=0, normalize_out=last), xs

    xp, xs = ffn(xp, xs, 1, 0)
    pw = pool_w.astype(BF)
    hist_s = jnp.pad(state_pool.astype(F32), ((0, 0), (POOL_PAD - state_pool.shape[1], 0), (0, 0)))
    xp, pool_p = _pool_mixer(xp, row(mix_norm[0]), jnp.zeros((BP, POOL_PAD, D), F32), pw, row(pool_scale),
                             B=BP, T=TP, tm=512, pos0=0)
    xs, pool_s = _pool_mixer(xs, row(mix_norm[0]), hist_s, pw, row(pool_scale), B=BS, T=TS, tm=TS, pos0=past)
    pool_p = pool_p[:, 1:]
    pool_s = pool_s[:, 1:]
    xp, xs = ffn(xp, xs, 2, 0)

    xp, xs = ffn(xp, xs, 1, 1)
    w_in = lru_w_in.astype(BF)
    lru_args = (lru_conv_w.astype(F32), row(lru_conv_b), lru_w_a.astype(BF), row(lru_b_a),
                lru_w_i.astype(BF), row(lru_b_i), row(lru_lambda))
    lru_in = lambda x, jcol: _proj_plain(x, row(mix_norm[1]), w_in, jcol, F32, name="rglru_proj")
    cst_s = jnp.pad(state_lru_conv.astype(F32), ((0, 0), (CONV_PAD - (CONV_W - 1), 0), (0, 0)))
    yp, conv_p, h_p = _lru_core(lru_in(xp, 0), lru_in(xp, 1), jnp.zeros((BP, CONV_PAD, D), F32),
                                jnp.zeros((BP, 1, D), F32), *lru_args, B=BP, T=TP, tm=256)
    ys, conv_s, h_s = _lru_core(lru_in(xs, 0), lru_in(xs, 1), cst_s, state_lru_h.astype(F32).reshape(BS, 1, D),
                                *lru_args, B=BS, T=TS, tm=TS)
    w_out = lru_w_out.astype(BF)
    xp = _mm_res(yp, w_out, xp)
    xs = _mm_res(ys, w_out, xs)
    lru_conv_p, lru_conv_s = conv_p[:, CONV_PAD - (CONV_W - 1):], conv_s[:, CONV_PAD - (CONV_W - 1):]
    lru_h_p, lru_h_s = h_p.reshape(BP, D), h_s.reshape(BS, D)
    xp, xs = ffn(xp, xs, 2, 1)

    xp, xs = ffn(xp, xs, 1, 2)
    w_qkv = fox_w_qkv.astype(BF)
    wf = fox_w_f.astype(BF)
    wft = wf.T
    bf_r = fox_b_f.astype(F32).reshape(1, FOX_HEADS)
    bf_c = fox_b_f.astype(F32).reshape(FOX_HEADS, 1)
    qh_p, k_p, v_p, ka_p, vt_p, lf_p, cc_p, cr_p = _fox_proj(
        xp, row(mix_norm[2]), w_qkv, wf, wft, bf_r, bf_c, tm=512, seg=512, carry=True)
    qh_s, k_s, v_s, ka_s, vt_s, lf_s, cc_s, cr_s = _fox_proj(
        xs, row(mix_norm[2]), w_qkv, wf, wft, bf_r, bf_c, tm=RS, seg=TS, carry=False)
    o_p = _fox_attn(qh_p, ka_p, vt_p, cr_p, T=TP)
    dsuf = _suffix_sum(jnp.swapaxes(cache_fox_logf.astype(F32), 1, 2))
    o_s = _fox_decode(qh_s, ka_s, vt_s,
                      cache_fox_k.reshape(BS, past * FOX_HEADS, FOX_HD),
                      cache_fox_v.reshape(BS, past * FOX_HEADS, FOX_HD),
                      dsuf, cc_s, cr_s, B=BS, tq=TS)
    w_o = fox_w_o.astype(BF)
    xp = _mm_res(o_p, w_o, xp)
    xs = _mm_res(o_s, w_o, xs)
    fox_k_p = k_p.reshape(BP, TP, FOX_HEADS, FOX_HD)
    fox_v_p = v_p.reshape(BP, TP, FOX_HEADS, FOX_HD)
    fox_k_s = k_s.reshape(BS, TS, FOX_HEADS, FOX_HD)
    fox_v_s = v_s.reshape(BS, TS, FOX_HEADS, FOX_HD)
    fox_lf_p = lf_p.reshape(BP, TP, FOX_HEADS)
    fox_lf_s = lf_s.reshape(BS, TS, FOX_HEADS)
    xp, xs = ffn(xp, xs, 2, 2)

    xp, xs = ffn(xp, xs, 1, 3)
    n_main = 2 * ML_QK + 2 * D
    w_main = mlstm_w_in[:, :n_main].astype(BF)
    w_gates = mlstm_w_in[:, n_main:].astype(BF)
    b_gates = jnp.concatenate([mlstm_b_i, mlstm_b_f]).astype(F32)
    ml_w = (row(mix_norm[3]), w_main, w_gates, w_gates.T, b_gates.reshape(1, -1), b_gates.reshape(-1, 1))
    q_p, kk_p, vv_p, og_p, gc_p, gr_p = _mlstm_proj(xp, *ml_w)
    q_s, kk_s, vv_s, og_s, gc_s, gr_s = _mlstm_proj(xs, *ml_w)
    G = 2 * ML_HEADS
    gn = row(mlstm_norm)
    yp, ml_c_p, ml_n_p, ml_m_p = _mlstm_core(
        q_p, kk_p, vv_p, og_p, gc_p, gr_p.reshape(1, G, TP), gn,
        jnp.zeros((BP, ML_HEADS, ML_DK, ML_DV), F32), jnp.zeros((BP, ML_HEADS, 1, ML_DK), F32),
        jnp.zeros((BP, ML_HEADS, 1, 128), F32), B=BP, T=TP, L=256)
    m0_s = jnp.broadcast_to(state_mlstm_m.astype(F32)[:, :, None, None], (BS, ML_HEADS, 1, 128))
    ys, ml_c_s, ml_n_s, ml_m_s = _mlstm_core(
        q_s, kk_s, vv_s, og_s, gc_s, jnp.swapaxes(gr_s.reshape(G, BS, TS), 0, 1), gn,
        state_mlstm_c.astype(F32), state_mlstm_n.astype(F32).reshape(BS, ML_HEADS, 1, ML_DK), m0_s,
        B=BS, T=TS, L=TS)
    w_out = mlstm_w_out.astype(BF)
    xp = _mm_res(yp, w_out, xp)
    xs = _mm_res(ys, w_out, xs)
    ml_n_p, ml_n_s = ml_n_p.reshape(BP, ML_HEADS, ML_DK), ml_n_s.reshape(BS, ML_HEADS, ML_DK)
    ml_m_p, ml_m_s = ml_m_p[:, :, 0, 0], ml_m_s[:, :, 0, 0]
    xp, xs = ffn(xp, xs, 2, 3, last=True)

    y_prompt = xp.reshape(BP, TP, D)
    y_sample = xs.reshape(BS, TS, D)
    return (y_prompt, y_sample, pool_p, pool_s, lru_conv_p, lru_conv_s, lru_h_p, lru_h_s,
            fox_k_p, fox_k_s, fox_v_p, fox_v_s, fox_lf_p, fox_lf_s,
            ml_c_p, ml_c_s, ml_n_p, ml_n_s, ml_m_p, ml_m_s)
```
